```python
import math
import jax, jax.numpy as jnp
from jax import lax
import numpy as np

D_MODEL = 1024
BATCH = 16
SEQ = 2048
DEPTH = 1
DEC_BATCH = 32
DEC_SEQ = 32
PAST_LEN = 1024

CHUNK = 64

FOX_HEADS = 8
HEAD_DIM = 64
FOX_WIDTH = FOX_HEADS * HEAD_DIM
Q_BLOCK = 128
ATTN_SCALE = HEAD_DIM ** -0.5

CONV_CH = D_MODEL // 2
CONV_WIDTH = 31

N_EXPERTS = 256
TOP_K = 8
N_GROUPS = 8
TOPK_GROUPS = 4
D_EXPERT = D_MODEL // 4
D_SHARED = D_EXPERT
ROUTED_SCALE = 2.5
EXPERT_BLOCK = 128

LN_EPS = 1e-5
DN_ALPHA = (2 * DEPTH) ** 0.25
DN_BETA = (8 * DEPTH) ** -0.25

OFF_Q = 0
OFF_K = FOX_WIDTH
OFF_V = 2 * FOX_WIDTH
OFF_F = 3 * FOX_WIDTH
OFF_GLU = OFF_F + FOX_HEADS
OFF_GA = OFF_GLU + 2 * CONV_CH
OFF_GB = OFF_GA + D_MODEL
N_IN = OFF_GB + D_MODEL

kernel_name = 'fox_conformer_moe_deepnorm_stream_step'


def _layer_norm(x, g, b):
    xf = x.astype(jnp.float32)
    mu = xf.mean(-1, keepdims=True)
    var = jnp.square(xf - mu).mean(-1, keepdims=True)
    return ((xf - mu) * lax.rsqrt(var + LN_EPS) * g.astype(jnp.float32) + b.astype(jnp.float32)).astype(x.dtype)


def _heads(t):
    return t.reshape(*t.shape[:-1], FOX_HEADS, HEAD_DIM)


def _mixer_inputs(x, w_in, b_in):
    z = x @ w_in + b_in
    q = _heads(z[..., OFF_Q:OFF_K])
    k = _heads(z[..., OFF_K:OFF_V])
    v = _heads(z[..., OFF_V:OFF_F])
    logf = jax.nn.log_sigmoid(z[..., OFF_F:OFF_GLU].astype(jnp.float32))
    glu = z[..., OFF_GLU:OFF_GA]
    u = glu[..., :CONV_CH] * jax.nn.sigmoid(glu[..., CONV_CH:])
    return q, k, v, logf, u, z[..., OFF_GA:OFF_GB], z[..., OFF_GB:]


def _fox_prompt(q, k, v, logf):
    b, s = q.shape[0], q.shape[1]
    c = jnp.cumsum(logf, axis=1).transpose(0, 2, 1)
    kpos = jnp.arange(s)

    def block(i):
        s0 = i * Q_BLOCK
        qb = lax.dynamic_slice_in_dim(q, s0, Q_BLOCK, axis=1)
        cq = lax.dynamic_slice_in_dim(c, s0, Q_BLOCK, axis=2)
        logits = jnp.einsum('bqhd,bkhd->bhqk', qb, k, preferred_element_type=jnp.float32) * ATTN_SCALE
        logits = logits + cq[..., :, None] - c[..., None, :]
        qpos = s0 + jnp.arange(Q_BLOCK)
        logits = jnp.where(kpos[None, :] <= qpos[:, None], logits, -jnp.inf)
        probs = jax.nn.softmax(logits, axis=-1)
        return jnp.einsum('bhqk,bkhd->bqhd', probs.astype(v.dtype), v)

    out = lax.map(block, jnp.arange(s // Q_BLOCK))
    return out.transpose(1, 0, 2, 3, 4).reshape(b, s, FOX_WIDTH)


def _fox_sample(q, k, v, logf, ck, cv, clogf):
    t, p = q.shape[1], ck.shape[1]
    k_all = jnp.concatenate([ck.astype(k.dtype), k], axis=1)
    v_all = jnp.concatenate([cv.astype(v.dtype), v], axis=1)
    c = jnp.cumsum(jnp.concatenate([clogf.astype(jnp.float32), logf], axis=1), axis=1).transpose(0, 2, 1)
    logits = jnp.einsum('bqhd,bkhd->bhqk', q, k_all, preferred_element_type=jnp.float32) * ATTN_SCALE
    logits = logits + c[..., p:, None] - c[..., None, :]
    mask = jnp.arange(p + t)[None, :] <= p + jnp.arange(t)[:, None]
    probs = jax.nn.softmax(jnp.where(mask, logits, -jnp.inf), axis=-1)
    out = jnp.einsum('bhqk,bkhd->bqhd', probs.astype(v.dtype), v_all)
    return out.reshape(q.shape[0], t, FOX_WIDTH)


def _conv_module(u_ext, conv_w, conv_b, cln_g, cln_b, w_b, b_b):
    h = lax.conv_general_dilated(u_ext, conv_w[:, None, :].astype(u_ext.dtype), (1,), 'VALID',
                                 dimension_numbers=('NWC', 'WIO', 'NWC'), feature_group_count=CONV_CH)
    h = jax.nn.silu(_layer_norm(h + conv_b, cln_g, cln_b))
    return h @ w_b + b_b


def _merge(x, attn, conv_out, g_a, g_b, w_a, w_out, ln_g, ln_b):
    m = jax.nn.sigmoid(g_a) * (attn @ w_a) + jax.nn.sigmoid(g_b) * conv_out
    return _layer_norm(DN_ALPHA * x + m @ w_out, ln_g, ln_b)


def _swiglu(x, wg, wu, wd):
    return (jax.nn.silu(x @ wg) * (x @ wu)) @ wd


def _moe_ffn(x, w_router, b_router, w_e_gate, w_e_up, w_e_down, w_s_gate, w_s_up, w_s_down):
    lead = x.shape[:-1]
    xt = x.reshape(-1, D_MODEL)
    t = xt.shape[0]
    scores = jax.nn.sigmoid(jnp.dot(xt, w_router, preferred_element_type=jnp.float32))
    sel = scores + b_router.astype(jnp.float32)
    gscore = lax.top_k(sel.reshape(t, N_GROUPS, N_EXPERTS // N_GROUPS), 2)[0].sum(-1)
    _, gidx = lax.top_k(gscore, TOPK_GROUPS)
    gmask = jax.nn.one_hot(gidx, N_GROUPS, dtype=jnp.float32).sum(1)
    sel = jnp.where(jnp.repeat(gmask, N_EXPERTS // N_GROUPS, axis=1) > 0, sel, -jnp.inf)
    _, eidx = lax.top_k(sel, TOP_K)
    gate = jnp.take_along_axis(scores, eidx, axis=1)
    gate = gate / gate.sum(-1, keepdims=True) * ROUTED_SCALE
    n_assign = t * TOP_K
    flat_e = eidx.reshape(n_assign)
    flat_tok = jnp.arange(n_assign, dtype=jnp.int32) // TOP_K
    flat_w = gate.reshape(n_assign)
    order = jnp.argsort(flat_e)
    se = flat_e[order]
    counts = jnp.bincount(flat_e, length=N_EXPERTS)
    starts = jnp.cumsum(counts) - counts
    pcounts = (counts + EXPERT_BLOCK - 1) // EXPERT_BLOCK * EXPERT_BLOCK
    pends = jnp.cumsum(pcounts)
    pstarts = pends - pcounts
    dest = pstarts[se] + jnp.arange(n_assign, dtype=jnp.int32) - starts[se]
    n_blocks = -(-n_assign // EXPERT_BLOCK) + N_EXPERTS
    n_rows = n_blocks * EXPERT_BLOCK
    buf_tok = jnp.full((n_rows,), t, jnp.int32).at[dest].set(flat_tok[order])
    buf_w = jnp.zeros((n_rows,), jnp.float32).at[dest].set(flat_w[order])
    blk_e = jnp.minimum(jnp.searchsorted(pends, jnp.arange(n_blocks, dtype=jnp.int32) * EXPERT_BLOCK,
                                         side='right'), N_EXPERTS - 1)
    xpad = jnp.concatenate([xt, jnp.zeros((1, D_MODEL), xt.dtype)], axis=0)

    def run_block(args):
        tok, w, e = args
        yb = _swiglu(xpad[tok], w_e_gate[e], w_e_up[e], w_e_down[e])
        return yb * w[:, None].astype(yb.dtype)

    yb = lax.map(run_block, (buf_tok.reshape(n_blocks, EXPERT_BLOCK),
                             buf_w.reshape(n_blocks, EXPERT_BLOCK), blk_e))
    routed = jnp.zeros((t + 1, D_MODEL), yb.dtype).at[buf_tok].add(yb.reshape(n_rows, D_MODEL))[:t]
    out = routed.astype(xt.dtype) + _swiglu(xt, w_s_gate, w_s_up, w_s_down)
    return out.reshape(*lead, D_MODEL)


def setup_inputs(seed: int = 0) -> dict:
    key = jax.random.key(seed)
    ks = jax.random.split(key, 32)
    nrm = lambda k, shape, scale: jax.random.normal(k, shape, jnp.float32) * scale
    forget_bias = jnp.asarray(np.linspace(1.0, 4.0, FOX_HEADS), jnp.float32)
    col_scale = np.ones((N_IN,), np.float32)
    col_scale[OFF_V:OFF_F] = DN_BETA
    b_off = np.zeros((N_IN,), np.float32)
    b_off[OFF_F:OFF_GLU] = np.linspace(1.0, 4.0, FOX_HEADS)
    gain = lambda k, n: 1.0 + nrm(k, (DEPTH, n), 0.02)
    return {
        'x_prompt': nrm(ks[0], (BATCH, SEQ, D_MODEL), 1.0),
        'x_sample': nrm(ks[1], (DEC_BATCH, DEC_SEQ, D_MODEL), 1.0),
        'cache_k': nrm(ks[2], (DEPTH, DEC_BATCH, PAST_LEN, FOX_HEADS, HEAD_DIM), 1.0),
        'cache_v': nrm(ks[3], (DEPTH, DEC_BATCH, PAST_LEN, FOX_HEADS, HEAD_DIM), DN_BETA),
        'cache_logf': jax.nn.log_sigmoid(nrm(ks[4], (DEPTH, DEC_BATCH, PAST_LEN, FOX_HEADS), 1.0) + forget_bias),
        'state_conv': nrm(ks[5], (DEPTH, DEC_BATCH, CONV_WIDTH - 1, CONV_CH), 0.5),
        'w_in': nrm(ks[6], (DEPTH, D_MODEL, N_IN), D_MODEL ** -0.5) * jnp.asarray(col_scale),
        'b_in': nrm(ks[7], (DEPTH, N_IN), 0.02) + jnp.asarray(b_off),
        'conv_w': nrm(ks[8], (DEPTH, CONV_WIDTH, CONV_CH), CONV_WIDTH ** -0.5),
        'conv_b': nrm(ks[9], (DEPTH, CONV_CH), 0.02),
        'conv_ln_g': gain(ks[10], CONV_CH),
        'conv_ln_b': nrm(ks[11], (DEPTH, CONV_CH), 0.02),
        'w_a': nrm(ks[12], (DEPTH, FOX_WIDTH, D_MODEL), FOX_WIDTH ** -0.5 * DN_BETA),
        'w_b': nrm(ks[13], (DEPTH, CONV_CH, D_MODEL), CONV_CH ** -0.5 * DN_BETA),
        'b_b': nrm(ks[14], (DEPTH, D_MODEL), 0.02),
        'w_out': nrm(ks[15], (DEPTH, D_MODEL, D_MODEL), D_MODEL ** -0.5 * DN_BETA),
        'ln1_g': gain(ks[16], D_MODEL),
        'ln1_b': nrm(ks[17], (DEPTH, D_MODEL), 0.02),
        'w_router': nrm(ks[18], (DEPTH, D_MODEL, N_EXPERTS), D_MODEL ** -0.5),
        'b_router': nrm(ks[19], (DEPTH, N_EXPERTS), 0.01),
        'w_e_gate': nrm(ks[20], (DEPTH, N_EXPERTS, D_MODEL, D_EXPERT), D_MODEL ** -0.5 * DN_BETA),
        'w_e_up': nrm(ks[21], (DEPTH, N_EXPERTS, D_MODEL, D_EXPERT), D_MODEL ** -0.5 * DN_BETA),
        'w_e_down': nrm(ks[22], (DEPTH, N_EXPERTS, D_EXPERT, D_MODEL), D_EXPERT ** -0.5 * DN_BETA),
        'w_s_gate': nrm(ks[23], (DEPTH, D_MODEL, D_SHARED), D_MODEL ** -0.5 * DN_BETA),
        'w_s_up': nrm(ks[24], (DEPTH, D_MODEL, D_SHARED), D_MODEL ** -0.5 * DN_BETA),
        'w_s_down': nrm(ks[25], (DEPTH, D_SHARED, D_MODEL), D_SHARED ** -0.5 * DN_BETA),
        'ln2_g': gain(ks[26], D_MODEL),
        'ln2_b': nrm(ks[27], (DEPTH, D_MODEL), 0.02),
    }


def reference(x_prompt, x_sample, cache_k, cache_v, cache_logf, state_conv,
              w_in, b_in, conv_w, conv_b, conv_ln_g, conv_ln_b, w_a, w_b, b_b, w_out, ln1_g, ln1_b,
              w_router, b_router, w_e_gate, w_e_up, w_e_down, w_s_gate, w_s_up, w_s_down, ln2_g, ln2_b):
    hp, hs = x_prompt, x_sample
    kp, vp, fp, cp, ks_, vs_, fs_, cs_ = [], [], [], [], [], [], [], []
    for l in range(DEPTH):
        ffn = (w_router[l], b_router[l], w_e_gate[l], w_e_up[l], w_e_down[l],
               w_s_gate[l], w_s_up[l], w_s_down[l])
        conv_p = (conv_w[l], conv_b[l], conv_ln_g[l], conv_ln_b[l], w_b[l], b_b[l])
        q, k, v, logf, u, g_a, g_b = _mixer_inputs(hp, w_in[l], b_in[l])
        attn = _fox_prompt(q, k, v, logf)
        u_ext = jnp.pad(u, ((0, 0), (CONV_WIDTH - 1, 0), (0, 0)))
        mid = _merge(hp, attn, _conv_module(u_ext, *conv_p), g_a, g_b, w_a[l], w_out[l], ln1_g[l], ln1_b[l])
        hp = _layer_norm(DN_ALPHA * mid + _moe_ffn(mid, *ffn), ln2_g[l], ln2_b[l])
        kp.append(k)
        vp.append(v)
        fp.append(logf)
        cp.append(u_ext[:, -(CONV_WIDTH - 1):])
        q, k, v, logf, u, g_a, g_b = _mixer_inputs(hs, w_in[l], b_in[l])
        attn = _fox_sample(q, k, v, logf, cache_k[l], cache_v[l], cache_logf[l])
        u_ext = jnp.concatenate([state_conv[l].astype(u.dtype), u], axis=1)
        mid = _merge(hs, attn, _conv_module(u_ext, *conv_p), g_a, g_b, w_a[l], w_out[l], ln1_g[l], ln1_b[l])
        hs = _layer_norm(DN_ALPHA * mid + _moe_ffn(mid, *ffn), ln2_g[l], ln2_b[l])
        ks_.append(k)
        vs_.append(v)
        fs_.append(logf)
        cs_.append(u_ext[:, -(CONV_WIDTH - 1):])
    new_k_prompt = jnp.stack(kp)
    new_v_prompt = jnp.stack(vp)
    new_logf_prompt = jnp.stack(fp)
    new_conv_prompt = jnp.stack(cp)
    new_k_sample = jnp.stack(ks_)
    new_v_sample = jnp.stack(vs_)
    new_logf_sample = jnp.stack(fs_)
    new_conv_sample = jnp.stack(cs_)
    return (hp, hs, new_k_prompt, new_v_prompt, new_logf_prompt, new_conv_prompt,
            new_k_sample, new_v_sample, new_logf_sample, new_conv_sample)
```

```python
import functools

import jax
import jax.numpy as jnp
from jax import lax
from jax.experimental import pallas as pl
from jax.experimental.pallas import tpu as pltpu

D_MODEL = 1024
FOX_HEADS = 8
HEAD_DIM = 64
FOX_WIDTH = FOX_HEADS * HEAD_DIM
ATTN_SCALE = HEAD_DIM ** -0.5
CONV_CH = D_MODEL // 2
CONV_WIDTH = 31
N_EXPERTS = 256
TOP_K = 8
N_GROUPS = 8
GROUP_SIZE = N_EXPERTS // N_GROUPS
TOPK_GROUPS = 4
D_EXPERT = D_MODEL // 4
ROUTED_SCALE = 2.5
LN_EPS = 1e-5

OFF_K = FOX_WIDTH
OFF_V = 2 * FOX_WIDTH
OFF_F = 3 * FOX_WIDTH
OFF_GLU = OFF_F + FOX_HEADS
OFF_GA = OFF_GLU + 2 * CONV_CH
OFF_GB = OFF_GA + D_MODEL

LANES = 128
SUBLANES = 8
ROW_TILES = D_MODEL // LANES
VMEM_LIMIT = 56 * 1024 * 1024

IN_TILE = 512
ATTN_TILE = 256
MERGE_TILE = 256
CONV_SUB = 32
HIST = 32
ROUTER_TILE = 512
MOVE_TILE = 128
EXPERT_BLOCK = 256

F32 = jnp.float32
BF16 = jnp.bfloat16
NEG_INF = float("-inf")
NT_DIMS = (((1,), (1,)), ((), ()))


def _const_spec(shape):
    nd = len(shape)
    return pl.BlockSpec(shape, lambda *_: (0,) * nd, pipeline_mode=pl.Buffered(1))


def _split3(x):
    hi = x.astype(BF16)
    r1 = x - hi.astype(F32)
    mid = r1.astype(BF16)
    lo = (r1 - mid.astype(F32)).astype(BF16)
    return hi, mid, lo


def _dot(a, b):
    return jnp.dot(a, b, preferred_element_type=F32)


def _dot_nt(a, b):
    return lax.dot_general(a, b, NT_DIMS, preferred_element_type=F32)


def _exact_dot(ones_mat, x, *, ones_on_left):
    acc = None
    for part in _split3(x):
        term = _dot(ones_mat, part) if ones_on_left else _dot(part, ones_mat)
        acc = term if acc is None else acc + term
    return acc


def _layer_norm(x, g, b):
    mu = jnp.mean(x, axis=-1, keepdims=True)
    xc = x - mu
    var = jnp.mean(xc * xc, axis=-1, keepdims=True)
    return xc * lax.rsqrt(var + LN_EPS) * g + b


def _log_sigmoid(z):
    return jnp.minimum(z, 0.0) - jnp.log1p(jnp.exp(-jnp.abs(z)))


def _inproj_kernel(x_ref, w_ref, b_ref, wf_ref, bf_ref, tri_ref,
                   q_ref, k_ref, v_ref, kb_ref, vb_ref, logf_ref, c_ref, u_ref,
                   sa_ref, sb_ref, carry_ref, *, tiles_per_seq):
    i = pl.program_id(0)
    xb = x_ref[...].astype(BF16)

    def proj(c0, c1):
        return _dot(xb, w_ref[:, c0:c1]) + b_ref[:, c0:c1]

    q = proj(0, FOX_WIDTH)
    q_ref[...] = (q * ATTN_SCALE).astype(BF16)
    k = proj(FOX_WIDTH, 2 * FOX_WIDTH)
    k_ref[...] = k
    kb_ref[...] = k.astype(BF16)
    v = proj(2 * FOX_WIDTH, 3 * FOX_WIDTH)
    v_ref[...] = v
    vb_ref[...] = v.astype(BF16)

    logf = _log_sigmoid(_dot(xb, wf_ref[...]) + bf_ref[...])
    logf_ref[...] = logf[:, :FOX_HEADS]

    @pl.when(i % tiles_per_seq == 0)
    def _():
        carry_ref[...] = jnp.zeros_like(carry_ref)

    c = _exact_dot(tri_ref[...], logf, ones_on_left=True) + carry_ref[...]
    c_ref[...] = c[:, :FOX_HEADS]
    carry_ref[...] = c[IN_TILE - 1:IN_TILE, :]

    g0 = 3 * FOX_WIDTH
    glu_a = proj(g0, g0 + CONV_CH)
    glu_b = proj(g0 + CONV_CH, g0 + 2 * CONV_CH)
    u_ref[...] = glu_a * jax.nn.sigmoid(glu_b)
    g1 = g0 + 2 * CONV_CH
    sa_ref[...] = jax.nn.sigmoid(proj(g1, g1 + D_MODEL)).astype(BF16)
    sb_ref[...] = jax.nn.sigmoid(proj(g1 + D_MODEL, g1 + 2 * D_MODEL)).astype(BF16)


def _inproj(x, w_main, b_main, w_f, b_f, tri, *, tiles_per_seq):
    rows = x.shape[0]
    n_main = w_main.shape[1]
    row_spec = lambda w: pl.BlockSpec((IN_TILE, w), lambda i: (i, 0))
    out_shapes = (
        jax.ShapeDtypeStruct((rows, FOX_WIDTH), BF16),
        jax.ShapeDtypeStruct((rows, FOX_WIDTH), F32),
        jax.ShapeDtypeStruct((rows, FOX_WIDTH), F32),
        jax.ShapeDtypeStruct((rows, FOX_WIDTH), BF16),
        jax.ShapeDtypeStruct((rows, FOX_WIDTH), BF16),
        jax.ShapeDtypeStruct((rows, FOX_HEADS), F32),
        jax.ShapeDtypeStruct((rows, FOX_HEADS), F32),
        jax.ShapeDtypeStruct((rows, CONV_CH), F32),
        jax.ShapeDtypeStruct((rows, D_MODEL), BF16),
        jax.ShapeDtypeStruct((rows, D_MODEL), BF16),
    )
    out_specs = (row_spec(FOX_WIDTH),) * 5 + (row_spec(FOX_HEADS),) * 2 + (
        row_spec(CONV_CH), row_spec(D_MODEL), row_spec(D_MODEL))
    return pl.pallas_call(
        functools.partial(_inproj_kernel, tiles_per_seq=tiles_per_seq),
        grid=(rows // IN_TILE,),
        in_specs=[row_spec(D_MODEL), _const_spec((D_MODEL, n_main)), _const_spec((1, n_main)),
                  _const_spec((D_MODEL, LANES)), _const_spec((1, LANES)),
                  _const_spec((IN_TILE, IN_TILE))],
        out_specs=out_specs,
        out_shape=out_shapes,
        scratch_shapes=[pltpu.VMEM((1, LANES), F32)],
        compiler_params=pltpu.CompilerParams(
            dimension_semantics=("arbitrary",), vmem_limit_bytes=VMEM_LIMIT),
        name="inproj",
    )(x, w_main, b_main, w_f, b_f, tri)


def _attn_prompt_kernel(q_ref, k_ref, v_ref, ccol_ref, crow_ref, o_ref,
                        m_scr, l_scr, acc_scr):
    t = ATTN_TILE
    i = pl.program_id(1)
    lane = lax.broadcasted_iota(jnp.int32, (t, LANES), 1)
    row = lax.broadcasted_iota(jnp.int32, (t, t), 0)
    col = lax.broadcasted_iota(jnp.int32, (t, t), 1)
    causal = col <= row
    wide = lambda x: jnp.concatenate([x] * (t // LANES), axis=1)

    for pair in range(FOX_HEADS // 2):
        lanes = slice(pair * LANES, (pair + 1) * LANES)
        qp = q_ref[:, lanes]
        halves = []
        for hh in range(2):
            h = 2 * pair + hh
            in_head = (lane < HEAD_DIM) if hh == 0 else (lane >= HEAD_DIM)
            qm = jnp.where(in_head, qp, jnp.zeros_like(qp))
            cq = ccol_ref[:, h:h + 1]
            m_scr[...] = jnp.full(m_scr.shape, NEG_INF, F32)
            l_scr[...] = jnp.zeros_like(l_scr)
            acc_scr[...] = jnp.zeros_like(acc_scr)

            def step(j, masked):
                r0 = pl.multiple_of(j * t, t)
                kj = k_ref[pl.ds(r0, t), lanes]
                vj = v_ref[pl.ds(r0, t), lanes]
                ck = crow_ref[0, h:h + 1, pl.ds(r0, t)]
                s = _dot_nt(qm, kj) + (cq - ck)
                if masked:
                    s = jnp.where(causal, s, NEG_INF)
                m_prev = m_scr[...]
                m_new = jnp.maximum(m_prev, jnp.max(s, axis=1, keepdims=True))
                alpha = jnp.exp(m_prev - m_new)
                p = jnp.exp(s - wide(m_new))
                l_scr[...] = alpha * l_scr[...] + jnp.sum(p, axis=1, keepdims=True)
                acc_scr[...] = alpha * acc_scr[...] + _dot(p.astype(BF16), vj)
                m_scr[...] = m_new

            def body(j, carry):
                step(j, False)
                return carry

            lax.fori_loop(0, i, body, 0)
            step(i, True)
            halves.append(acc_scr[...] / l_scr[...])
        o_ref[:, lanes] = jnp.where(lane < HEAD_DIM, halves[0], halves[1]).astype(BF16)


def _attn_prompt(q, kb, vb, c_col, c_row, *, batch, seq):
    t = ATTN_TILE
    nq = seq // t
    return pl.pallas_call(
        _attn_prompt_kernel,
        grid=(batch, nq),
        in_specs=[
            pl.BlockSpec((t, FOX_WIDTH), lambda b, i: (b * nq + i, 0)),
            pl.BlockSpec((seq, FOX_WIDTH), lambda b, i: (b, 0)),
            pl.BlockSpec((seq, FOX_WIDTH), lambda b, i: (b, 0)),
            pl.BlockSpec((t, FOX_HEADS), lambda b, i: (b * nq + i, 0)),
            pl.BlockSpec((1, FOX_HEADS, seq), lambda b, i: (b, 0, 0)),
        ],
        out_specs=pl.BlockSpec((t, FOX_WIDTH), lambda b, i: (b * nq + i, 0)),
        out_shape=jax.ShapeDtypeStruct((batch * seq, FOX_WIDTH), BF16),
        scratch_shapes=[pltpu.VMEM((t, LANES), F32)] * 3,
        compiler_params=pltpu.CompilerParams(
            dimension_semantics=("arbitrary", "arbitrary"), vmem_limit_bytes=VMEM_LIMIT),
        name="attn_prompt",
    )(q, kb, vb, c_col, c_row)


def _attn_sample_kernel(q_ref, kn_ref, vn_ref, lf_ref, lft_ref, ck_ref, cv_ref, clft_ref,
                        upper_ref, o_ref, *, t, past):
    rows = FOX_HEADS * t
    lane_head = lax.broadcasted_iota(jnp.int32, (t, FOX_WIDTH), 1) // HEAD_DIM
    q = q_ref[...]
    q_stack = jnp.concatenate(
        [jnp.where(lane_head == h, q, jnp.zeros_like(q)) for h in range(FOX_HEADS)], axis=0)

    clf = clft_ref[0]
    prefix = _exact_dot(upper_ref[...], clf, ones_on_left=False)
    to_end = prefix[:, past - 1:past] - prefix
    ri = lax.broadcasted_iota(jnp.int32, (t, t), 0)
    ci = lax.broadcasted_iota(jnp.int32, (t, t), 1)
    lower = jnp.where(ci <= ri, 1.0, 0.0).astype(BF16)
    upper = jnp.where(ri <= ci, 1.0, 0.0).astype(BF16)
    cn_col = _exact_dot(lower, lf_ref[...], ones_on_left=True)
    cn_row = _exact_dot(upper, lft_ref[0], ones_on_left=False)

    stack = lambda f: jnp.concatenate([f(h) for h in range(FOX_HEADS)], axis=0)
    cn_stack = stack(lambda h: cn_col[:, h:h + 1])
    bias_c = stack(lambda h: jnp.broadcast_to(to_end[h:h + 1, :], (t, past)))
    bias_n = stack(lambda h: jnp.broadcast_to(cn_row[h:h + 1, :], (t, t)))

    kc = ck_ref[0].astype(BF16)
    vc = cv_ref[0].astype(BF16)
    s_c = _dot_nt(q_stack, kc) + bias_c + cn_stack
    s_n = _dot_nt(q_stack, kn_ref[...]) + (cn_stack - bias_n)
    tq = lax.broadcasted_iota(jnp.int32, (rows, t), 0) % t
    tk = lax.broadcasted_iota(jnp.int32, (rows, t), 1)
    s_n = jnp.where(tk <= tq, s_n, NEG_INF)
    m = jnp.maximum(jnp.max(s_c, axis=1, keepdims=True), jnp.max(s_n, axis=1, keepdims=True))
    p_c = jnp.exp(s_c - m)
    p_n = jnp.exp(s_n - m)
    denom = jnp.sum(p_c, axis=1, keepdims=True) + jnp.sum(p_n, axis=1, keepdims=True)
    o_stack = (_dot(p_c.astype(BF16), vc) + _dot(p_n.astype(BF16), vn_ref[...])) / denom
    out = jnp.zeros((t, FOX_WIDTH), F32)
    for h in range(FOX_HEADS):
        out = out + jnp.where(lane_head == h, o_stack[h * t:(h + 1) * t, :], 0.0)
    o_ref[...] = out.astype(BF16)


def _attn_sample(q, kb, vb, logf, logf_t, cache_k, cache_v, cache_logf_t, upper, *, batch, t, past):
    row_spec = lambda w: pl.BlockSpec((t, w), lambda b: (b, 0))
    return pl.pallas_call(
        functools.partial(_attn_sample_kernel, t=t, past=past),
        grid=(batch,),
        in_specs=[
            row_spec(FOX_WIDTH), row_spec(FOX_WIDTH), row_spec(FOX_WIDTH), row_spec(FOX_HEADS),
            pl.BlockSpec((1, FOX_HEADS, t), lambda b: (b, 0, 0)),
            pl.BlockSpec((1, past, FOX_WIDTH), lambda b: (b, 0, 0)),
            pl.BlockSpec((1, past, FOX_WIDTH), lambda b: (b, 0, 0)),
            pl.BlockSpec((1, FOX_HEADS, past), lambda b: (b, 0, 0)),
            _const_spec((past, past)),
        ],
        out_specs=row_spec(FOX_WIDTH),
        out_shape=jax.ShapeDtypeStruct((batch * t, FOX_WIDTH), BF16),
        compiler_params=pltpu.CompilerParams(
            dimension_semantics=("arbitrary",), vmem_limit_bytes=VMEM_LIMIT),
        name="attn_sample",
    )(q, kb, vb, logf, logf_t, cache_k, cache_v, cache_logf_t, upper)


def _merge_kernel(u_ref, hist_ref, attn_ref, sa_ref, sb_ref, x_ref,
                  cw_ref, cb_ref, cg_ref, cbeta_ref, wb_ref, bb_ref, wa_ref, wout_ref,
                  g1_ref, b1_ref, *rest, ts, tiles_per_seq, n_tiles, zero_first_hist, alpha,
                  aliased):
    if aliased:
        rest = rest[2:]
    mid_ref, mid3_ref, ue_scr, ph_scr, h_scr = rest
    i = pl.program_id(0)

    @pl.when(i >= n_tiles)
    def _():
        mid_ref[...] = jnp.zeros_like(mid_ref)
        mid3_ref[...] = jnp.zeros_like(mid3_ref)

    @pl.when(i < n_tiles)
    def _():
        _merge_tile(u_ref, hist_ref, attn_ref, sa_ref, sb_ref, x_ref, cw_ref, cb_ref, cg_ref,
                    cbeta_ref, wb_ref, bb_ref, wa_ref, wout_ref, g1_ref, b1_ref, mid_ref,
                    mid3_ref, ue_scr, ph_scr, h_scr, ts=ts, alpha=alpha,
                    zero_hist=(i % tiles_per_seq == 0) if zero_first_hist else None)


def _merge_tile(u_ref, hist_ref, attn_ref, sa_ref, sb_ref, x_ref, cw_ref, cb_ref, cg_ref,
                cbeta_ref, wb_ref, bb_ref, wa_ref, wout_ref, g1_ref, b1_ref, mid_ref, mid3_ref,
                ue_scr, ph_scr, h_scr, *, ts, alpha, zero_hist):
    hist = hist_ref[...]
    if zero_hist is not None:
        hist = jnp.where(zero_hist, 0.0, hist)
    ue_scr[0:HIST, :] = hist
    ue_scr[HIST:HIST + ts, :] = u_ref[...]
    lead = HIST - (CONV_WIDTH - 1)
    for r in range(SUBLANES):
        span = ts + (CONV_WIDTH - 1 - r) // SUBLANES * SUBLANES
        ph_scr[r, 0:span, :] = ue_scr[lead + r:lead + r + span, :]

    def conv_rows(rt, carry):
        base = pl.multiple_of(rt * CONV_SUB, CONV_SUB)
        acc = jnp.zeros((CONV_SUB, CONV_CH), F32)
        for j in range(CONV_WIDTH):
            r, a = j % SUBLANES, j // SUBLANES
            acc = acc + cw_ref[j:j + 1, :] * ph_scr[r, pl.ds(base + a * SUBLANES, CONV_SUB), :]
        h_scr[pl.ds(base, CONV_SUB), :] = acc
        return carry

    lax.fori_loop(0, ts // CONV_SUB, conv_rows, 0)
    h = _layer_norm(h_scr[...] + cb_ref[...], cg_ref[...], cbeta_ref[...])
    h = h * jax.nn.sigmoid(h)
    conv_out = _dot(h.astype(BF16), wb_ref[...]) + bb_ref[...]
    attn_out = _dot(attn_ref[...], wa_ref[...])
    m = sa_ref[...].astype(F32) * attn_out + sb_ref[...].astype(F32) * conv_out
    z = alpha * x_ref[...] + _dot(m.astype(BF16), wout_ref[...])
    mid = _layer_norm(z, g1_ref[...], b1_ref[...])
    mid_ref[...] = mid
    for j in range(ROW_TILES):
        mid3_ref[pl.ds(j, ts, stride=ROW_TILES), :] = mid[:, j * LANES:(j + 1) * LANES]


def _merge(u, hist, attn, sa, sb, x, conv_p, wa, wout, g1, b1, *, n_seq, seq, ts, hist_from_u,
           alpha, total_rows, row_offset, prev=None):
    nt = seq // ts
    n_tiles = n_seq * nt
    off = row_offset // ts
    grid_tiles = n_tiles if prev is not None else total_rows // ts
    src = lambda i: jnp.minimum(i, n_tiles - 1)
    row_spec = lambda w: pl.BlockSpec((ts, w), lambda i: (src(i), 0))
    if hist_from_u:
        per = ts // HIST
        hist_spec = pl.BlockSpec((HIST, CONV_CH), lambda i: (jnp.maximum(src(i) * per - 1, 0), 0))
    else:
        hist_spec = pl.BlockSpec((HIST, CONV_CH), lambda i: (src(i) // nt, 0))
    cw, cb, cg, cbeta, wb, bb = conv_p
    consts = [cw, cb, cg, cbeta, wb, bb, wa, wout, g1, b1]
    in_specs = [row_spec(CONV_CH), hist_spec, row_spec(FOX_WIDTH), row_spec(D_MODEL),
                row_spec(D_MODEL), row_spec(D_MODEL)] + [_const_spec(c.shape) for c in consts]
    args = [u, hist, attn, sa, sb, x] + consts
    aliases = {}
    if prev is not None:
        in_specs += [pl.BlockSpec(memory_space=pl.ANY)] * 2
        aliases = {len(args): 0, len(args) + 1: 1}
        args += list(prev)
    span = ts + (CONV_WIDTH - 1) // SUBLANES * SUBLANES
    return pl.pallas_call(
        functools.partial(_merge_kernel, ts=ts, tiles_per_seq=nt, n_tiles=n_tiles,
                          zero_first_hist=hist_from_u, alpha=alpha, aliased=prev is not None),
        grid=(grid_tiles,),
        in_specs=in_specs,
        out_specs=(pl.BlockSpec((ts, D_MODEL), lambda i: (off + i, 0)),
                   pl.BlockSpec((ts * ROW_TILES, LANES), lambda i: (off + i, 0))),
        out_shape=(jax.ShapeDtypeStruct((total_rows, D_MODEL), F32),
                   jax.ShapeDtypeStruct((total_rows * ROW_TILES, LANES), F32)),
        scratch_shapes=[pltpu.VMEM((HIST + ts, CONV_CH), F32),
                        pltpu.VMEM((SUBLANES, span, CONV_CH), F32),
                        pltpu.VMEM((ts, CONV_CH), F32)],
        input_output_aliases=aliases,
        compiler_params=pltpu.CompilerParams(
            dimension_semantics=("arbitrary",), vmem_limit_bytes=VMEM_LIMIT),
        name="merge",
    )(*args)


def _router_kernel(mid_ref, wr_hi_ref, wr_lo_ref, br_ref, before_ref,
                   eidx_ref, gate_ref, rank_ref, cnt_ref, carry_scr):
    tr = ROUTER_TILE
    i = pl.program_id(0)

    @pl.when(i == 0)
    def _():
        carry_scr[...] = jnp.zeros_like(carry_scr)

    x = mid_ref[...]
    x_hi = x.astype(BF16)
    x_lo = (x - x_hi.astype(F32)).astype(BF16)
    wr_hi = wr_hi_ref[...]
    logits = _dot_nt(wr_hi, x_hi) + _dot_nt(wr_hi, x_lo) + _dot_nt(wr_lo_ref[...], x_hi)
    scores = jax.nn.sigmoid(logits)
    sel = scores + br_ref[...]

    sel3 = sel.reshape(N_GROUPS, GROUP_SIZE, tr)
    in_group = lax.broadcasted_iota(jnp.int32, sel3.shape, 1)
    m1 = jnp.max(sel3, axis=1, keepdims=True)
    first = jnp.min(jnp.where(sel3 == m1, in_group, GROUP_SIZE), axis=1, keepdims=True)
    m2 = jnp.max(jnp.where(in_group == first, NEG_INF, sel3), axis=1, keepdims=True)
    gs = m1 + m2
    gi = lax.broadcasted_iota(jnp.int32, gs.shape, 0)
    beaten = jnp.zeros(gs.shape, F32)
    for g in range(N_GROUPS):
        other = gs[g:g + 1]
        wins = (other > gs) | ((other == gs) & (g < gi))
        beaten = beaten + jnp.where(wins, 1.0, 0.0)
    drop = jnp.where(beaten < TOPK_GROUPS, 0.0, NEG_INF)
    cur = (sel3 + drop).reshape(N_EXPERTS, tr)

    ei = lax.broadcasted_iota(jnp.int32, (N_EXPERTS, tr), 0)
    idxs, vals = [], []
    picked = jnp.zeros((N_EXPERTS, tr), F32)
    for _ in range(TOP_K):
        m = jnp.max(cur, axis=0, keepdims=True)
        idx = jnp.min(jnp.where(cur == m, ei, N_EXPERTS), axis=0, keepdims=True)
        hit = ei == idx
        vals.append(jnp.sum(jnp.where(hit, scores, 0.0), axis=0, keepdims=True))
        idxs.append(idx)
        picked = picked + jnp.where(hit, 1.0, 0.0)
        cur = jnp.where(hit, NEG_INF, cur)

    total = vals[0]
    for v in vals[1:]:
        total = total + v
    for k in range(TOP_K):
        gate_ref[k:k + 1, :] = vals[k] / total * ROUTED_SCALE
        eidx_ref[k:k + 1, :] = idxs[k]

    ahead = _dot(picked.astype(BF16), before_ref[...]) + carry_scr[:, 0:1]
    for k in range(TOP_K):
        rank = jnp.sum(jnp.where(ei == idxs[k], ahead, 0.0), axis=0, keepdims=True)
        rank_ref[k:k + 1, :] = rank.astype(jnp.int32)
    carry_scr[...] = carry_scr[...] + jnp.sum(picked, axis=1, keepdims=True)
    cnt_ref[...] = carry_scr[...]


def _router(mid, wr_hi, wr_lo, br, before):
    tr = ROUTER_TILE
    tokens = mid.shape[0]
    tok_spec = pl.BlockSpec((TOP_K, tr), lambda i: (0, i))
    return pl.pallas_call(
        _router_kernel,
        grid=(tokens // tr,),
        in_specs=[pl.BlockSpec((tr, D_MODEL), lambda i: (i, 0)),
                  _const_spec((N_EXPERTS, D_MODEL)), _const_spec((N_EXPERTS, D_MODEL)),
                  _const_spec((N_EXPERTS, 1)), _const_spec((tr, tr))],
        out_specs=(tok_spec, tok_spec, tok_spec,
                   pl.BlockSpec((N_EXPERTS, LANES), lambda i: (0, 0))),
        out_shape=(jax.ShapeDtypeStruct((TOP_K, tokens), jnp.int32),
                   jax.ShapeDtypeStruct((TOP_K, tokens), F32),
                   jax.ShapeDtypeStruct((TOP_K, tokens), jnp.int32),
                   jax.ShapeDtypeStruct((N_EXPERTS, LANES), F32)),
        scratch_shapes=[pltpu.VMEM((N_EXPERTS, LANES), F32)],
        compiler_params=pltpu.CompilerParams(
            dimension_semantics=("arbitrary",), vmem_limit_bytes=VMEM_LIMIT),
        name="router",
    )(mid, wr_hi, wr_lo, br, before)


def _row_slice(ref, row):
    return ref.at[pl.ds(pl.multiple_of(row * ROW_TILES, ROW_TILES), ROW_TILES), :]


def _dispatch_kernel(dest_ref, rows_ref, xs_ref, sem):
    tt = MOVE_TILE

    def issue(t, carry):
        src = _row_slice(rows_ref, t)
        for k in range(TOP_K):
            pltpu.make_async_copy(src, _row_slice(xs_ref, dest_ref[t * TOP_K + k]), sem).start()
        return carry

    lax.fori_loop(0, tt, issue, 0)
    for _ in range(TOP_K):
        pltpu.make_async_copy(rows_ref, xs_ref.at[pl.ds(0, tt * ROW_TILES), :], sem).wait()


def _dispatch(dest_flat, rows3):
    tt = MOVE_TILE
    tokens = rows3.shape[0] // ROW_TILES
    n_rows = tokens * TOP_K
    return pl.pallas_call(
        _dispatch_kernel,
        grid=(tokens // tt,),
        in_specs=[pl.BlockSpec((tt * TOP_K,), lambda i: (i,), memory_space=pltpu.SMEM),
                  pl.BlockSpec((tt * ROW_TILES, LANES), lambda i: (i, 0))],
        out_specs=pl.BlockSpec(memory_space=pl.ANY),
        out_shape=jax.ShapeDtypeStruct((n_rows * ROW_TILES, LANES), F32),
        scratch_shapes=[pltpu.SemaphoreType.DMA(())],
        compiler_params=pltpu.CompilerParams(dimension_semantics=("arbitrary",)),
        name="dispatch",
    )(dest_flat, rows3)


def _expert_kernel(vblk_ref, vexp_ref, vlo_ref, vhi_ref, xs_ref, wg_ref, wu_ref, wd_ref,
                   ys_ref):
    bm = EXPERT_BLOCK
    v = pl.program_id(0)
    lo, hi = vlo_ref[v], vhi_ref[v]

    @pl.when(hi > lo)
    def _():
        x = jnp.concatenate(
            [xs_ref[pl.ds(j, bm, stride=ROW_TILES), :] for j in range(ROW_TILES)], axis=1)
        xb = x.astype(BF16)
        g = _dot(xb, wg_ref[0].astype(BF16))
        u = _dot(xb, wu_ref[0].astype(BF16))
        h = (g * jax.nn.sigmoid(g) * u).astype(BF16)
        y = _dot(h, wd_ref[0].astype(BF16))
        row = lax.broadcasted_iota(jnp.int32, (bm, LANES), 0)
        mine = (row >= lo) & (row < hi)
        first = jnp.logical_or(v == 0, vblk_ref[jnp.maximum(v - 1, 0)] != vblk_ref[v])

        @pl.when(first)
        def _():
            for j in range(ROW_TILES):
                ys_ref[pl.ds(j, bm, stride=ROW_TILES), :] = jnp.where(
                    mine, y[:, j * LANES:(j + 1) * LANES], 0.0)

        @pl.when(jnp.logical_not(first))
        def _():
            for j in range(ROW_TILES):
                tile = pl.ds(j, bm, stride=ROW_TILES)
                ys_ref[tile, :] = jnp.where(mine, y[:, j * LANES:(j + 1) * LANES], ys_ref[tile, :])


def _experts(vblk, vexp, vlo, vhi, xs, wg, wu, wd):
    bm = EXPERT_BLOCK
    n_visits = vblk.shape[0]
    row_map = lambda v, vb, ve, lo, hi: (vb[v], 0)
    w_map = lambda v, vb, ve, lo, hi: (ve[v], 0, 0)
    grid_spec = pltpu.PrefetchScalarGridSpec(
        num_scalar_prefetch=4,
        grid=(n_visits,),
        in_specs=[
            pl.BlockSpec((bm * ROW_TILES, LANES), row_map),
            pl.BlockSpec((1, D_MODEL, D_EXPERT), w_map),
            pl.BlockSpec((1, D_MODEL, D_EXPERT), w_map),
            pl.BlockSpec((1, D_EXPERT, D_MODEL), w_map),
        ],
        out_specs=pl.BlockSpec((bm * ROW_TILES, LANES), row_map),
    )
    return pl.pallas_call(
        _expert_kernel,
        grid_spec=grid_spec,
        out_shape=jax.ShapeDtypeStruct(xs.shape, F32),
        compiler_params=pltpu.CompilerParams(
            dimension_semantics=("arbitrary",), vmem_limit_bytes=VMEM_LIMIT),
        name="experts",
    )(vblk, vexp, vlo, vhi, xs, wg, wu, wd)


def _combine_kernel(dest_ref, ys_ref, gate_ref, mid_ref, wsg_ref, wsu_ref, wsd_ref,
                    g2_ref, b2_ref, out_ref, buf, sem, *, alpha):
    tt = MOVE_TILE

    def slot(k, t):
        return _row_slice(buf, k * tt + t)

    def issue(t, carry):
        for k in range(TOP_K):
            pltpu.make_async_copy(_row_slice(ys_ref, dest_ref[t * TOP_K + k]), slot(k, t),
                                  sem).start()
        return carry

    lax.fori_loop(0, tt, issue, 0)
    pltpu.make_async_copy(ys_ref.at[pl.ds(0, TOP_K * tt * ROW_TILES), :], buf, sem).wait()

    mid = mid_ref[...]
    xb = mid.astype(BF16)
    g = _dot(xb, wsg_ref[...])
    h = (g * jax.nn.sigmoid(g) * _dot(xb, wsu_ref[...])).astype(BF16)
    acc = alpha * mid + _dot(h, wsd_ref[...])
    gates = gate_ref[...]
    routed = jnp.zeros((tt, D_MODEL), F32)
    for k in range(TOP_K):
        yk = jnp.concatenate(
            [buf[pl.ds(k * tt * ROW_TILES + j, tt, stride=ROW_TILES), :]
             for j in range(ROW_TILES)], axis=1)
        routed = routed + gates[:, k:k + 1] * yk
    out_ref[...] = _layer_norm(acc + routed, g2_ref[...], b2_ref[...])


def _combine(dest_flat, ys, gate, mid, wsg, wsu, wsd, g2, b2, *, alpha):
    tt = MOVE_TILE
    tokens = mid.shape[0]
    consts = [wsg, wsu, wsd, g2, b2]
    return pl.pallas_call(
        functools.partial(_combine_kernel, alpha=alpha),
        grid=(tokens // tt,),
        in_specs=[pl.BlockSpec((tt * TOP_K,), lambda i: (i,), memory_space=pltpu.SMEM),
                  pl.BlockSpec(memory_space=pl.ANY),
                  pl.BlockSpec((tt, TOP_K), lambda i: (i, 0)),
                  pl.BlockSpec((tt, D_MODEL), lambda i: (i, 0))]
                 + [_const_spec(c.shape) for c in consts],
        out_specs=pl.BlockSpec((tt, D_MODEL), lambda i: (i, 0)),
        out_shape=jax.ShapeDtypeStruct((tokens, D_MODEL), F32),
        scratch_shapes=[pltpu.VMEM((TOP_K * tt * ROW_TILES, LANES), F32),
                        pltpu.SemaphoreType.DMA(())],
        compiler_params=pltpu.CompilerParams(
            dimension_semantics=("arbitrary",), vmem_limit_bytes=VMEM_LIMIT),
        name="combine",
    )(dest_flat, ys, gate, mid, *consts)


def _tri(n, *, lower):
    r = lax.broadcasted_iota(jnp.int32, (n, n), 0)
    c = lax.broadcasted_iota(jnp.int32, (n, n), 1)
    return jnp.where((c <= r) if lower else (r <= c), 1.0, 0.0).astype(BF16)


def _moe(mid, rows3, w_router, b_router, w_e_gate, w_e_up, w_e_down, wsg, wsu, wsd, g2, b2, alpha):
    tokens = mid.shape[0]
    bm = EXPERT_BLOCK
    wr_t = w_router.T
    wr_hi = wr_t.astype(BF16)
    wr_lo = (wr_t - wr_hi.astype(F32)).astype(BF16)
    r = lax.broadcasted_iota(jnp.int32, (ROUTER_TILE, ROUTER_TILE), 0)
    c = lax.broadcasted_iota(jnp.int32, (ROUTER_TILE, ROUTER_TILE), 1)
    before = jnp.where(r < c, 1.0, 0.0).astype(BF16)
    eidx, gate, rank, cnt = _router(mid, wr_hi, wr_lo, b_router.reshape(N_EXPERTS, 1), before)

    i32 = lambda a: a.astype(jnp.int32)
    counts = i32(cnt[:, 0])
    ends = jnp.cumsum(counts)
    starts = ends - counts
    first_blk = starts // bm
    n_vis = jnp.where(counts > 0, (ends - 1) // bm - first_blk + 1, 0)
    vis_ends = jnp.cumsum(n_vis)
    vis_starts = vis_ends - n_vis
    max_visits = tokens * TOP_K // bm + N_EXPERTS - 1
    v = jnp.arange(max_visits, dtype=jnp.int32)
    vc = jnp.minimum(v, vis_ends[-1] - 1)
    vexp = jnp.minimum(jnp.sum(i32(vis_ends[None, :] <= vc[:, None]), axis=1), N_EXPERTS - 1)
    vblk = first_blk[vexp] + vc - vis_starts[vexp]
    vlo = jnp.maximum(starts[vexp], vblk * bm) - vblk * bm
    vhi = jnp.minimum(ends[vexp], (vblk + 1) * bm) - vblk * bm
    vhi = jnp.where(v < vis_ends[-1], vhi, vlo)
    dest = i32((jnp.take(starts, eidx) + rank).T.reshape(tokens * TOP_K))

    xs = _dispatch(dest, rows3)
    ys = _experts(i32(vblk), vexp, i32(vlo), i32(vhi), xs, w_e_gate, w_e_up, w_e_down)
    return _combine(dest, ys, gate.T, mid, wsg, wsu, wsd, g2, b2, alpha=alpha)


def kernel(x_prompt, x_sample, cache_k, cache_v, cache_logf, state_conv, w_in, b_in, conv_w,
           conv_b, conv_ln_g, conv_ln_b, w_a, w_b, b_b, w_out, ln1_g, ln1_b, w_router, b_router,
           w_e_gate, w_e_up, w_e_down, w_s_gate, w_s_up, w_s_down, ln2_g, ln2_b):
    depth = w_in.shape[0]
    alpha = float((2 * depth) ** 0.25)
    batch, seq, _ = x_prompt.shape
    dbatch, dseq, _ = x_sample.shape
    past = cache_k.shape[2]
    rows_p, rows_s = batch * seq, dbatch * dseq
    total = rows_p + rows_s
    assert seq % IN_TILE == 0 and rows_s % IN_TILE == 0 and seq % MERGE_TILE == 0
    assert total % ROUTER_TILE == 0 and dseq == HIST and rows_p % dseq == 0

    hp = x_prompt.reshape(rows_p, D_MODEL)
    hs = x_sample.reshape(rows_s, D_MODEL)
    tri_in = _tri(IN_TILE, lower=True)
    upper_past = _tri(past, lower=False)
    row2 = lambda a: a.reshape(1, -1)
    outs = {n: [] for n in ("kp", "vp", "fp", "cp", "ks", "vs", "fs", "cs")}

    for l in range(depth):
        w = w_in[l]
        b = b_in[l]
        main_cols = lambda a: jnp.concatenate([a[..., :OFF_F], a[..., OFF_GLU:]], axis=-1)
        w_main = main_cols(w).astype(BF16)
        b_main = row2(main_cols(b))
        w_f = jnp.pad(w[:, OFF_F:OFF_GLU], ((0, 0), (0, LANES - FOX_HEADS))).astype(BF16)
        b_f = row2(jnp.pad(b[OFF_F:OFF_GLU], (0, LANES - FOX_HEADS)))
        cw = jnp.pad(conv_w[l], ((0, 1), (0, 0)))
        conv_p = (cw, row2(conv_b[l]), row2(conv_ln_g[l]), row2(conv_ln_b[l]),
                  w_b[l].astype(BF16), row2(b_b[l]))
        wa, wout = w_a[l].astype(BF16), w_out[l].astype(BF16)
        g1, b1 = row2(ln1_g[l]), row2(ln1_b[l])

        q, k, v, kb, vb, logf, c, u, sa, sb = _inproj(
            hp, w_main, b_main, w_f, b_f, tri_in, tiles_per_seq=seq // IN_TILE)
        c_row = c.reshape(batch, seq, FOX_HEADS).transpose(0, 2, 1)
        attn = _attn_prompt(q, kb, vb, c, c_row, batch=batch, seq=seq)
        mid, rows3 = _merge(u, u, attn, sa, sb, hp, conv_p, wa, wout, g1, b1,
                            n_seq=batch, seq=seq, ts=MERGE_TILE, hist_from_u=True, alpha=alpha,
                            total_rows=total, row_offset=0)
        outs["kp"].append(k.reshape(batch, seq, FOX_HEADS, HEAD_DIM))
        outs["vp"].append(v.reshape(batch, seq, FOX_HEADS, HEAD_DIM))
        outs["fp"].append(logf.reshape(batch, seq, FOX_HEADS))
        outs["cp"].append(u.reshape(batch, seq, CONV_CH)[:, seq - (CONV_WIDTH - 1):])

        q, k, v, kb, vb, logf, _, u, sa, sb = _inproj(
            hs, w_main, b_main, w_f, b_f, tri_in, tiles_per_seq=1)
        logf_t = logf.reshape(dbatch, dseq, FOX_HEADS).transpose(0, 2, 1)
        attn = _attn_sample(
            q, kb, vb, logf, logf_t, cache_k[l].reshape(dbatch, past, FOX_WIDTH),
            cache_v[l].reshape(dbatch, past, FOX_WIDTH), cache_logf[l].transpose(0, 2, 1),
            upper_past, batch=dbatch, t=dseq, past=past)
        hist = jnp.pad(state_conv[l], ((0, 0), (HIST - (CONV_WIDTH - 1), 0), (0, 0)))
        mid, rows3 = _merge(u, hist.reshape(dbatch * HIST, CONV_CH), attn, sa, sb, hs, conv_p,
                            wa, wout, g1, b1, n_seq=dbatch, seq=dseq, ts=dseq, hist_from_u=False,
                            alpha=alpha, total_rows=total, row_offset=rows_p, prev=(mid, rows3))
        outs["ks"].append(k.reshape(dbatch, dseq, FOX_HEADS, HEAD_DIM))
        outs["vs"].append(v.reshape(dbatch, dseq, FOX_HEADS, HEAD_DIM))
        outs["fs"].append(logf.reshape(dbatch, dseq, FOX_HEADS))
        u3 = u.reshape(dbatch, dseq, CONV_CH)
        u_ext = jnp.concatenate([state_conv[l], u3], axis=1)
        outs["cs"].append(u_ext[:, -(CONV_WIDTH - 1):])

        y = _moe(mid, rows3, w_router[l], b_router[l], w_e_gate[l], w_e_up[l], w_e_down[l],
                 w_s_gate[l].astype(BF16), w_s_up[l].astype(BF16), w_s_down[l].astype(BF16),
                 row2(ln2_g[l]), row2(ln2_b[l]), alpha)
        hp, hs = y[:rows_p], y[rows_p:]

    st = lambda n: jnp.stack(outs[n])
    return (hp.reshape(batch, seq, D_MODEL), hs.reshape(dbatch, dseq, D_MODEL),
            st("kp"), st("vp"), st("fp"), st("cp"), st("ks"), st("vs"), st("fs"), st("cs"))
```

```python
import functools

import jax
import jax.numpy as jnp
from jax import lax
from jax.experimental import pallas as pl
from jax.experimental.pallas import tpu as pltpu

D_MODEL = 1024
FOX_HEADS = 8
HEAD_DIM = 64
FOX_WIDTH = FOX_HEADS * HEAD_DIM
ATTN_SCALE = HEAD_DIM ** -0.5
LOG2E = 1.4426950408889634
CONV_CH = D_MODEL // 2
CONV_WIDTH = 31
N_EXPERTS = 256
TOP_K = 8
N_GROUPS = 8
GROUP_SIZE = N_EXPERTS // N_GROUPS
TOPK_GROUPS = 4
D_EXPERT = D_MODEL // 4
ROUTED_SCALE = 2.5
LN_EPS = 1e-5

OFF_K = FOX_WIDTH
OFF_V = 2 * FOX_WIDTH
OFF_F = 3 * FOX_WIDTH
OFF_GLU = OFF_F + FOX_HEADS
OFF_GA = OFF_GLU + 2 * CONV_CH
OFF_GB = OFF_GA + D_MODEL

LANES = 128
SUBLANES = 8
ROW_TILES = D_MODEL // LANES
VMEM_LIMIT = 56 * 1024 * 1024

IN_TILE = 512
ATTN_TILE = 256
MERGE_TILE = 256
CONV_SUB = 32
HIST = 32
ROUTER_TILE = 512
MOVE_TILE = 128
DEST_TILE = 1024
EXPERT_BLOCK = 256

F32 = jnp.float32
BF16 = jnp.bfloat16
NEG_INF = float("-inf")
NT_DIMS = (((1,), (1,)), ((), ()))


def _const_spec(shape):
    nd = len(shape)
    return pl.BlockSpec(shape, lambda *_: (0,) * nd, pipeline_mode=pl.Buffered(1))


def _split3(x):
    hi = x.astype(BF16)
    r1 = x - hi.astype(F32)
    mid = r1.astype(BF16)
    lo = (r1 - mid.astype(F32)).astype(BF16)
    return hi, mid, lo


def _dot(a, b):
    return jnp.dot(a, b, preferred_element_type=F32)


def _dot_nt(a, b):
    return lax.dot_general(a, b, NT_DIMS, preferred_element_type=F32)


def _exact_dot(ones_mat, x, *, ones_on_left):
    acc = None
    for part in _split3(x):
        term = _dot(ones_mat, part) if ones_on_left else _dot(part, ones_mat)
        acc = term if acc is None else acc + term
    return acc


def _layer_norm(x, g, b):
    mu = jnp.mean(x, axis=-1, keepdims=True)
    xc = x - mu
    var = jnp.mean(xc * xc, axis=-1, keepdims=True)
    return xc * lax.rsqrt(var + LN_EPS) * g + b


def _log_sigmoid(z):
    return jnp.minimum(z, 0.0) - jnp.log1p(jnp.exp(-jnp.abs(z)))


def _inproj_kernel(x_ref, w_ref, b_ref, wf_ref, bf_ref, tri_ref,
                   q_ref, k_ref, v_ref, kb_ref, vb_ref, logf_ref, c_ref, u_ref,
                   sa_ref, sb_ref, carry_ref, *, tiles_per_seq):
    i = pl.program_id(0)
    xb = x_ref[...].astype(BF16)

    def proj(c0, c1):
        return _dot(xb, w_ref[:, c0:c1]) + b_ref[:, c0:c1]

    q = proj(0, FOX_WIDTH)
    q_ref[...] = (q * (ATTN_SCALE * LOG2E)).astype(BF16)
    k = proj(FOX_WIDTH, 2 * FOX_WIDTH)
    k_ref[...] = k
    kb_ref[...] = k.astype(BF16)
    v = proj(2 * FOX_WIDTH, 3 * FOX_WIDTH)
    v_ref[...] = v
    vb_ref[...] = v.astype(BF16)

    logf = _log_sigmoid(_dot(xb, wf_ref[...]) + bf_ref[...])
    logf_ref[...] = logf[:, :FOX_HEADS]

    @pl.when(i % tiles_per_seq == 0)
    def _():
        carry_ref[...] = jnp.zeros_like(carry_ref)

    c = _exact_dot(tri_ref[...], logf, ones_on_left=True) + carry_ref[...]
    c_ref[...] = c[:, :FOX_HEADS]
    carry_ref[...] = c[IN_TILE - 1:IN_TILE, :]

    g0 = 3 * FOX_WIDTH
    glu_a = proj(g0, g0 + CONV_CH)
    glu_b = proj(g0 + CONV_CH, g0 + 2 * CONV_CH)
    u_ref[...] = glu_a * jax.nn.sigmoid(glu_b)
    g1 = g0 + 2 * CONV_CH
    sa_ref[...] = jax.nn.sigmoid(proj(g1, g1 + D_MODEL)).astype(BF16)
    sb_ref[...] = jax.nn.sigmoid(proj(g1 + D_MODEL, g1 + 2 * D_MODEL)).astype(BF16)


def _inproj(x, w_main, b_main, w_f, b_f, tri, *, tiles_per_seq):
    rows = x.shape[0]
    n_main = w_main.shape[1]
    row_spec = lambda w: pl.BlockSpec((IN_TILE, w), lambda i: (i, 0))
    out_shapes = (
        jax.ShapeDtypeStruct((rows, FOX_WIDTH), BF16),
        jax.ShapeDtypeStruct((rows, FOX_WIDTH), F32),
        jax.ShapeDtypeStruct((rows, FOX_WIDTH), F32),
        jax.ShapeDtypeStruct((rows, FOX_WIDTH), BF16),
        jax.ShapeDtypeStruct((rows, FOX_WIDTH), BF16),
        jax.ShapeDtypeStruct((rows, FOX_HEADS), F32),
        jax.ShapeDtypeStruct((rows, FOX_HEADS), F32),
        jax.ShapeDtypeStruct((rows, CONV_CH), F32),
        jax.ShapeDtypeStruct((rows, D_MODEL), BF16),
        jax.ShapeDtypeStruct((rows, D_MODEL), BF16),
    )
    out_specs = (row_spec(FOX_WIDTH),) * 5 + (row_spec(FOX_HEADS),) * 2 + (
        row_spec(CONV_CH), row_spec(D_MODEL), row_spec(D_MODEL))
    return pl.pallas_call(
        functools.partial(_inproj_kernel, tiles_per_seq=tiles_per_seq),
        grid=(rows // IN_TILE,),
        in_specs=[row_spec(D_MODEL), _const_spec((D_MODEL, n_main)), _const_spec((1, n_main)),
                  _const_spec((D_MODEL, LANES)), _const_spec((1, LANES)),
                  _const_spec((IN_TILE, IN_TILE))],
        out_specs=out_specs,
        out_shape=out_shapes,
        scratch_shapes=[pltpu.VMEM((1, LANES), F32)],
        compiler_params=pltpu.CompilerParams(
            dimension_semantics=("arbitrary",), vmem_limit_bytes=VMEM_LIMIT),
        name="inproj",
    )(x, w_main, b_main, w_f, b_f, tri)


def _attn_prompt_kernel(q_ref, k_ref, v_ref, crow_ref, o_ref, qm_scr, m_scr, acc_scr):
    t = ATTN_TILE
    i = pl.program_id(1)
    lane = lax.broadcasted_iota(jnp.int32, (t, LANES), 1)
    row = lax.broadcasted_iota(jnp.int32, (t, t), 0)
    col = lax.broadcasted_iota(jnp.int32, (t, t), 1)
    causal = col <= row
    wide = lambda x: jnp.concatenate([x] * (t // LANES), axis=1)
    pair_lanes = lambda pair: slice(pair * LANES, (pair + 1) * LANES)

    for h in range(FOX_HEADS):
        qp = q_ref[:, pair_lanes(h // 2)]
        in_head = (lane < HEAD_DIM) if h % 2 == 0 else (lane >= HEAD_DIM)
        qm_scr[h] = jnp.where(in_head, qp, jnp.zeros_like(qp))
        m_scr[h] = jnp.full((t, LANES), NEG_INF, F32)
        acc_scr[h] = jnp.zeros((t, 2 * LANES), F32)
    ones = jnp.ones((t, LANES), BF16)

    def step(j, masked):
        r0 = pl.multiple_of(j * t, t)
        ck = crow_ref[0, :, pl.ds(r0, t)] * LOG2E
        for pair in range(FOX_HEADS // 2):
            kj = k_ref[pl.ds(r0, t), pair_lanes(pair)]
            vj = jnp.concatenate([v_ref[pl.ds(r0, t), pair_lanes(pair)], ones], axis=1)
            for h in (2 * pair, 2 * pair + 1):
                s = _dot_nt(qm_scr[h], kj) - ck[h:h + 1, :]
                if masked:
                    s = jnp.where(causal, s, NEG_INF)
                m_prev = m_scr[h]
                m_new = jnp.maximum(m_prev, jnp.max(s, axis=1, keepdims=True))
                alpha = jnp.exp2(m_prev - m_new)
                p = jnp.exp2(s - wide(m_new))
                acc_scr[h] = wide(alpha) * acc_scr[h] + _dot(p.astype(BF16), vj)
                m_scr[h] = m_new

    def body(j, carry):
        step(j, False)
        return carry

    lax.fori_loop(0, i, body, 0)
    step(i, True)
    for pair in range(FOX_HEADS // 2):
        o0, o1 = (acc_scr[h, :, :LANES] / acc_scr[h, :, LANES:] for h in (2 * pair, 2 * pair + 1))
        o_ref[:, pair_lanes(pair)] = jnp.where(lane < HEAD_DIM, o0, o1).astype(BF16)


def _attn_prompt(q, kb, vb, c_row, *, batch, seq):
    t = ATTN_TILE
    nq = seq // t
    return pl.pallas_call(
        _attn_prompt_kernel,
        grid=(batch, nq),
        in_specs=[
            pl.BlockSpec((t, FOX_WIDTH), lambda b, i: (b * nq + i, 0)),
            pl.BlockSpec((seq, FOX_WIDTH), lambda b, i: (b, 0)),
            pl.BlockSpec((seq, FOX_WIDTH), lambda b, i: (b, 0)),
            pl.BlockSpec((1, FOX_HEADS, seq), lambda b, i: (b, 0, 0)),
        ],
        out_specs=pl.BlockSpec((t, FOX_WIDTH), lambda b, i: (b * nq + i, 0)),
        out_shape=jax.ShapeDtypeStruct((batch * seq, FOX_WIDTH), BF16),
        scratch_shapes=[pltpu.VMEM((FOX_HEADS, t, LANES), BF16),
                        pltpu.VMEM((FOX_HEADS, t, LANES), F32),
                        pltpu.VMEM((FOX_HEADS, t, 2 * LANES), F32)],
        compiler_params=pltpu.CompilerParams(
            dimension_semantics=("arbitrary", "arbitrary"), vmem_limit_bytes=VMEM_LIMIT),
        name="attn_prompt",
    )(q, kb, vb, c_row)


def _attn_sample_kernel(q_ref, kn_ref, vn_ref, lf_ref, lft_ref, ck_ref, cv_ref, clft_ref,
                        upper_ref, o_ref, *, t, past):
    rows = FOX_HEADS * t
    lane_head = lax.broadcasted_iota(jnp.int32, (t, FOX_WIDTH), 1) // HEAD_DIM
    q = q_ref[...]
    q_stack = jnp.concatenate(
        [jnp.where(lane_head == h, q, jnp.zeros_like(q)) for h in range(FOX_HEADS)], axis=0)

    clf = clft_ref[0]
    prefix = _exact_dot(upper_ref[...], clf, ones_on_left=False)
    to_end = prefix[:, past - 1:past] - prefix
    ri = lax.broadcasted_iota(jnp.int32, (t, t), 0)
    ci = lax.broadcasted_iota(jnp.int32, (t, t), 1)
    lower = jnp.where(ci <= ri, 1.0, 0.0).astype(BF16)
    upper = jnp.where(ri <= ci, 1.0, 0.0).astype(BF16)
    cn_col = _exact_dot(lower, lf_ref[...], ones_on_left=True)
    cn_row = _exact_dot(upper, lft_ref[0], ones_on_left=False)

    stack = lambda f: jnp.concatenate([f(h) for h in range(FOX_HEADS)], axis=0)
    cn_stack = stack(lambda h: cn_col[:, h:h + 1])
    bias_c = stack(lambda h: jnp.broadcast_to(to_end[h:h + 1, :], (t, past)))
    bias_n = stack(lambda h: jnp.broadcast_to(cn_row[h:h + 1, :], (t, t)))

    kc = ck_ref[0].astype(BF16)
    vc = cv_ref[0].astype(BF16)
    s_c = _dot_nt(q_stack, kc) + (bias_c + cn_stack) * LOG2E
    s_n = _dot_nt(q_stack, kn_ref[...]) + (cn_stack - bias_n) * LOG2E
    tq = lax.broadcasted_iota(jnp.int32, (rows, t), 0) % t
    tk = lax.broadcasted_iota(jnp.int32, (rows, t), 1)
    s_n = jnp.where(tk <= tq, s_n, NEG_INF)
    m = jnp.maximum(jnp.max(s_c, axis=1, keepdims=True), jnp.max(s_n, axis=1, keepdims=True))
    p_c = jnp.exp2(s_c - m)
    p_n = jnp.exp2(s_n - m)
    denom = jnp.sum(p_c, axis=1, keepdims=True) + jnp.sum(p_n, axis=1, keepdims=True)
    o_stack = (_dot(p_c.astype(BF16), vc) + _dot(p_n.astype(BF16), vn_ref[...])) / denom
    out = jnp.zeros((t, FOX_WIDTH), F32)
    for h in range(FOX_HEADS):
        out = out + jnp.where(lane_head == h, o_stack[h * t:(h + 1) * t, :], 0.0)
    o_ref[...] = out.astype(BF16)


def _attn_sample(q, kb, vb, logf, logf_t, cache_k, cache_v, cache_logf_t, upper, *, batch, t, past):
    row_spec = lambda w: pl.BlockSpec((t, w), lambda b: (b, 0))
    return pl.pallas_call(
        functools.partial(_attn_sample_kernel, t=t, past=past),
        grid=(batch,),
        in_specs=[
            row_spec(FOX_WIDTH), row_spec(FOX_WIDTH), row_spec(FOX_WIDTH), row_spec(FOX_HEADS),
            pl.BlockSpec((1, FOX_HEADS, t), lambda b: (b, 0, 0)),
            pl.BlockSpec((1, past, FOX_WIDTH), lambda b: (b, 0, 0)),
            pl.BlockSpec((1, past, FOX_WIDTH), lambda b: (b, 0, 0)),
            pl.BlockSpec((1, FOX_HEADS, past), lambda b: (b, 0, 0)),
            _const_spec((past, past)),
        ],
        out_specs=row_spec(FOX_WIDTH),
        out_shape=jax.ShapeDtypeStruct((batch * t, FOX_WIDTH), BF16),
        compiler_params=pltpu.CompilerParams(
            dimension_semantics=("arbitrary",), vmem_limit_bytes=VMEM_LIMIT),
        name="attn_sample",
    )(q, kb, vb, logf, logf_t, cache_k, cache_v, cache_logf_t, upper)


def _merge_kernel(u_ref, hist_ref, attn_ref, sa_ref, sb_ref, x_ref,
                  cw_ref, cb_ref, cg_ref, cbeta_ref, wb_ref, bb_ref, wa_ref, wout_ref,
                  g1_ref, b1_ref, *rest, ts, tiles_per_seq, n_tiles, zero_first_hist, alpha,
                  aliased):
    if aliased:
        rest = rest[2:]
    mid_ref, mid3_ref, ue_scr, ph_scr, h_scr = rest
    i = pl.program_id(0)

    @pl.when(i >= n_tiles)
    def _():
        mid_ref[...] = jnp.zeros_like(mid_ref)
        mid3_ref[...] = jnp.zeros_like(mid3_ref)

    @pl.when(i < n_tiles)
    def _():
        _merge_tile(u_ref, hist_ref, attn_ref, sa_ref, sb_ref, x_ref, cw_ref, cb_ref, cg_ref,
                    cbeta_ref, wb_ref, bb_ref, wa_ref, wout_ref, g1_ref, b1_ref, mid_ref,
                    mid3_ref, ue_scr, ph_scr, h_scr, ts=ts, alpha=alpha,
                    zero_hist=(i % tiles_per_seq == 0) if zero_first_hist else None)


def _merge_tile(u_ref, hist_ref, attn_ref, sa_ref, sb_ref, x_ref, cw_ref, cb_ref, cg_ref,
                cbeta_ref, wb_ref, bb_ref, wa_ref, wout_ref, g1_ref, b1_ref, mid_ref, mid3_ref,
                ue_scr, ph_scr, h_scr, *, ts, alpha, zero_hist):
    hist = hist_ref[...]
    if zero_hist is not None:
        hist = jnp.where(zero_hist, 0.0, hist)
    ue_scr[0:HIST, :] = hist
    ue_scr[HIST:HIST + ts, :] = u_ref[...]
    lead = HIST - (CONV_WIDTH - 1)
    for r in range(SUBLANES):
        span = ts + (CONV_WIDTH - 1 - r) // SUBLANES * SUBLANES
        ph_scr[r, 0:span, :] = ue_scr[lead + r:lead + r + span, :]

    def conv_rows(rt, carry):
        base = pl.multiple_of(rt * CONV_SUB, CONV_SUB)
        acc = jnp.zeros((CONV_SUB, CONV_CH), F32)
        for j in range(CONV_WIDTH):
            r, a = j % SUBLANES, j // SUBLANES
            acc = acc + cw_ref[j:j + 1, :] * ph_scr[r, pl.ds(base + a * SUBLANES, CONV_SUB), :]
        h_scr[pl.ds(base, CONV_SUB), :] = acc
        return carry

    lax.fori_loop(0, ts // CONV_SUB, conv_rows, 0)
    h = _layer_norm(h_scr[...] + cb_ref[...], cg_ref[...], cbeta_ref[...])
    h = h * jax.nn.sigmoid(h)
    conv_out = _dot(h.astype(BF16), wb_ref[...]) + bb_ref[...]
    attn_out = _dot(attn_ref[...], wa_ref[...])
    m = sa_ref[...].astype(F32) * attn_out + sb_ref[...].astype(F32) * conv_out
    z = alpha * x_ref[...] + _dot(m.astype(BF16), wout_ref[...])
    mid = _layer_norm(z, g1_ref[...], b1_ref[...])
    mid_ref[...] = mid
    for j in range(ROW_TILES):
        mid3_ref[pl.ds(j, ts, stride=ROW_TILES), :] = mid[:, j * LANES:(j + 1) * LANES]


def _merge(u, hist, attn, sa, sb, x, conv_p, wa, wout, g1, b1, *, n_seq, seq, ts, hist_from_u,
           alpha, total_rows, row_offset, prev=None):
    nt = seq // ts
    n_tiles = n_seq * nt
    off = row_offset // ts
    grid_tiles = n_tiles if prev is not None else total_rows // ts
    src = lambda i: jnp.minimum(i, n_tiles - 1)
    row_spec = lambda w: pl.BlockSpec((ts, w), lambda i: (src(i), 0))
    if hist_from_u:
        per = ts // HIST
        hist_spec = pl.BlockSpec((HIST, CONV_CH), lambda i: (jnp.maximum(src(i) * per - 1, 0), 0))
    else:
        hist_spec = pl.BlockSpec((HIST, CONV_CH), lambda i: (src(i) // nt, 0))
    cw, cb, cg, cbeta, wb, bb = conv_p
    consts = [cw, cb, cg, cbeta, wb, bb, wa, wout, g1, b1]
    in_specs = [row_spec(CONV_CH), hist_spec, row_spec(FOX_WIDTH), row_spec(D_MODEL),
                row_spec(D_MODEL), row_spec(D_MODEL)] + [_const_spec(c.shape) for c in consts]
    args = [u, hist, attn, sa, sb, x] + consts
    aliases = {}
    if prev is not None:
        in_specs += [pl.BlockSpec(memory_space=pl.ANY)] * 2
        aliases = {len(args): 0, len(args) + 1: 1}
        args += list(prev)
    span = ts + (CONV_WIDTH - 1) // SUBLANES * SUBLANES
    return pl.pallas_call(
        functools.partial(_merge_kernel, ts=ts, tiles_per_seq=nt, n_tiles=n_tiles,
                          zero_first_hist=hist_from_u, alpha=alpha, aliased=prev is not None),
        grid=(grid_tiles,),
        in_specs=in_specs,
        out_specs=(pl.BlockSpec((ts, D_MODEL), lambda i: (off + i, 0)),
                   pl.BlockSpec((ts * ROW_TILES, LANES), lambda i: (off + i, 0))),
        out_shape=(jax.ShapeDtypeStruct((total_rows, D_MODEL), F32),
                   jax.ShapeDtypeStruct((total_rows * ROW_TILES, LANES), F32)),
        scratch_shapes=[pltpu.VMEM((HIST + ts, CONV_CH), F32),
                        pltpu.VMEM((SUBLANES, span, CONV_CH), F32),
                        pltpu.VMEM((ts, CONV_CH), F32)],
        input_output_aliases=aliases,
        compiler_params=pltpu.CompilerParams(
            dimension_semantics=("arbitrary",), vmem_limit_bytes=VMEM_LIMIT),
        name="merge",
    )(*args)


def _router_kernel(mid_ref, wr_hi_ref, wr_lo_ref, br_ref, before_ref,
                   eidx_ref, gate_ref, rank_ref, cnt_ref, carry_scr):
    tr = ROUTER_TILE
    i = pl.program_id(0)

    @pl.when(i == 0)
    def _():
        carry_scr[...] = jnp.zeros_like(carry_scr)

    x = mid_ref[...]
    x_hi = x.astype(BF16)
    x_lo = (x - x_hi.astype(F32)).astype(BF16)
    wr_hi = wr_hi_ref[...]
    logits = _dot_nt(wr_hi, x_hi) + _dot_nt(wr_hi, x_lo) + _dot_nt(wr_lo_ref[...], x_hi)
    scores = jax.nn.sigmoid(logits)
    sel = scores + br_ref[...]

    sel3 = sel.reshape(N_GROUPS, GROUP_SIZE, tr)
    in_group = lax.broadcasted_iota(jnp.int32, sel3.shape, 1)
    m1 = jnp.max(sel3, axis=1, keepdims=True)
    first = jnp.min(jnp.where(sel3 == m1, in_group, GROUP_SIZE), axis=1, keepdims=True)
    m2 = jnp.max(jnp.where(in_group == first, NEG_INF, sel3), axis=1, keepdims=True)
    gs = m1 + m2
    gi = lax.broadcasted_iota(jnp.int32, gs.shape, 0)
    beaten = jnp.zeros(gs.shape, F32)
    for g in range(N_GROUPS):
        other = gs[g:g + 1]
        wins = (other > gs) | ((other == gs) & (g < gi))
        beaten = beaten + jnp.where(wins, 1.0, 0.0)
    drop = jnp.where(beaten < TOPK_GROUPS, 0.0, NEG_INF)
    cur = (sel3 + drop).reshape(N_EXPERTS, tr)

    ei = lax.broadcasted_iota(jnp.int32, (N_EXPERTS, tr), 0)
    idxs, vals = [], []
    picked = jnp.zeros((N_EXPERTS, tr), F32)
    for _ in range(TOP_K):
        m = jnp.max(cur, axis=0, keepdims=True)
        idx = jnp.min(jnp.where(cur == m, ei, N_EXPERTS), axis=0, keepdims=True)
        hit = ei == idx
        vals.append(jnp.sum(jnp.where(hit, scores, 0.0), axis=0, keepdims=True))
        idxs.append(idx)
        picked = picked + jnp.where(hit, 1.0, 0.0)
        cur = jnp.where(hit, NEG_INF, cur)

    total = vals[0]
    for v in vals[1:]:
        total = total + v
    for k in range(TOP_K):
        gate_ref[k:k + 1, :] = vals[k] / total * ROUTED_SCALE
        eidx_ref[k:k + 1, :] = idxs[k]

    ahead = _dot(picked.astype(BF16), before_ref[...]) + carry_scr[:, 0:1]
    for k in range(TOP_K):
        rank = jnp.sum(jnp.where(ei == idxs[k], ahead, 0.0), axis=0, keepdims=True)
        rank_ref[k:k + 1, :] = rank.astype(jnp.int32)
    carry_scr[...] = carry_scr[...] + jnp.sum(picked, axis=1, keepdims=True)
    cnt_ref[...] = carry_scr[...]


def _router(mid, wr_hi, wr_lo, br, before):
    tr = ROUTER_TILE
    tokens = mid.shape[0]
    tok_spec = pl.BlockSpec((TOP_K, tr), lambda i: (0, i))
    return pl.pallas_call(
        _router_kernel,
        grid=(tokens // tr,),
        in_specs=[pl.BlockSpec((tr, D_MODEL), lambda i: (i, 0)),
                  _const_spec((N_EXPERTS, D_MODEL)), _const_spec((N_EXPERTS, D_MODEL)),
                  _const_spec((N_EXPERTS, 1)), _const_spec((tr, tr))],
        out_specs=(tok_spec, tok_spec, tok_spec,
                   pl.BlockSpec((N_EXPERTS, LANES), lambda i: (0, 0))),
        out_shape=(jax.ShapeDtypeStruct((TOP_K, tokens), jnp.int32),
                   jax.ShapeDtypeStruct((TOP_K, tokens), F32),
                   jax.ShapeDtypeStruct((TOP_K, tokens), jnp.int32),
                   jax.ShapeDtypeStruct((N_EXPERTS, LANES), F32)),
        scratch_shapes=[pltpu.VMEM((N_EXPERTS, LANES), F32)],
        compiler_params=pltpu.CompilerParams(
            dimension_semantics=("arbitrary",), vmem_limit_bytes=VMEM_LIMIT),
        name="router",
    )(mid, wr_hi, wr_lo, br, before)


def _dest_kernel(eidx_ref, rank_ref, starts_ref, dest_ref):
    tt = MOVE_TILE
    tokens = eidx_ref.shape[1]
    ei = lax.broadcasted_iota(jnp.int32, (N_EXPERTS, tokens), 0)
    starts = starts_ref[...]
    for k in range(TOP_K):
        hit = ei == eidx_ref[k:k + 1, :]
        start = jnp.sum(jnp.where(hit, starts, 0.0), axis=0, keepdims=True)
        dest = start.astype(jnp.int32) + rank_ref[k:k + 1, :]
        for c in range(tokens // tt):
            dest_ref[c, k:k + 1, :] = dest[:, c * tt:(c + 1) * tt]


def _dest(eidx, rank, starts):
    tt = MOVE_TILE
    tokens = eidx.shape[1]
    step = DEST_TILE
    tok_spec = pl.BlockSpec((TOP_K, step), lambda i: (0, i))
    return pl.pallas_call(
        _dest_kernel,
        grid=(tokens // step,),
        in_specs=[tok_spec, tok_spec, _const_spec((N_EXPERTS, 1))],
        out_specs=pl.BlockSpec((step // tt, TOP_K, tt), lambda i: (i, 0, 0)),
        out_shape=jax.ShapeDtypeStruct((tokens // tt, TOP_K, tt), jnp.int32),
        compiler_params=pltpu.CompilerParams(dimension_semantics=("arbitrary",)),
        name="dest",
    )(eidx, rank, starts)


def _row_slice(ref, row):
    return ref.at[pl.ds(pl.multiple_of(row * ROW_TILES, ROW_TILES), ROW_TILES), :]


def _dispatch_kernel(dest_ref, rows_ref, xs_ref, sem):
    tt = MOVE_TILE

    def issue(t, carry):
        src = _row_slice(rows_ref, t)
        for k in range(TOP_K):
            pltpu.make_async_copy(src, _row_slice(xs_ref, dest_ref[k * tt + t]), sem).start(
                priority=k % 2)
        return carry

    lax.fori_loop(0, tt, issue, 0)
    for _ in range(TOP_K):
        pltpu.make_async_copy(rows_ref, xs_ref.at[pl.ds(0, tt * ROW_TILES), :], sem).wait()


def _dispatch(dest_flat, rows3):
    tt = MOVE_TILE
    tokens = rows3.shape[0] // ROW_TILES
    n_rows = tokens * TOP_K
    return pl.pallas_call(
        _dispatch_kernel,
        grid=(tokens // tt,),
        in_specs=[pl.BlockSpec((tt * TOP_K,), lambda i: (i,), memory_space=pltpu.SMEM),
                  pl.BlockSpec((tt * ROW_TILES, LANES), lambda i: (i, 0))],
        out_specs=pl.BlockSpec(memory_space=pl.ANY),
        out_shape=jax.ShapeDtypeStruct((n_rows * ROW_TILES, LANES), F32),
        scratch_shapes=[pltpu.SemaphoreType.DMA(())],
        compiler_params=pltpu.CompilerParams(dimension_semantics=("arbitrary",)),
        name="dispatch",
    )(dest_flat, rows3)


def _expert_kernel(vblk_ref, vexp_ref, vlo_ref, vhi_ref, xs_ref, wg_ref, wu_ref, wd_ref,
                   ys_ref):
    bm = EXPERT_BLOCK
    v = pl.program_id(0)
    lo, hi = vlo_ref[v], vhi_ref[v]

    @pl.when(hi > lo)
    def _():
        x = jnp.concatenate(
            [xs_ref[pl.ds(j, bm, stride=ROW_TILES), :] for j in range(ROW_TILES)], axis=1)
        xb = x.astype(BF16)
        g = _dot(xb, wg_ref[0].astype(BF16))
        u = _dot(xb, wu_ref[0].astype(BF16))
        h = (g * jax.nn.sigmoid(g) * u).astype(BF16)
        y = _dot(h, wd_ref[0].astype(BF16))
        row = lax.broadcasted_iota(jnp.int32, (bm, LANES), 0)
        mine = (row >= lo) & (row < hi)
        whole = jnp.logical_and(lo == 0, hi == bm)
        first = jnp.logical_or(v == 0, vblk_ref[jnp.maximum(v - 1, 0)] != vblk_ref[v])
        first = jnp.logical_and(first, jnp.logical_not(whole))

        @pl.when(whole)
        def _():
            for j in range(ROW_TILES):
                ys_ref[pl.ds(j, bm, stride=ROW_TILES), :] = y[:, j * LANES:(j + 1) * LANES]

        @pl.when(first)
        def _():
            for j in range(ROW_TILES):
                ys_ref[pl.ds(j, bm, stride=ROW_TILES), :] = jnp.where(
                    mine, y[:, j * LANES:(j + 1) * LANES], 0.0)

        @pl.when(jnp.logical_not(jnp.logical_or(first, whole)))
        def _():
            for j in range(ROW_TILES):
                tile = pl.ds(j, bm, stride=ROW_TILES)
                ys_ref[tile, :] = jnp.where(mine, y[:, j * LANES:(j + 1) * LANES], ys_ref[tile, :])


def _experts(vblk, vexp, vlo, vhi, xs, wg, wu, wd):
    bm = EXPERT_BLOCK
    n_visits = vblk.shape[0]
    row_map = lambda v, vb, ve, lo, hi: (vb[v], 0)
    w_map = lambda v, vb, ve, lo, hi: (ve[v], 0, 0)
    grid_spec = pltpu.PrefetchScalarGridSpec(
        num_scalar_prefetch=4,
        grid=(n_visits,),
        in_specs=[
            pl.BlockSpec((bm * ROW_TILES, LANES), row_map),
            pl.BlockSpec((1, D_MODEL, D_EXPERT), w_map),
            pl.BlockSpec((1, D_MODEL, D_EXPERT), w_map),
            pl.BlockSpec((1, D_EXPERT, D_MODEL), w_map),
        ],
        out_specs=pl.BlockSpec((bm * ROW_TILES, LANES), row_map),
    )
    return pl.pallas_call(
        _expert_kernel,
        grid_spec=grid_spec,
        out_shape=jax.ShapeDtypeStruct(xs.shape, F32),
        compiler_params=pltpu.CompilerParams(
            dimension_semantics=("arbitrary",), vmem_limit_bytes=VMEM_LIMIT),
        name="experts",
    )(vblk, vexp, vlo, vhi, xs, wg, wu, wd)


def _combine_kernel(dest_ref, dest_next_ref, ys_ref, gate_ref, mid_ref, wsg_ref, wsu_ref,
                    wsd_ref, g2_ref, b2_ref, out_ref, buf, sem, *, alpha):
    tt = MOVE_TILE
    i = pl.program_id(0)
    n = pl.num_programs(0)
    cur = i % 2

    def start_rows(slots_ref, b):
        def issue(t, carry):
            for k in range(TOP_K):
                pltpu.make_async_copy(_row_slice(ys_ref, slots_ref[k * tt + t]),
                                      _row_slice(buf.at[b], k * tt + t),
                                      sem.at[b]).start(priority=k % 2)
            return carry
        lax.fori_loop(0, tt, issue, 0)

    @pl.when(i == 0)
    def _():
        start_rows(dest_ref, cur)

    @pl.when(i + 1 < n)
    def _():
        start_rows(dest_next_ref, 1 - cur)

    mid = mid_ref[...]
    xb = mid.astype(BF16)
    g = _dot(xb, wsg_ref[...])
    h = (g * jax.nn.sigmoid(g) * _dot(xb, wsu_ref[...])).astype(BF16)
    acc = alpha * mid + _dot(h, wsd_ref[...])

    pltpu.make_async_copy(ys_ref.at[pl.ds(0, TOP_K * tt * ROW_TILES), :], buf.at[cur],
                          sem.at[cur]).wait()
    gates = gate_ref[...]
    routed = jnp.zeros((tt, D_MODEL), F32)
    for k in range(TOP_K):
        yk = jnp.concatenate(
            [buf[cur, pl.ds(k * tt * ROW_TILES + j, tt, stride=ROW_TILES), :]
             for j in range(ROW_TILES)], axis=1)
        routed = routed + gates[:, k:k + 1] * yk
    out_ref[...] = _layer_norm(acc + routed, g2_ref[...], b2_ref[...])


def _combine(dest_flat, ys, gate, mid, wsg, wsu, wsd, g2, b2, *, alpha):
    tt = MOVE_TILE
    tokens = mid.shape[0]
    n = tokens // tt
    consts = [wsg, wsu, wsd, g2, b2]
    slots_spec = lambda f: pl.BlockSpec((tt * TOP_K,), f, memory_space=pltpu.SMEM)
    return pl.pallas_call(
        functools.partial(_combine_kernel, alpha=alpha),
        grid=(n,),
        in_specs=[slots_spec(lambda i: (i,)),
                  slots_spec(lambda i: (jnp.minimum(i + 1, n - 1),)),
                  pl.BlockSpec(memory_space=pl.ANY),
                  pl.BlockSpec((tt, TOP_K), lambda i: (i, 0)),
                  pl.BlockSpec((tt, D_MODEL), lambda i: (i, 0))]
                 + [_const_spec(c.shape) for c in consts],
        out_specs=pl.BlockSpec((tt, D_MODEL), lambda i: (i, 0)),
        out_shape=jax.ShapeDtypeStruct((tokens, D_MODEL), F32),
        scratch_shapes=[pltpu.VMEM((2, TOP_K * tt * ROW_TILES, LANES), F32),
                        pltpu.SemaphoreType.DMA((2,))],
        compiler_params=pltpu.CompilerParams(
            dimension_semantics=("arbitrary",), vmem_limit_bytes=VMEM_LIMIT),
        name="combine",
    )(dest_flat, dest_flat, ys, gate, mid, *consts)


def _tri(n, *, lower):
    r = lax.broadcasted_iota(jnp.int32, (n, n), 0)
    c = lax.broadcasted_iota(jnp.int32, (n, n), 1)
    return jnp.where((c <= r) if lower else (r <= c), 1.0, 0.0).astype(BF16)


def _moe(mid, rows3, w_router, b_router, w_e_gate, w_e_up, w_e_down, wsg, wsu, wsd, g2, b2, alpha):
    tokens = mid.shape[0]
    bm = EXPERT_BLOCK
    wr_t = w_router.T
    wr_hi = wr_t.astype(BF16)
    wr_lo = (wr_t - wr_hi.astype(F32)).astype(BF16)
    r = lax.broadcasted_iota(jnp.int32, (ROUTER_TILE, ROUTER_TILE), 0)
    c = lax.broadcasted_iota(jnp.int32, (ROUTER_TILE, ROUTER_TILE), 1)
    before = jnp.where(r < c, 1.0, 0.0).astype(BF16)
    eidx, gate, rank, cnt = _router(mid, wr_hi, wr_lo, b_router.reshape(N_EXPERTS, 1), before)

    i32 = lambda a: a.astype(jnp.int32)
    counts = i32(cnt[:, 0])
    ends = jnp.cumsum(counts)
    starts = ends - counts
    first_blk = starts // bm
    n_vis = jnp.where(counts > 0, (ends - 1) // bm - first_blk + 1, 0)
    vis_ends = jnp.cumsum(n_vis)
    vis_starts = vis_ends - n_vis
    max_visits = tokens * TOP_K // bm + N_EXPERTS - 1
    v = jnp.arange(max_visits, dtype=jnp.int32)
    vc = jnp.minimum(v, vis_ends[-1] - 1)
    vexp = jnp.minimum(jnp.sum(i32(vis_ends[None, :] <= vc[:, None]), axis=1), N_EXPERTS - 1)
    vblk = first_blk[vexp] + vc - vis_starts[vexp]
    vlo = jnp.maximum(starts[vexp], vblk * bm) - vblk * bm
    vhi = jnp.minimum(ends[vexp], (vblk + 1) * bm) - vblk * bm
    vhi = jnp.where(v < vis_ends[-1], vhi, vlo)
    dest = _dest(eidx, rank, starts.astype(F32).reshape(N_EXPERTS, 1)).reshape(tokens * TOP_K)

    xs = _dispatch(dest, rows3)
    ys = _experts(i32(vblk), vexp, i32(vlo), i32(vhi), xs, w_e_gate, w_e_up, w_e_down)
    return _combine(dest, ys, gate.T, mid, wsg, wsu, wsd, g2, b2, alpha=alpha)


def kernel(x_prompt, x_sample, cache_k, cache_v, cache_logf, state_conv, w_in, b_in, conv_w,
           conv_b, conv_ln_g, conv_ln_b, w_a, w_b, b_b, w_out, ln1_g, ln1_b, w_router, b_router,
           w_e_gate, w_e_up, w_e_down, w_s_gate, w_s_up, w_s_down, ln2_g, ln2_b):
    depth = w_in.shape[0]
    alpha = float((2 * depth) ** 0.25)
    batch, seq, _ = x_prompt.shape
    dbatch, dseq, _ = x_sample.shape
    past = cache_k.shape[2]
    rows_p, rows_s = batch * seq, dbatch * dseq
    total = rows_p + rows_s
    assert seq % IN_TILE == 0 and rows_s % IN_TILE == 0 and seq % MERGE_TILE == 0
    assert total % ROUTER_TILE == 0 and dseq == HIST and rows_p % dseq == 0
    assert total % DEST_TILE == 0 and (total * TOP_K) % EXPERT_BLOCK == 0

    hp = x_prompt.reshape(rows_p, D_MODEL)
    hs = x_sample.reshape(rows_s, D_MODEL)
    tri_in = _tri(IN_TILE, lower=True)
    upper_past = _tri(past, lower=False)
    row2 = lambda a: a.reshape(1, -1)
    outs = {n: [] for n in ("kp", "vp", "fp", "cp", "ks", "vs", "fs", "cs")}

    for l in range(depth):
        w = w_in[l]
        b = b_in[l]
        main_cols = lambda a: jnp.concatenate([a[..., :OFF_F], a[..., OFF_GLU:]], axis=-1)
        w_main = main_cols(w).astype(BF16)
        b_main = row2(main_cols(b))
        w_f = jnp.pad(w[:, OFF_F:OFF_GLU], ((0, 0), (0, LANES - FOX_HEADS))).astype(BF16)
        b_f = row2(jnp.pad(b[OFF_F:OFF_GLU], (0, LANES - FOX_HEADS)))
        cw = jnp.pad(conv_w[l], ((0, 1), (0, 0)))
        conv_p = (cw, row2(conv_b[l]), row2(conv_ln_g[l]), row2(conv_ln_b[l]),
                  w_b[l].astype(BF16), row2(b_b[l]))
        wa, wout = w_a[l].astype(BF16), w_out[l].astype(BF16)
        g1, b1 = row2(ln1_g[l]), row2(ln1_b[l])

        q, k, v, kb, vb, logf, c, u, sa, sb = _inproj(
            hp, w_main, b_main, w_f, b_f, tri_in, tiles_per_seq=seq // IN_TILE)
        c_row = c.reshape(batch, seq, FOX_HEADS).transpose(0, 2, 1)
        attn = _attn_prompt(q, kb, vb, c_row, batch=batch, seq=seq)
        mid, rows3 = _merge(u, u, attn, sa, sb, hp, conv_p, wa, wout, g1, b1,
                            n_seq=batch, seq=seq, ts=MERGE_TILE, hist_from_u=True, alpha=alpha,
                            total_rows=total, row_offset=0)
        outs["kp"].append(k.reshape(batch, seq, FOX_HEADS, HEAD_DIM))
        outs["vp"].append(v.reshape(batch, seq, FOX_HEADS, HEAD_DIM))
        outs["fp"].append(logf.reshape(batch, seq, FOX_HEADS))
        outs["cp"].append(u.reshape(batch, seq, CONV_CH)[:, seq - (CONV_WIDTH - 1):])

        q, k, v, kb, vb, logf, _, u, sa, sb = _inproj(
            hs, w_main, b_main, w_f, b_f, tri_in, tiles_per_seq=1)
        logf_t = logf.reshape(dbatch, dseq, FOX_HEADS).transpose(0, 2, 1)
        attn = _attn_sample(
            q, kb, vb, logf, logf_t, cache_k[l].reshape(dbatch, past, FOX_WIDTH),
            cache_v[l].reshape(dbatch, past, FOX_WIDTH), cache_logf[l].transpose(0, 2, 1),
            upper_past, batch=dbatch, t=dseq, past=past)
        hist = jnp.pad(state_conv[l], ((0, 0), (HIST - (CONV_WIDTH - 1), 0), (0, 0)))
        mid, rows3 = _merge(u, hist.reshape(dbatch * HIST, CONV_CH), attn, sa, sb, hs, conv_p,
                            wa, wout, g1, b1, n_seq=dbatch, seq=dseq, ts=dseq, hist_from_u=False,
                            alpha=alpha, total_rows=total, row_offset=rows_p, prev=(mid, rows3))
        outs["ks"].append(k.reshape(dbatch, dseq, FOX_HEADS, HEAD_DIM))
        outs["vs"].append(v.reshape(dbatch, dseq, FOX_HEADS, HEAD_DIM))
        outs["fs"].append(logf.reshape(dbatch, dseq, FOX_HEADS))
        u3 = u.reshape(dbatch, dseq, CONV_CH)
        u_ext = jnp.concatenate([state_conv[l], u3], axis=1)
        outs["cs"].append(u_ext[:, -(CONV_WIDTH - 1):])

        y = _moe(mid, rows3, w_router[l], b_router[l], w_e_gate[l], w_e_up[l], w_e_down[l],
                 w_s_gate[l].astype(BF16), w_s_up[l].astype(BF16), w_s_down[l].astype(BF16),
                 row2(ln2_g[l]), row2(ln2_b[l]), alpha)
        hp, hs = y[:rows_p], y[rows_p:]

    st = lambda n: jnp.stack(outs[n])
    return (hp.reshape(batch, seq, D_MODEL), hs.reshape(dbatch, dseq, D_MODEL),
            st("kp"), st("vp"), st("fp"), st("cp"), st("ks"), st("vs"), st("fs"), st("cs"))
```

```python
import functools

import jax
import jax.numpy as jnp
from jax import lax
from jax.experimental import pallas as pl
from jax.experimental.pallas import tpu as pltpu

D_MODEL = 1024
FOX_HEADS = 8
HEAD_DIM = 64
FOX_WIDTH = FOX_HEADS * HEAD_DIM
ATTN_SCALE = HEAD_DIM ** -0.5
LOG2E = 1.4426950408889634
CONV_CH = D_MODEL // 2
CONV_WIDTH = 31
N_EXPERTS = 256
TOP_K = 8
N_GROUPS = 8
GROUP_SIZE = N_EXPERTS // N_GROUPS
TOPK_GROUPS = 4
D_EXPERT = D_MODEL // 4
ROUTED_SCALE = 2.5
LN_EPS = 1e-5

OFF_K = FOX_WIDTH
OFF_V = 2 * FOX_WIDTH
OFF_F = 3 * FOX_WIDTH
OFF_GLU = OFF_F + FOX_HEADS
OFF_GA = OFF_GLU + 2 * CONV_CH
OFF_GB = OFF_GA + D_MODEL

LANES = 128
SUBLANES = 8
ROW_TILES = D_MODEL // LANES
VMEM_LIMIT = 56 * 1024 * 1024

IN_TILE = 512
ATTN_TILE = 256
MERGE_TILE = 256
CONV_SUB = 32
HIST = 32
ROUTER_TILE = 512
MOVE_TILE = 128
DEST_TILE = 1024
EXPERT_BLOCK = 256

F32 = jnp.float32
BF16 = jnp.bfloat16
NEG_INF = float("-inf")
NT_DIMS = (((1,), (1,)), ((), ()))


def _const_spec(shape):
    nd = len(shape)
    return pl.BlockSpec(shape, lambda *_: (0,) * nd, pipeline_mode=pl.Buffered(1))


def _split3(x):
    hi = x.astype(BF16)
    r1 = x - hi.astype(F32)
    mid = r1.astype(BF16)
    lo = (r1 - mid.astype(F32)).astype(BF16)
    return hi, mid, lo


def _dot(a, b):
    return jnp.dot(a, b, preferred_element_type=F32)


def _dot_nt(a, b):
    return lax.dot_general(a, b, NT_DIMS, preferred_element_type=F32)


def _exact_dot(ones_mat, x, *, ones_on_left):
    acc = None
    for part in _split3(x):
        term = _dot(ones_mat, part) if ones_on_left else _dot(part, ones_mat)
        acc = term if acc is None else acc + term
    return acc


def _layer_norm(x, g, b):
    mu = jnp.mean(x, axis=-1, keepdims=True)
    xc = x - mu
    var = jnp.mean(xc * xc, axis=-1, keepdims=True)
    return xc * lax.rsqrt(var + LN_EPS) * g + b


def _log_sigmoid(z):
    return jnp.minimum(z, 0.0) - jnp.log1p(jnp.exp(-jnp.abs(z)))


def _inproj_kernel(x_ref, w_ref, b_ref, wf_ref, bf_ref, tri_ref,
                   q_ref, k_ref, v_ref, kb_ref, vb_ref, logf_ref, c_ref, u_ref,
                   sa_ref, sb_ref, carry_ref, *, tiles_per_seq):
    i = pl.program_id(0)
    xb = x_ref[...].astype(BF16)

    def proj(c0, c1):
        return _dot(xb, w_ref[:, c0:c1]) + b_ref[:, c0:c1]

    q = proj(0, FOX_WIDTH)
    q_ref[...] = (q * (ATTN_SCALE * LOG2E)).astype(BF16)
    k = proj(FOX_WIDTH, 2 * FOX_WIDTH)
    kb_ref[...] = k.astype(BF16)
    v = proj(2 * FOX_WIDTH, 3 * FOX_WIDTH)
    vb_ref[...] = v.astype(BF16)
    for h in range(FOX_HEADS):
        head_rows = pl.ds(h, IN_TILE, stride=FOX_HEADS)
        k_ref[head_rows, :] = k[:, h * HEAD_DIM:(h + 1) * HEAD_DIM]
        v_ref[head_rows, :] = v[:, h * HEAD_DIM:(h + 1) * HEAD_DIM]

    logf = _log_sigmoid(_dot(xb, wf_ref[...]) + bf_ref[...])
    logf_ref[...] = logf[:, :FOX_HEADS]

    @pl.when(i % tiles_per_seq == 0)
    def _():
        carry_ref[...] = jnp.zeros_like(carry_ref)

    c = _exact_dot(tri_ref[...], logf, ones_on_left=True) + carry_ref[...]
    c_ref[...] = c[:, :FOX_HEADS]
    carry_ref[...] = c[IN_TILE - 1:IN_TILE, :]

    g0 = 3 * FOX_WIDTH
    glu_a = proj(g0, g0 + CONV_CH)
    glu_b = proj(g0 + CONV_CH, g0 + 2 * CONV_CH)
    u_ref[...] = glu_a * jax.nn.sigmoid(glu_b)
    g1 = g0 + 2 * CONV_CH
    sa_ref[...] = jax.nn.sigmoid(proj(g1, g1 + D_MODEL)).astype(BF16)
    sb_ref[...] = jax.nn.sigmoid(proj(g1 + D_MODEL, g1 + 2 * D_MODEL)).astype(BF16)


def _inproj(x, w_main, b_main, w_f, b_f, tri, *, tiles_per_seq):
    rows = x.shape[0]
    n_main = w_main.shape[1]
    row_spec = lambda w: pl.BlockSpec((IN_TILE, w), lambda i: (i, 0))
    out_shapes = (
        jax.ShapeDtypeStruct((rows, FOX_WIDTH), BF16),
        jax.ShapeDtypeStruct((rows * FOX_HEADS, HEAD_DIM), F32),
        jax.ShapeDtypeStruct((rows * FOX_HEADS, HEAD_DIM), F32),
        jax.ShapeDtypeStruct((rows, FOX_WIDTH), BF16),
        jax.ShapeDtypeStruct((rows, FOX_WIDTH), BF16),
        jax.ShapeDtypeStruct((rows, FOX_HEADS), F32),
        jax.ShapeDtypeStruct((rows, FOX_HEADS), F32),
        jax.ShapeDtypeStruct((rows, CONV_CH), F32),
        jax.ShapeDtypeStruct((rows, D_MODEL), BF16),
        jax.ShapeDtypeStruct((rows, D_MODEL), BF16),
    )
    head_spec = pl.BlockSpec((IN_TILE * FOX_HEADS, HEAD_DIM), lambda i: (i, 0))
    out_specs = (row_spec(FOX_WIDTH), head_spec, head_spec) + (row_spec(FOX_WIDTH),) * 2 + (
        row_spec(FOX_HEADS),) * 2 + (row_spec(CONV_CH), row_spec(D_MODEL), row_spec(D_MODEL))
    return pl.pallas_call(
        functools.partial(_inproj_kernel, tiles_per_seq=tiles_per_seq),
        grid=(rows // IN_TILE,),
        in_specs=[row_spec(D_MODEL), _const_spec((D_MODEL, n_main)), _const_spec((1, n_main)),
                  _const_spec((D_MODEL, LANES)), _const_spec((1, LANES)),
                  _const_spec((IN_TILE, IN_TILE))],
        out_specs=out_specs,
        out_shape=out_shapes,
        scratch_shapes=[pltpu.VMEM((1, LANES), F32)],
        compiler_params=pltpu.CompilerParams(
            dimension_semantics=("arbitrary",), vmem_limit_bytes=VMEM_LIMIT),
        name="inproj",
    )(x, w_main, b_main, w_f, b_f, tri)


def _attn_prompt_kernel(q_ref, k_ref, v_ref, crow_ref, o_ref, qm_scr, m_scr, acc_scr):
    t = ATTN_TILE
    i = pl.program_id(1)
    lane = lax.broadcasted_iota(jnp.int32, (t, LANES), 1)
    row = lax.broadcasted_iota(jnp.int32, (t, t), 0)
    col = lax.broadcasted_iota(jnp.int32, (t, t), 1)
    causal = col <= row
    wide = lambda x: jnp.concatenate([x] * (t // LANES), axis=1)
    pair_lanes = lambda pair: slice(pair * LANES, (pair + 1) * LANES)

    for h in range(FOX_HEADS):
        qp = q_ref[:, pair_lanes(h // 2)]
        in_head = (lane < HEAD_DIM) if h % 2 == 0 else (lane >= HEAD_DIM)
        qm_scr[h] = jnp.where(in_head, qp, jnp.zeros_like(qp))
        m_scr[h] = jnp.full((t, LANES), NEG_INF, F32)
        acc_scr[h] = jnp.zeros((t, 2 * LANES), F32)
    ones = jnp.ones((t, LANES), BF16)

    def step(j, masked):
        r0 = pl.multiple_of(j * t, t)
        ck = crow_ref[0, :, pl.ds(r0, t)] * LOG2E
        for pair in range(FOX_HEADS // 2):
            kj = k_ref[pl.ds(r0, t), pair_lanes(pair)]
            vj = jnp.concatenate([v_ref[pl.ds(r0, t), pair_lanes(pair)], ones], axis=1)
            for h in (2 * pair, 2 * pair + 1):
                s = _dot_nt(qm_scr[h], kj) - ck[h:h + 1, :]
                if masked:
                    s = jnp.where(causal, s, NEG_INF)
                m_prev = m_scr[h]
                m_new = jnp.maximum(m_prev, jnp.max(s, axis=1, keepdims=True))
                alpha = jnp.exp2(m_prev - m_new)
                p = jnp.exp2(s - wide(m_new))
                acc_scr[h] = wide(alpha) * acc_scr[h] + _dot(p.astype(BF16), vj)
                m_scr[h] = m_new

    def body(j, carry):
        step(j, False)
        return carry

    lax.fori_loop(0, i, body, 0)
    step(i, True)
    for pair in range(FOX_HEADS // 2):
        o0, o1 = (acc_scr[h, :, :LANES] / acc_scr[h, :, LANES:] for h in (2 * pair, 2 * pair + 1))
        o_ref[:, pair_lanes(pair)] = jnp.where(lane < HEAD_DIM, o0, o1).astype(BF16)


def _attn_prompt(q, kb, vb, c_row, *, batch, seq):
    t = ATTN_TILE
    nq = seq // t
    return pl.pallas_call(
        _attn_prompt_kernel,
        grid=(batch, nq),
        in_specs=[
            pl.BlockSpec((t, FOX_WIDTH), lambda b, i: (b * nq + i, 0)),
            pl.BlockSpec((seq, FOX_WIDTH), lambda b, i: (b, 0)),
            pl.BlockSpec((seq, FOX_WIDTH), lambda b, i: (b, 0)),
            pl.BlockSpec((1, FOX_HEADS, seq), lambda b, i: (b, 0, 0)),
        ],
        out_specs=pl.BlockSpec((t, FOX_WIDTH), lambda b, i: (b * nq + i, 0)),
        out_shape=jax.ShapeDtypeStruct((batch * seq, FOX_WIDTH), BF16),
        scratch_shapes=[pltpu.VMEM((FOX_HEADS, t, LANES), BF16),
                        pltpu.VMEM((FOX_HEADS, t, LANES), F32),
                        pltpu.VMEM((FOX_HEADS, t, 2 * LANES), F32)],
        compiler_params=pltpu.CompilerParams(
            dimension_semantics=("arbitrary", "arbitrary"), vmem_limit_bytes=VMEM_LIMIT),
        name="attn_prompt",
    )(q, kb, vb, c_row)


def _attn_sample_kernel(q_ref, kn_ref, vn_ref, lf_ref, lft_ref, ck_ref, cv_ref, clft_ref,
                        upper_ref, o_ref, *, t, past):
    rows = FOX_HEADS * t
    lane_head = lax.broadcasted_iota(jnp.int32, (t, FOX_WIDTH), 1) // HEAD_DIM
    q = q_ref[...]
    q_stack = jnp.concatenate(
        [jnp.where(lane_head == h, q, jnp.zeros_like(q)) for h in range(FOX_HEADS)], axis=0)

    clf = clft_ref[0]
    prefix = _exact_dot(upper_ref[...], clf, ones_on_left=False)
    to_end = prefix[:, past - 1:past] - prefix
    ri = lax.broadcasted_iota(jnp.int32, (t, t), 0)
    ci = lax.broadcasted_iota(jnp.int32, (t, t), 1)
    lower = jnp.where(ci <= ri, 1.0, 0.0).astype(BF16)
    upper = jnp.where(ri <= ci, 1.0, 0.0).astype(BF16)
    cn_col = _exact_dot(lower, lf_ref[...], ones_on_left=True)
    cn_row = _exact_dot(upper, lft_ref[0], ones_on_left=False)

    stack = lambda f: jnp.concatenate([f(h) for h in range(FOX_HEADS)], axis=0)
    cn_stack = stack(lambda h: cn_col[:, h:h + 1])
    bias_c = stack(lambda h: jnp.broadcast_to(to_end[h:h + 1, :], (t, past)))
    bias_n = stack(lambda h: jnp.broadcast_to(cn_row[h:h + 1, :], (t, t)))

    kc = ck_ref[0].astype(BF16)
    vc = cv_ref[0].astype(BF16)
    s_c = _dot_nt(q_stack, kc) + (bias_c + cn_stack) * LOG2E
    s_n = _dot_nt(q_stack, kn_ref[...]) + (cn_stack - bias_n) * LOG2E
    tq = lax.broadcasted_iota(jnp.int32, (rows, t), 0) % t
    tk = lax.broadcasted_iota(jnp.int32, (rows, t), 1)
    s_n = jnp.where(tk <= tq, s_n, NEG_INF)
    m = jnp.maximum(jnp.max(s_c, axis=1, keepdims=True), jnp.max(s_n, axis=1, keepdims=True))
    p_c = jnp.exp2(s_c - m)
    p_n = jnp.exp2(s_n - m)
    denom = jnp.sum(p_c, axis=1, keepdims=True) + jnp.sum(p_n, axis=1, keepdims=True)
    o_stack = (_dot(p_c.astype(BF16), vc) + _dot(p_n.astype(BF16), vn_ref[...])) / denom
    out = jnp.zeros((t, FOX_WIDTH), F32)
    for h in range(FOX_HEADS):
        out = out + jnp.where(lane_head == h, o_stack[h * t:(h + 1) * t, :], 0.0)
    o_ref[...] = out.astype(BF16)


def _attn_sample(q, kb, vb, logf, logf_t, cache_k, cache_v, cache_logf_t, upper, *, batch, t, past):
    row_spec = lambda w: pl.BlockSpec((t, w), lambda b: (b, 0))
    return pl.pallas_call(
        functools.partial(_attn_sample_kernel, t=t, past=past),
        grid=(batch,),
        in_specs=[
            row_spec(FOX_WIDTH), row_spec(FOX_WIDTH), row_spec(FOX_WIDTH), row_spec(FOX_HEADS),
            pl.BlockSpec((1, FOX_HEADS, t), lambda b: (b, 0, 0)),
            pl.BlockSpec((1, past, FOX_WIDTH), lambda b: (b, 0, 0)),
            pl.BlockSpec((1, past, FOX_WIDTH), lambda b: (b, 0, 0)),
            pl.BlockSpec((1, FOX_HEADS, past), lambda b: (b, 0, 0)),
            _const_spec((past, past)),
        ],
        out_specs=row_spec(FOX_WIDTH),
        out_shape=jax.ShapeDtypeStruct((batch * t, FOX_WIDTH), BF16),
        compiler_params=pltpu.CompilerParams(
            dimension_semantics=("arbitrary",), vmem_limit_bytes=VMEM_LIMIT),
        name="attn_sample",
    )(q, kb, vb, logf, logf_t, cache_k, cache_v, cache_logf_t, upper)


def _merge_kernel(u_ref, hist_ref, attn_ref, sa_ref, sb_ref, x_ref,
                  cw_ref, cb_ref, cg_ref, cbeta_ref, wb_ref, bb_ref, wa_ref, wout_ref,
                  g1_ref, b1_ref, *rest, ts, tiles_per_seq, n_tiles, zero_first_hist, alpha,
                  aliased):
    if aliased:
        rest = rest[2:]
    mid_ref, mid3_ref, ue_scr, ph_scr, h_scr = rest
    i = pl.program_id(0)

    @pl.when(i >= n_tiles)
    def _():
        mid_ref[...] = jnp.zeros_like(mid_ref)
        mid3_ref[...] = jnp.zeros_like(mid3_ref)

    @pl.when(i < n_tiles)
    def _():
        _merge_tile(u_ref, hist_ref, attn_ref, sa_ref, sb_ref, x_ref, cw_ref, cb_ref, cg_ref,
                    cbeta_ref, wb_ref, bb_ref, wa_ref, wout_ref, g1_ref, b1_ref, mid_ref,
                    mid3_ref, ue_scr, ph_scr, h_scr, ts=ts, alpha=alpha,
                    zero_hist=(i % tiles_per_seq == 0) if zero_first_hist else None)


def _merge_tile(u_ref, hist_ref, attn_ref, sa_ref, sb_ref, x_ref, cw_ref, cb_ref, cg_ref,
                cbeta_ref, wb_ref, bb_ref, wa_ref, wout_ref, g1_ref, b1_ref, mid_ref, mid3_ref,
                ue_scr, ph_scr, h_scr, *, ts, alpha, zero_hist):
    hist = hist_ref[...]
    if zero_hist is not None:
        hist = jnp.where(zero_hist, 0.0, hist)
    ue_scr[0:HIST, :] = hist
    ue_scr[HIST:HIST + ts, :] = u_ref[...]
    lead = HIST - (CONV_WIDTH - 1)
    for r in range(SUBLANES):
        span = ts + (CONV_WIDTH - 1 - r) // SUBLANES * SUBLANES
        ph_scr[r, 0:span, :] = ue_scr[lead + r:lead + r + span, :]

    def conv_rows(rt, carry):
        base = pl.multiple_of(rt * CONV_SUB, CONV_SUB)
        acc = jnp.zeros((CONV_SUB, CONV_CH), F32)
        for j in range(CONV_WIDTH):
            r, a = j % SUBLANES, j // SUBLANES
            acc = acc + cw_ref[j:j + 1, :] * ph_scr[r, pl.ds(base + a * SUBLANES, CONV_SUB), :]
        h_scr[pl.ds(base, CONV_SUB), :] = acc
        return carry

    lax.fori_loop(0, ts // CONV_SUB, conv_rows, 0)
    h = _layer_norm(h_scr[...] + cb_ref[...], cg_ref[...], cbeta_ref[...])
    h = h * jax.nn.sigmoid(h)
    conv_out = _dot(h.astype(BF16), wb_ref[...]) + bb_ref[...]
    attn_out = _dot(attn_ref[...], wa_ref[...])
    m = sa_ref[...].astype(F32) * attn_out + sb_ref[...].astype(F32) * conv_out
    z = alpha * x_ref[...] + _dot(m.astype(BF16), wout_ref[...])
    mid = _layer_norm(z, g1_ref[...], b1_ref[...])
    mid_ref[...] = mid
    for j in range(ROW_TILES):
        mid3_ref[pl.ds(j, ts, stride=ROW_TILES), :] = mid[:, j * LANES:(j + 1) * LANES]


def _merge(u, hist, attn, sa, sb, x, conv_p, wa, wout, g1, b1, *, n_seq, seq, ts, hist_from_u,
           alpha, total_rows, row_offset, prev=None):
    nt = seq // ts
    n_tiles = n_seq * nt
    off = row_offset // ts
    grid_tiles = n_tiles if prev is not None else total_rows // ts
    src = lambda i: jnp.minimum(i, n_tiles - 1)
    row_spec = lambda w: pl.BlockSpec((ts, w), lambda i: (src(i), 0))
    if hist_from_u:
        per = ts // HIST
        hist_spec = pl.BlockSpec((HIST, CONV_CH), lambda i: (jnp.maximum(src(i) * per - 1, 0), 0))
    else:
        hist_spec = pl.BlockSpec((HIST, CONV_CH), lambda i: (src(i) // nt, 0))
    cw, cb, cg, cbeta, wb, bb = conv_p
    consts = [cw, cb, cg, cbeta, wb, bb, wa, wout, g1, b1]
    in_specs = [row_spec(CONV_CH), hist_spec, row_spec(FOX_WIDTH), row_spec(D_MODEL),
                row_spec(D_MODEL), row_spec(D_MODEL)] + [_const_spec(c.shape) for c in consts]
    args = [u, hist, attn, sa, sb, x] + consts
    aliases = {}
    if prev is not None:
        in_specs += [pl.BlockSpec(memory_space=pl.ANY)] * 2
        aliases = {len(args): 0, len(args) + 1: 1}
        args += list(prev)
    span = ts + (CONV_WIDTH - 1) // SUBLANES * SUBLANES
    return pl.pallas_call(
        functools.partial(_merge_kernel, ts=ts, tiles_per_seq=nt, n_tiles=n_tiles,
                          zero_first_hist=hist_from_u, alpha=alpha, aliased=prev is not None),
        grid=(grid_tiles,),
        in_specs=in_specs,
        out_specs=(pl.BlockSpec((ts, D_MODEL), lambda i: (off + i, 0)),
                   pl.BlockSpec((ts * ROW_TILES, LANES), lambda i: (off + i, 0))),
        out_shape=(jax.ShapeDtypeStruct((total_rows, D_MODEL), F32),
                   jax.ShapeDtypeStruct((total_rows * ROW_TILES, LANES), F32)),
        scratch_shapes=[pltpu.VMEM((HIST + ts, CONV_CH), F32),
                        pltpu.VMEM((SUBLANES, span, CONV_CH), F32),
                        pltpu.VMEM((ts, CONV_CH), F32)],
        input_output_aliases=aliases,
        compiler_params=pltpu.CompilerParams(
            dimension_semantics=("arbitrary",), vmem_limit_bytes=VMEM_LIMIT),
        name="merge",
    )(*args)


def _router_kernel(mid_ref, wr_hi_ref, wr_lo_ref, br_ref, before_ref,
                   eidx_ref, gate_ref, rank_ref, cnt_ref, carry_scr):
    tr = ROUTER_TILE
    i = pl.program_id(0)

    @pl.when(i == 0)
    def _():
        carry_scr[...] = jnp.zeros_like(carry_scr)

    x = mid_ref[...]
    x_hi = x.astype(BF16)
    x_lo = (x - x_hi.astype(F32)).astype(BF16)
    wr_hi = wr_hi_ref[...]
    logits = _dot_nt(wr_hi, x_hi) + _dot_nt(wr_hi, x_lo) + _dot_nt(wr_lo_ref[...], x_hi)
    scores = jax.nn.sigmoid(logits)
    sel = scores + br_ref[...]

    sel3 = sel.reshape(N_GROUPS, GROUP_SIZE, tr)
    in_group = lax.broadcasted_iota(jnp.int32, sel3.shape, 1)
    m1 = jnp.max(sel3, axis=1, keepdims=True)
    first = jnp.min(jnp.where(sel3 == m1, in_group, GROUP_SIZE), axis=1, keepdims=True)
    m2 = jnp.max(jnp.where(in_group == first, NEG_INF, sel3), axis=1, keepdims=True)
    gs = m1 + m2
    gi = lax.broadcasted_iota(jnp.int32, gs.shape, 0)
    beaten = jnp.zeros(gs.shape, F32)
    for g in range(N_GROUPS):
        other = gs[g:g + 1]
        wins = (other > gs) | ((other == gs) & (g < gi))
        beaten = beaten + jnp.where(wins, 1.0, 0.0)
    drop = jnp.where(beaten < TOPK_GROUPS, 0.0, NEG_INF)
    cur = (sel3 + drop).reshape(N_EXPERTS, tr)

    ei = lax.broadcasted_iota(jnp.int32, (N_EXPERTS, tr), 0)
    idxs, vals = [], []
    picked = jnp.zeros((N_EXPERTS, tr), F32)
    for _ in range(TOP_K):
        m = jnp.max(cur, axis=0, keepdims=True)
        idx = jnp.min(jnp.where(cur == m, ei, N_EXPERTS), axis=0, keepdims=True)
        hit = ei == idx
        vals.append(jnp.sum(jnp.where(hit, scores, 0.0), axis=0, keepdims=True))
        idxs.append(idx)
        picked = picked + jnp.where(hit, 1.0, 0.0)
        cur = jnp.where(hit, NEG_INF, cur)

    total = vals[0]
    for v in vals[1:]:
        total = total + v
    for k in range(TOP_K):
        gate_ref[k:k + 1, :] = vals[k] / total * ROUTED_SCALE
        eidx_ref[k:k + 1, :] = idxs[k]

    ahead = _dot(picked.astype(BF16), before_ref[...]) + carry_scr[:, 0:1]
    for k in range(TOP_K):
        rank = jnp.sum(jnp.where(ei == idxs[k], ahead, 0.0), axis=0, keepdims=True)
        rank_ref[k:k + 1, :] = rank.astype(jnp.int32)
    carry_scr[...] = carry_scr[...] + jnp.sum(picked, axis=1, keepdims=True)
    cnt_ref[...] = carry_scr[...]


def _router(mid, wr_hi, wr_lo, br, before):
    tr = ROUTER_TILE
    tokens = mid.shape[0]
    tok_spec = pl.BlockSpec((TOP_K, tr), lambda i: (0, i))
    return pl.pallas_call(
        _router_kernel,
        grid=(tokens // tr,),
        in_specs=[pl.BlockSpec((tr, D_MODEL), lambda i: (i, 0)),
                  _const_spec((N_EXPERTS, D_MODEL)), _const_spec((N_EXPERTS, D_MODEL)),
                  _const_spec((N_EXPERTS, 1)), _const_spec((tr, tr))],
        out_specs=(tok_spec, tok_spec, tok_spec,
                   pl.BlockSpec((N_EXPERTS, LANES), lambda i: (0, 0))),
        out_shape=(jax.ShapeDtypeStruct((TOP_K, tokens), jnp.int32),
                   jax.ShapeDtypeStruct((TOP_K, tokens), F32),
                   jax.ShapeDtypeStruct((TOP_K, tokens), jnp.int32),
                   jax.ShapeDtypeStruct((N_EXPERTS, LANES), F32)),
        scratch_shapes=[pltpu.VMEM((N_EXPERTS, LANES), F32)],
        compiler_params=pltpu.CompilerParams(
            dimension_semantics=("arbitrary",), vmem_limit_bytes=VMEM_LIMIT),
        name="router",
    )(mid, wr_hi, wr_lo, br, before)


def _dest_kernel(eidx_ref, rank_ref, starts_ref, dest_ref):
    tt = MOVE_TILE
    tokens = eidx_ref.shape[1]
    ei = lax.broadcasted_iota(jnp.int32, (N_EXPERTS, tokens), 0)
    starts = starts_ref[...]
    for k in range(TOP_K):
        hit = ei == eidx_ref[k:k + 1, :]
        start = jnp.sum(jnp.where(hit, starts, 0.0), axis=0, keepdims=True)
        dest = start.astype(jnp.int32) + rank_ref[k:k + 1, :]
        for c in range(tokens // tt):
            dest_ref[c, k:k + 1, :] = dest[:, c * tt:(c + 1) * tt]


def _dest(eidx, rank, starts):
    tt = MOVE_TILE
    tokens = eidx.shape[1]
    step = DEST_TILE
    tok_spec = pl.BlockSpec((TOP_K, step), lambda i: (0, i))
    return pl.pallas_call(
        _dest_kernel,
        grid=(tokens // step,),
        in_specs=[tok_spec, tok_spec, _const_spec((N_EXPERTS, 1))],
        out_specs=pl.BlockSpec((step // tt, TOP_K, tt), lambda i: (i, 0, 0)),
        out_shape=jax.ShapeDtypeStruct((tokens // tt, TOP_K, tt), jnp.int32),
        compiler_params=pltpu.CompilerParams(dimension_semantics=("arbitrary",)),
        name="dest",
    )(eidx, rank, starts)


def _row_slice(ref, row):
    return ref.at[pl.ds(pl.multiple_of(row * ROW_TILES, ROW_TILES), ROW_TILES), :]


def _dispatch_kernel(dest_ref, rows_ref, xs_ref, sem):
    tt = MOVE_TILE

    def issue(t, carry):
        src = _row_slice(rows_ref, t)
        for k in range(TOP_K):
            pltpu.make_async_copy(src, _row_slice(xs_ref, dest_ref[k * tt + t]), sem).start(
                priority=k % 2)
        return carry

    lax.fori_loop(0, tt, issue, 0)
    for _ in range(TOP_K):
        pltpu.make_async_copy(rows_ref, xs_ref.at[pl.ds(0, tt * ROW_TILES), :], sem).wait()


def _dispatch(dest_flat, rows3):
    tt = MOVE_TILE
    tokens = rows3.shape[0] // ROW_TILES
    n_rows = tokens * TOP_K
    return pl.pallas_call(
        _dispatch_kernel,
        grid=(tokens // tt,),
        in_specs=[pl.BlockSpec((tt * TOP_K,), lambda i: (i,), memory_space=pltpu.SMEM),
                  pl.BlockSpec((tt * ROW_TILES, LANES), lambda i: (i, 0))],
        out_specs=pl.BlockSpec(memory_space=pl.ANY),
        out_shape=jax.ShapeDtypeStruct((n_rows * ROW_TILES, LANES), F32),
        scratch_shapes=[pltpu.SemaphoreType.DMA(())],
        compiler_params=pltpu.CompilerParams(dimension_semantics=("arbitrary",)),
        name="dispatch",
    )(dest_flat, rows3)


def _expert_kernel(vblk_ref, vexp_ref, vlo_ref, vhi_ref, vnew_ref, vnext_ref, xs_ref,
                   wg_hbm, wu_hbm, wd_hbm, ys_ref, wg_buf, wu_buf, wd_buf, wgb, wub, wdb, sem):
    bm = EXPERT_BLOCK
    v = pl.program_id(0)
    lo, hi = vlo_ref[v], vhi_ref[v]
    slot = vnew_ref[v]

    def weight_copies(e, s):
        return [pltpu.make_async_copy(hbm.at[e], buf.at[s], sem.at[s])
                for hbm, buf in ((wg_hbm, wg_buf), (wu_hbm, wu_buf), (wd_hbm, wd_buf))]

    @pl.when(v == 0)
    def _():
        for c in weight_copies(vexp_ref[0], 0):
            c.start()

    @pl.when(slot >= 0)
    def _():
        for c in weight_copies(vexp_ref[v], slot):
            c.wait()
        nxt = vnext_ref[v]

        @pl.when(nxt >= 0)
        def _():
            for c in weight_copies(nxt, 1 - slot):
                c.start()

        wgb[...] = wg_buf[slot].astype(BF16)
        wub[...] = wu_buf[slot].astype(BF16)
        wdb[...] = wd_buf[slot].astype(BF16)

    @pl.when(hi > lo)
    def _():
        x = jnp.concatenate(
            [xs_ref[pl.ds(j, bm, stride=ROW_TILES), :] for j in range(ROW_TILES)], axis=1)
        xb = x.astype(BF16)
        g = _dot(xb, wgb[...])
        u = _dot(xb, wub[...])
        h = (g * jax.nn.sigmoid(g) * u).astype(BF16)
        y = _dot(h, wdb[...])
        row = lax.broadcasted_iota(jnp.int32, (bm, LANES), 0)
        mine = (row >= lo) & (row < hi)
        whole = jnp.logical_and(lo == 0, hi == bm)
        first = jnp.logical_or(v == 0, vblk_ref[jnp.maximum(v - 1, 0)] != vblk_ref[v])
        first = jnp.logical_and(first, jnp.logical_not(whole))

        @pl.when(whole)
        def _():
            for j in range(ROW_TILES):
                ys_ref[pl.ds(j, bm, stride=ROW_TILES), :] = y[:, j * LANES:(j + 1) * LANES]

        @pl.when(first)
        def _():
            for j in range(ROW_TILES):
                ys_ref[pl.ds(j, bm, stride=ROW_TILES), :] = jnp.where(
                    mine, y[:, j * LANES:(j + 1) * LANES], 0.0)

        @pl.when(jnp.logical_not(jnp.logical_or(first, whole)))
        def _():
            for j in range(ROW_TILES):
                tile = pl.ds(j, bm, stride=ROW_TILES)
                ys_ref[tile, :] = jnp.where(mine, y[:, j * LANES:(j + 1) * LANES], ys_ref[tile, :])


def _experts(vblk, vexp, vlo, vhi, vnew, vnext, xs, wg, wu, wd):
    bm = EXPERT_BLOCK
    n_visits = vblk.shape[0]
    row_map = lambda v, vb, *_: (vb[v], 0)
    grid_spec = pltpu.PrefetchScalarGridSpec(
        num_scalar_prefetch=6,
        grid=(n_visits,),
        in_specs=[pl.BlockSpec((bm * ROW_TILES, LANES), row_map)]
                 + [pl.BlockSpec(memory_space=pl.ANY)] * 3,
        out_specs=pl.BlockSpec((bm * ROW_TILES, LANES), row_map),
        scratch_shapes=[pltpu.VMEM((2, D_MODEL, D_EXPERT), F32),
                        pltpu.VMEM((2, D_MODEL, D_EXPERT), F32),
                        pltpu.VMEM((2, D_EXPERT, D_MODEL), F32),
                        pltpu.VMEM((D_MODEL, D_EXPERT), BF16),
                        pltpu.VMEM((D_MODEL, D_EXPERT), BF16),
                        pltpu.VMEM((D_EXPERT, D_MODEL), BF16),
                        pltpu.SemaphoreType.DMA((2,))],
    )
    return pl.pallas_call(
        _expert_kernel,
        grid_spec=grid_spec,
        out_shape=jax.ShapeDtypeStruct(xs.shape, F32),
        compiler_params=pltpu.CompilerParams(
            dimension_semantics=("arbitrary",), vmem_limit_bytes=VMEM_LIMIT),
        name="experts",
    )(vblk, vexp, vlo, vhi, vnew, vnext, xs, wg, wu, wd)


def _combine_kernel(dest_ref, dest_next_ref, ys_ref, gate_ref, mid_ref, wsg_ref, wsu_ref,
                    wsd_ref, g2_ref, b2_ref, out_a_ref, out_b_ref, buf, sem, *, alpha, steps_a):
    tt = MOVE_TILE
    i = pl.program_id(0)
    n = pl.num_programs(0)
    cur = i % 2

    def start_rows(slots_ref, b):
        def issue(t, carry):
            for k in range(TOP_K):
                pltpu.make_async_copy(_row_slice(ys_ref, slots_ref[k * tt + t]),
                                      _row_slice(buf.at[b], k * tt + t),
                                      sem.at[b]).start(priority=k % 2)
            return carry
        lax.fori_loop(0, tt, issue, 0)

    @pl.when(i == 0)
    def _():
        start_rows(dest_ref, cur)

    @pl.when(i + 1 < n)
    def _():
        start_rows(dest_next_ref, 1 - cur)

    mid = mid_ref[...]
    xb = mid.astype(BF16)
    g = _dot(xb, wsg_ref[...])
    h = (g * jax.nn.sigmoid(g) * _dot(xb, wsu_ref[...])).astype(BF16)
    acc = alpha * mid + _dot(h, wsd_ref[...])

    pltpu.make_async_copy(ys_ref.at[pl.ds(0, TOP_K * tt * ROW_TILES), :], buf.at[cur],
                          sem.at[cur]).wait()
    gates = gate_ref[...]
    routed = jnp.zeros((tt, D_MODEL), F32)
    for k in range(TOP_K):
        yk = jnp.concatenate(
            [buf[cur, pl.ds(k * tt * ROW_TILES + j, tt, stride=ROW_TILES), :]
             for j in range(ROW_TILES)], axis=1)
        routed = routed + gates[:, k:k + 1] * yk
    out = _layer_norm(acc + routed, g2_ref[...], b2_ref[...])

    @pl.when(i < steps_a)
    def _():
        out_a_ref[...] = out

    @pl.when(i >= steps_a)
    def _():
        out_b_ref[...] = out


def _combine(dest_flat, ys, gate, mid, wsg, wsu, wsd, g2, b2, *, alpha, rows_a):
    tt = MOVE_TILE
    tokens = mid.shape[0]
    n = tokens // tt
    steps_a = rows_a // tt
    consts = [wsg, wsu, wsd, g2, b2]
    slots_spec = lambda f: pl.BlockSpec((tt * TOP_K,), f, memory_space=pltpu.SMEM)
    return pl.pallas_call(
        functools.partial(_combine_kernel, alpha=alpha, steps_a=steps_a),
        grid=(n,),
        in_specs=[slots_spec(lambda i: (i,)),
                  slots_spec(lambda i: (jnp.minimum(i + 1, n - 1),)),
                  pl.BlockSpec(memory_space=pl.ANY),
                  pl.BlockSpec((tt, TOP_K), lambda i: (i, 0)),
                  pl.BlockSpec((tt, D_MODEL), lambda i: (i, 0))]
                 + [_const_spec(c.shape) for c in consts],
        out_specs=(pl.BlockSpec((tt, D_MODEL), lambda i: (jnp.minimum(i, steps_a - 1), 0)),
                   pl.BlockSpec((tt, D_MODEL), lambda i: (jnp.maximum(i - steps_a, 0), 0))),
        out_shape=(jax.ShapeDtypeStruct((rows_a, D_MODEL), F32),
                   jax.ShapeDtypeStruct((tokens - rows_a, D_MODEL), F32)),
        scratch_shapes=[pltpu.VMEM((2, TOP_K * tt * ROW_TILES, LANES), F32),
                        pltpu.SemaphoreType.DMA((2,))],
        compiler_params=pltpu.CompilerParams(
            dimension_semantics=("arbitrary",), vmem_limit_bytes=VMEM_LIMIT),
        name="combine",
    )(dest_flat, dest_flat, ys, gate, mid, *consts)


def _tri(n, *, lower):
    r = lax.broadcasted_iota(jnp.int32, (n, n), 0)
    c = lax.broadcasted_iota(jnp.int32, (n, n), 1)
    return jnp.where((c <= r) if lower else (r <= c), 1.0, 0.0).astype(BF16)


def _moe(mid, rows3, w_router, b_router, w_e_gate, w_e_up, w_e_down, wsg, wsu, wsd, g2, b2, alpha,
         rows_a):
    tokens = mid.shape[0]
    bm = EXPERT_BLOCK
    wr_t = w_router.T
    wr_hi = wr_t.astype(BF16)
    wr_lo = (wr_t - wr_hi.astype(F32)).astype(BF16)
    r = lax.broadcasted_iota(jnp.int32, (ROUTER_TILE, ROUTER_TILE), 0)
    c = lax.broadcasted_iota(jnp.int32, (ROUTER_TILE, ROUTER_TILE), 1)
    before = jnp.where(r < c, 1.0, 0.0).astype(BF16)
    eidx, gate, rank, cnt = _router(mid, wr_hi, wr_lo, b_router.reshape(N_EXPERTS, 1), before)

    i32 = lambda a: a.astype(jnp.int32)
    counts = i32(cnt[:, 0])
    ends = jnp.cumsum(counts)
    starts = ends - counts
    first_blk = starts // bm
    n_vis = jnp.where(counts > 0, (ends - 1) // bm - first_blk + 1, 0)
    vis_ends = jnp.cumsum(n_vis)
    vis_starts = vis_ends - n_vis
    max_visits = tokens * TOP_K // bm + N_EXPERTS - 1
    v = jnp.arange(max_visits, dtype=jnp.int32)
    vc = jnp.minimum(v, vis_ends[-1] - 1)
    vexp = jnp.minimum(jnp.sum(i32(vis_ends[None, :] <= vc[:, None]), axis=1), N_EXPERTS - 1)
    vblk = first_blk[vexp] + vc - vis_starts[vexp]
    vlo = jnp.maximum(starts[vexp], vblk * bm) - vblk * bm
    vhi = jnp.minimum(ends[vexp], (vblk + 1) * bm) - vblk * bm
    used = v < vis_ends[-1]
    vhi = jnp.where(used, vhi, vlo)
    is_first = used & (vexp != jnp.concatenate([jnp.full((1,), -1, jnp.int32), vexp[:-1]]))
    vnew = jnp.where(is_first, (jnp.cumsum(i32(is_first)) - 1) % 2, -1)
    nxt = vis_ends[vexp]
    vnext = jnp.where(is_first & (nxt < vis_ends[-1]), vexp[jnp.minimum(nxt, max_visits - 1)], -1)
    dest = _dest(eidx, rank, starts.astype(F32).reshape(N_EXPERTS, 1)).reshape(tokens * TOP_K)

    xs = _dispatch(dest, rows3)
    ys = _experts(i32(vblk), vexp, i32(vlo), i32(vhi), i32(vnew), i32(vnext), xs,
                  w_e_gate, w_e_up, w_e_down)
    return _combine(dest, ys, gate.T, mid, wsg, wsu, wsd, g2, b2, alpha=alpha, rows_a=rows_a)


def kernel(x_prompt, x_sample, cache_k, cache_v, cache_logf, state_conv, w_in, b_in, conv_w,
           conv_b, conv_ln_g, conv_ln_b, w_a, w_b, b_b, w_out, ln1_g, ln1_b, w_router, b_router,
           w_e_gate, w_e_up, w_e_down, w_s_gate, w_s_up, w_s_down, ln2_g, ln2_b):
    depth = w_in.shape[0]
    alpha = float((2 * depth) ** 0.25)
    batch, seq, _ = x_prompt.shape
    dbatch, dseq, _ = x_sample.shape
    past = cache_k.shape[2]
    rows_p, rows_s = batch * seq, dbatch * dseq
    total = rows_p + rows_s
    assert seq % IN_TILE == 0 and rows_s % IN_TILE == 0 and seq % MERGE_TILE == 0
    assert total % ROUTER_TILE == 0 and dseq == HIST and rows_p % dseq == 0
    assert total % DEST_TILE == 0 and (total * TOP_K) % EXPERT_BLOCK == 0

    hp = x_prompt.reshape(rows_p, D_MODEL)
    hs = x_sample.reshape(rows_s, D_MODEL)
    tri_in = _tri(IN_TILE, lower=True)
    upper_past = _tri(past, lower=False)
    row2 = lambda a: a.reshape(1, -1)
    outs = {n: [] for n in ("kp", "vp", "fp", "cp", "ks", "vs", "fs", "cs")}

    for l in range(depth):
        w = w_in[l]
        b = b_in[l]
        main_cols = lambda a: jnp.concatenate([a[..., :OFF_F], a[..., OFF_GLU:]], axis=-1)
        w_main = main_cols(w).astype(BF16)
        b_main = row2(main_cols(b))
        w_f = jnp.pad(w[:, OFF_F:OFF_GLU], ((0, 0), (0, LANES - FOX_HEADS))).astype(BF16)
        b_f = row2(jnp.pad(b[OFF_F:OFF_GLU], (0, LANES - FOX_HEADS)))
        cw = jnp.pad(conv_w[l], ((0, 1), (0, 0)))
        conv_p = (cw, row2(conv_b[l]), row2(conv_ln_g[l]), row2(conv_ln_b[l]),
                  w_b[l].astype(BF16), row2(b_b[l]))
        wa, wout = w_a[l].astype(BF16), w_out[l].astype(BF16)
        g1, b1 = row2(ln1_g[l]), row2(ln1_b[l])

        q, k, v, kb, vb, logf, c, u, sa, sb = _inproj(
            hp, w_main, b_main, w_f, b_f, tri_in, tiles_per_seq=seq // IN_TILE)
        c_row = c.reshape(batch, seq, FOX_HEADS).transpose(0, 2, 1)
        attn = _attn_prompt(q, kb, vb, c_row, batch=batch, seq=seq)
        mid, rows3 = _merge(u, u, attn, sa, sb, hp, conv_p, wa, wout, g1, b1,
                            n_seq=batch, seq=seq, ts=MERGE_TILE, hist_from_u=True, alpha=alpha,
                            total_rows=total, row_offset=0)
        outs["kp"].append(k.reshape(batch, seq, FOX_HEADS, HEAD_DIM))
        outs["vp"].append(v.reshape(batch, seq, FOX_HEADS, HEAD_DIM))
        outs["fp"].append(logf.reshape(batch, seq, FOX_HEADS))
        outs["cp"].append(u.reshape(batch, seq, CONV_CH)[:, seq - (CONV_WIDTH - 1):])

        q, k, v, kb, vb, logf, _, u, sa, sb = _inproj(
            hs, w_main, b_main, w_f, b_f, tri_in, tiles_per_seq=1)
        logf_t = logf.reshape(dbatch, dseq, FOX_HEADS).transpose(0, 2, 1)
        attn = _attn_sample(
            q, kb, vb, logf, logf_t, cache_k[l].reshape(dbatch, past, FOX_WIDTH),
            cache_v[l].reshape(dbatch, past, FOX_WIDTH), cache_logf[l].transpose(0, 2, 1),
            upper_past, batch=dbatch, t=dseq, past=past)
        hist = jnp.pad(state_conv[l], ((0, 0), (HIST - (CONV_WIDTH - 1), 0), (0, 0)))
        mid, rows3 = _merge(u, hist.reshape(dbatch * HIST, CONV_CH), attn, sa, sb, hs, conv_p,
                            wa, wout, g1, b1, n_seq=dbatch, seq=dseq, ts=dseq, hist_from_u=False,
                            alpha=alpha, total_rows=total, row_offset=rows_p, prev=(mid, rows3))
        outs["ks"].append(k.reshape(dbatch, dseq, FOX_HEADS, HEAD_DIM))
        outs["vs"].append(v.reshape(dbatch, dseq, FOX_HEADS, HEAD_DIM))
        outs["fs"].append(logf.reshape(dbatch, dseq, FOX_HEADS))
        u3 = u.reshape(dbatch, dseq, CONV_CH)
        u_ext = jnp.concatenate([state_conv[l], u3], axis=1)
        outs["cs"].append(u_ext[:, -(CONV_WIDTH - 1):])

        hp, hs = _moe(mid, rows3, w_router[l], b_router[l], w_e_gate[l], w_e_up[l], w_e_down[l],
                      w_s_gate[l].astype(BF16), w_s_up[l].astype(BF16), w_s_down[l].astype(BF16),
                      row2(ln2_g[l]), row2(ln2_b[l]), alpha, rows_p)

    st = lambda n: jnp.stack(outs[n])
    return (hp.reshape(batch, seq, D_MODEL), hs.reshape(dbatch, dseq, D_MODEL),
            st("kp"), st("vp"), st("fp"), st("cp"), st("ks"), st("vs"), st("fs"), st("cs"))
```

```python
import functools

import jax
import jax.numpy as jnp
from jax import lax
from jax.experimental import pallas as pl
from jax.experimental.pallas import tpu as pltpu

D_MODEL = 1024
FOX_HEADS = 8
HEAD_DIM = 64
FOX_WIDTH = FOX_HEADS * HEAD_DIM
ATTN_SCALE = HEAD_DIM ** -0.5
LOG2E = 1.4426950408889634
CONV_CH = D_MODEL // 2
CONV_WIDTH = 31
N_EXPERTS = 256
TOP_K = 8
N_GROUPS = 8
GROUP_SIZE = N_EXPERTS // N_GROUPS
TOPK_GROUPS = 4
D_EXPERT = D_MODEL // 4
ROUTED_SCALE = 2.5
LN_EPS = 1e-5

OFF_K = FOX_WIDTH
OFF_V = 2 * FOX_WIDTH
OFF_F = 3 * FOX_WIDTH
OFF_GLU = OFF_F + FOX_HEADS
OFF_GA = OFF_GLU + 2 * CONV_CH
OFF_GB = OFF_GA + D_MODEL

LANES = 128
SUBLANES = 8
ROW_TILES = D_MODEL // LANES
VMEM_LIMIT = 56 * 1024 * 1024

IN_TILE = 512
ATTN_TILE = 256
MERGE_TILE = 256
CONV_SUB = 32
HIST = 32
ROUTER_TILE = 512
MOVE_TILE = 128
DEST_TILE = 1024
EXPERT_BLOCK = 256

F32 = jnp.float32
BF16 = jnp.bfloat16
NEG_INF = float("-inf")
NT_DIMS = (((1,), (1,)), ((), ()))


def _const_spec(shape):
    nd = len(shape)
    return pl.BlockSpec(shape, lambda *_: (0,) * nd, pipeline_mode=pl.Buffered(1))


def _split3(x):
    hi = x.astype(BF16)
    r1 = x - hi.astype(F32)
    mid = r1.astype(BF16)
    lo = (r1 - mid.astype(F32)).astype(BF16)
    return hi, mid, lo


def _dot(a, b):
    return jnp.dot(a, b, preferred_element_type=F32)


def _dot_nt(a, b):
    return lax.dot_general(a, b, NT_DIMS, preferred_element_type=F32)


def _exact_dot(ones_mat, x, *, ones_on_left):
    acc = None
    for part in _split3(x):
        term = _dot(ones_mat, part) if ones_on_left else _dot(part, ones_mat)
        acc = term if acc is None else acc + term
    return acc


def _layer_norm(x, g, b):
    mu = jnp.mean(x, axis=-1, keepdims=True)
    xc = x - mu
    var = jnp.mean(xc * xc, axis=-1, keepdims=True)
    return xc * lax.rsqrt(var + LN_EPS) * g + b


def _log_sigmoid(z):
    return jnp.minimum(z, 0.0) - jnp.log1p(jnp.exp(-jnp.abs(z)))


def _inproj_kernel(x_ref, w_ref, b_ref, wf_ref, bf_ref, tri_ref,
                   q_ref, k_ref, v_ref, kb_ref, vb_ref, logf_ref, c_ref, u_ref,
                   sa_ref, sb_ref, carry_ref, *, tiles_per_seq):
    i = pl.program_id(0)
    xb = x_ref[...].astype(BF16)

    def proj(c0, c1):
        return _dot(xb, w_ref[:, c0:c1]) + b_ref[:, c0:c1]

    q = proj(0, FOX_WIDTH)
    q_ref[...] = (q * (ATTN_SCALE * LOG2E)).astype(BF16)
    k = proj(FOX_WIDTH, 2 * FOX_WIDTH)
    kb_ref[...] = k.astype(BF16)
    v = proj(2 * FOX_WIDTH, 3 * FOX_WIDTH)
    vb_ref[...] = v.astype(BF16)
    for h in range(FOX_HEADS):
        head_rows = pl.ds(h, IN_TILE, stride=FOX_HEADS)
        k_ref[head_rows, :] = k[:, h * HEAD_DIM:(h + 1) * HEAD_DIM]
        v_ref[head_rows, :] = v[:, h * HEAD_DIM:(h + 1) * HEAD_DIM]

    logf = _log_sigmoid(_dot(xb, wf_ref[...]) + bf_ref[...])
    logf_ref[...] = logf[:, :FOX_HEADS]

    @pl.when(i % tiles_per_seq == 0)
    def _():
        carry_ref[...] = jnp.zeros_like(carry_ref)

    c = _exact_dot(tri_ref[...], logf, ones_on_left=True) + carry_ref[...]
    c_ref[...] = c[:, :FOX_HEADS]
    carry_ref[...] = c[IN_TILE - 1:IN_TILE, :]

    g0 = 3 * FOX_WIDTH
    glu_a = proj(g0, g0 + CONV_CH)
    glu_b = proj(g0 + CONV_CH, g0 + 2 * CONV_CH)
    u_ref[...] = glu_a * jax.nn.sigmoid(glu_b)
    g1 = g0 + 2 * CONV_CH
    sa_ref[...] = jax.nn.sigmoid(proj(g1, g1 + D_MODEL)).astype(BF16)
    sb_ref[...] = jax.nn.sigmoid(proj(g1 + D_MODEL, g1 + 2 * D_MODEL)).astype(BF16)


def _inproj(x, w_main, b_main, w_f, b_f, tri, *, tiles_per_seq):
    rows = x.shape[0]
    n_main = w_main.shape[1]
    row_spec = lambda w: pl.BlockSpec((IN_TILE, w), lambda i: (i, 0))
    out_shapes = (
        jax.ShapeDtypeStruct((rows, FOX_WIDTH), BF16),
        jax.ShapeDtypeStruct((rows * FOX_HEADS, HEAD_DIM), F32),
        jax.ShapeDtypeStruct((rows * FOX_HEADS, HEAD_DIM), F32),
        jax.ShapeDtypeStruct((rows, FOX_WIDTH), BF16),
        jax.ShapeDtypeStruct((rows, FOX_WIDTH), BF16),
        jax.ShapeDtypeStruct((rows, FOX_HEADS), F32),
        jax.ShapeDtypeStruct((rows, FOX_HEADS), F32),
        jax.ShapeDtypeStruct((rows, CONV_CH), F32),
        jax.ShapeDtypeStruct((rows, D_MODEL), BF16),
        jax.ShapeDtypeStruct((rows, D_MODEL), BF16),
    )
    head_spec = pl.BlockSpec((IN_TILE * FOX_HEADS, HEAD_DIM), lambda i: (i, 0))
    out_specs = (row_spec(FOX_WIDTH), head_spec, head_spec) + (row_spec(FOX_WIDTH),) * 2 + (
        row_spec(FOX_HEADS),) * 2 + (row_spec(CONV_CH), row_spec(D_MODEL), row_spec(D_MODEL))
    return pl.pallas_call(
        functools.partial(_inproj_kernel, tiles_per_seq=tiles_per_seq),
        grid=(rows // IN_TILE,),
        in_specs=[row_spec(D_MODEL), _const_spec((D_MODEL, n_main)), _const_spec((1, n_main)),
                  _const_spec((D_MODEL, LANES)), _const_spec((1, LANES)),
                  _const_spec((IN_TILE, IN_TILE))],
        out_specs=out_specs,
        out_shape=out_shapes,
        scratch_shapes=[pltpu.VMEM((1, LANES), F32)],
        compiler_params=pltpu.CompilerParams(
            dimension_semantics=("arbitrary",), vmem_limit_bytes=VMEM_LIMIT),
        name="inproj",
    )(x, w_main, b_main, w_f, b_f, tri)


def _attn_prompt_kernel(q_ref, k_ref, v_ref, crow_ref, o_ref, qm_scr, m_scr, acc_scr):
    t = ATTN_TILE
    i = pl.program_id(1)
    lane = lax.broadcasted_iota(jnp.int32, (t, LANES), 1)
    row = lax.broadcasted_iota(jnp.int32, (t, t), 0)
    col = lax.broadcasted_iota(jnp.int32, (t, t), 1)
    causal = col <= row
    wide = lambda x: jnp.concatenate([x] * (t // LANES), axis=1)
    pair_lanes = lambda pair: slice(pair * LANES, (pair + 1) * LANES)

    for h in range(FOX_HEADS):
        qp = q_ref[:, pair_lanes(h // 2)]
        in_head = (lane < HEAD_DIM) if h % 2 == 0 else (lane >= HEAD_DIM)
        qm_scr[h] = jnp.where(in_head, qp, jnp.zeros_like(qp))
        m_scr[h] = jnp.full((t, LANES), NEG_INF, F32)
        acc_scr[h] = jnp.zeros((t, 2 * LANES), F32)
    ones = jnp.ones((t, LANES), BF16)

    def step(j, masked):
        r0 = pl.multiple_of(j * t, t)
        ck = crow_ref[0, :, pl.ds(r0, t)] * LOG2E
        for pair in range(FOX_HEADS // 2):
            kj = k_ref[pl.ds(r0, t), pair_lanes(pair)]
            vj = jnp.concatenate([v_ref[pl.ds(r0, t), pair_lanes(pair)], ones], axis=1)
            for h in (2 * pair, 2 * pair + 1):
                s = _dot_nt(qm_scr[h], kj) - ck[h:h + 1, :]
                if masked:
                    s = jnp.where(causal, s, NEG_INF)
                m_prev = m_scr[h]
                m_new = jnp.maximum(m_prev, jnp.max(s, axis=1, keepdims=True))
                alpha = jnp.exp2(m_prev - m_new)
                p = jnp.exp2(s - wide(m_new))
                acc_scr[h] = wide(alpha) * acc_scr[h] + _dot(p.astype(BF16), vj)
                m_scr[h] = m_new

    def body(j, carry):
        step(j, False)
        return carry

    lax.fori_loop(0, i, body, 0)
    step(i, True)
    for pair in range(FOX_HEADS // 2):
        o0, o1 = (acc_scr[h, :, :LANES] / acc_scr[h, :, LANES:] for h in (2 * pair, 2 * pair + 1))
        o_ref[:, pair_lanes(pair)] = jnp.where(lane < HEAD_DIM, o0, o1).astype(BF16)


def _attn_prompt(q, kb, vb, c_row, *, batch, seq):
    t = ATTN_TILE
    nq = seq // t
    return pl.pallas_call(
        _attn_prompt_kernel,
        grid=(batch, nq),
        in_specs=[
            pl.BlockSpec((t, FOX_WIDTH), lambda b, i: (b * nq + i, 0)),
            pl.BlockSpec((seq, FOX_WIDTH), lambda b, i: (b, 0)),
            pl.BlockSpec((seq, FOX_WIDTH), lambda b, i: (b, 0)),
            pl.BlockSpec((1, FOX_HEADS, seq), lambda b, i: (b, 0, 0)),
        ],
        out_specs=pl.BlockSpec((t, FOX_WIDTH), lambda b, i: (b * nq + i, 0)),
        out_shape=jax.ShapeDtypeStruct((batch * seq, FOX_WIDTH), BF16),
        scratch_shapes=[pltpu.VMEM((FOX_HEADS, t, LANES), BF16),
                        pltpu.VMEM((FOX_HEADS, t, LANES), F32),
                        pltpu.VMEM((FOX_HEADS, t, 2 * LANES), F32)],
        compiler_params=pltpu.CompilerParams(
            dimension_semantics=("arbitrary", "arbitrary"), vmem_limit_bytes=VMEM_LIMIT),
        name="attn_prompt",
    )(q, kb, vb, c_row)


def _attn_sample_kernel(q_ref, kn_ref, vn_ref, lf_ref, lft_ref, ck_ref, cv_ref, clft_ref,
                        upper_ref, o_ref, *, t, past):
    rows = FOX_HEADS * t
    lane_head = lax.broadcasted_iota(jnp.int32, (t, FOX_WIDTH), 1) // HEAD_DIM
    q = q_ref[...]
    q_stack = jnp.concatenate(
        [jnp.where(lane_head == h, q, jnp.zeros_like(q)) for h in range(FOX_HEADS)], axis=0)

    clf = clft_ref[0]
    prefix = _exact_dot(upper_ref[...], clf, ones_on_left=False)
    to_end = prefix[:, past - 1:past] - prefix
    ri = lax.broadcasted_iota(jnp.int32, (t, t), 0)
    ci = lax.broadcasted_iota(jnp.int32, (t, t), 1)
    lower = jnp.where(ci <= ri, 1.0, 0.0).astype(BF16)
    upper = jnp.where(ri <= ci, 1.0, 0.0).astype(BF16)
    cn_col = _exact_dot(lower, lf_ref[...], ones_on_left=True)
    cn_row = _exact_dot(upper, lft_ref[0], ones_on_left=False)

    stack = lambda f: jnp.concatenate([f(h) for h in range(FOX_HEADS)], axis=0)
    cn_stack = stack(lambda h: cn_col[:, h:h + 1])
    bias_c = stack(lambda h: jnp.broadcast_to(to_end[h:h + 1, :], (t, past)))
    bias_n = stack(lambda h: jnp.broadcast_to(cn_row[h:h + 1, :], (t, t)))

    kc = ck_ref[0].astype(BF16)
    vc = cv_ref[0].astype(BF16)
    s_c = _dot_nt(q_stack, kc) + (bias_c + cn_stack) * LOG2E
    s_n = _dot_nt(q_stack, kn_ref[...]) + (cn_stack - bias_n) * LOG2E
    tq = lax.broadcasted_iota(jnp.int32, (rows, t), 0) % t
    tk = lax.broadcasted_iota(jnp.int32, (rows, t), 1)
    s_n = jnp.where(tk <= tq, s_n, NEG_INF)
    m = jnp.maximum(jnp.max(s_c, axis=1, keepdims=True), jnp.max(s_n, axis=1, keepdims=True))
    p_c = jnp.exp2(s_c - m)
    p_n = jnp.exp2(s_n - m)
    denom = jnp.sum(p_c, axis=1, keepdims=True) + jnp.sum(p_n, axis=1, keepdims=True)
    o_stack = (_dot(p_c.astype(BF16), vc) + _dot(p_n.astype(BF16), vn_ref[...])) / denom
    out = jnp.zeros((t, FOX_WIDTH), F32)
    for h in range(FOX_HEADS):
        out = out + jnp.where(lane_head == h, o_stack[h * t:(h + 1) * t, :], 0.0)
    o_ref[...] = out.astype(BF16)


def _attn_sample(q, kb, vb, logf, logf_t, cache_k, cache_v, cache_logf_t, upper, *, batch, t, past):
    row_spec = lambda w: pl.BlockSpec((t, w), lambda b: (b, 0))
    return pl.pallas_call(
        functools.partial(_attn_sample_kernel, t=t, past=past),
        grid=(batch,),
        in_specs=[
            row_spec(FOX_WIDTH), row_spec(FOX_WIDTH), row_spec(FOX_WIDTH), row_spec(FOX_HEADS),
            pl.BlockSpec((1, FOX_HEADS, t), lambda b: (b, 0, 0)),
            pl.BlockSpec((1, past, FOX_WIDTH), lambda b: (b, 0, 0)),
            pl.BlockSpec((1, past, FOX_WIDTH), lambda b: (b, 0, 0)),
            pl.BlockSpec((1, FOX_HEADS, past), lambda b: (b, 0, 0)),
            _const_spec((past, past)),
        ],
        out_specs=row_spec(FOX_WIDTH),
        out_shape=jax.ShapeDtypeStruct((batch * t, FOX_WIDTH), BF16),
        compiler_params=pltpu.CompilerParams(
            dimension_semantics=("arbitrary",), vmem_limit_bytes=VMEM_LIMIT),
        name="attn_sample",
    )(q, kb, vb, logf, logf_t, cache_k, cache_v, cache_logf_t, upper)


def _merge_kernel(u_ref, hist_ref, attn_ref, sa_ref, sb_ref, x_ref,
                  cw_ref, cb_ref, cg_ref, cbeta_ref, wb_ref, bb_ref, wa_ref, wout_ref,
                  g1_ref, b1_ref, *rest, ts, tiles_per_seq, n_tiles, zero_first_hist, alpha,
                  aliased):
    if aliased:
        rest = rest[2:]
    mid_ref, mid3_ref, ue_scr, ph_scr, h_scr = rest
    i = pl.program_id(0)

    @pl.when(i >= n_tiles)
    def _():
        mid_ref[...] = jnp.zeros_like(mid_ref)
        mid3_ref[...] = jnp.zeros_like(mid3_ref)

    @pl.when(i < n_tiles)
    def _():
        _merge_tile(u_ref, hist_ref, attn_ref, sa_ref, sb_ref, x_ref, cw_ref, cb_ref, cg_ref,
                    cbeta_ref, wb_ref, bb_ref, wa_ref, wout_ref, g1_ref, b1_ref, mid_ref,
                    mid3_ref, ue_scr, ph_scr, h_scr, ts=ts, alpha=alpha,
                    zero_hist=(i % tiles_per_seq == 0) if zero_first_hist else None)


def _merge_tile(u_ref, hist_ref, attn_ref, sa_ref, sb_ref, x_ref, cw_ref, cb_ref, cg_ref,
                cbeta_ref, wb_ref, bb_ref, wa_ref, wout_ref, g1_ref, b1_ref, mid_ref, mid3_ref,
                ue_scr, ph_scr, h_scr, *, ts, alpha, zero_hist):
    hist = hist_ref[...]
    if zero_hist is not None:
        hist = jnp.where(zero_hist, 0.0, hist)
    ue_scr[0:HIST, :] = hist
    ue_scr[HIST:HIST + ts, :] = u_ref[...]
    lead = HIST - (CONV_WIDTH - 1)
    for r in range(SUBLANES):
        span = ts + (CONV_WIDTH - 1 - r) // SUBLANES * SUBLANES
        ph_scr[r, 0:span, :] = ue_scr[lead + r:lead + r + span, :]

    def conv_rows(rt, carry):
        base = pl.multiple_of(rt * CONV_SUB, CONV_SUB)
        acc = jnp.zeros((CONV_SUB, CONV_CH), F32)
        for j in range(CONV_WIDTH):
            r, a = j % SUBLANES, j // SUBLANES
            acc = acc + cw_ref[j:j + 1, :] * ph_scr[r, pl.ds(base + a * SUBLANES, CONV_SUB), :]
        h_scr[pl.ds(base, CONV_SUB), :] = acc
        return carry

    lax.fori_loop(0, ts // CONV_SUB, conv_rows, 0)
    h = _layer_norm(h_scr[...] + cb_ref[...], cg_ref[...], cbeta_ref[...])
    h = h * jax.nn.sigmoid(h)
    conv_out = _dot(h.astype(BF16), wb_ref[...]) + bb_ref[...]
    attn_out = _dot(attn_ref[...], wa_ref[...])
    m = sa_ref[...].astype(F32) * attn_out + sb_ref[...].astype(F32) * conv_out
    z = alpha * x_ref[...] + _dot(m.astype(BF16), wout_ref[...])
    mid = _layer_norm(z, g1_ref[...], b1_ref[...])
    mid_ref[...] = mid
    for j in range(ROW_TILES):
        mid3_ref[pl.ds(j, ts, stride=ROW_TILES), :] = mid[:, j * LANES:(j + 1) * LANES]


def _merge(u, hist, attn, sa, sb, x, conv_p, wa, wout, g1, b1, *, n_seq, seq, ts, hist_from_u,
           alpha, total_rows, row_offset, prev=None):
    nt = seq // ts
    n_tiles = n_seq * nt
    off = row_offset // ts
    grid_tiles = n_tiles if prev is not None else total_rows // ts
    src = lambda i: jnp.minimum(i, n_tiles - 1)
    row_spec = lambda w: pl.BlockSpec((ts, w), lambda i: (src(i), 0))
    if hist_from_u:
        per = ts // HIST
        hist_spec = pl.BlockSpec((HIST, CONV_CH), lambda i: (jnp.maximum(src(i) * per - 1, 0), 0))
    else:
        hist_spec = pl.BlockSpec((HIST, CONV_CH), lambda i: (src(i) // nt, 0))
    cw, cb, cg, cbeta, wb, bb = conv_p
    consts = [cw, cb, cg, cbeta, wb, bb, wa, wout, g1, b1]
    in_specs = [row_spec(CONV_CH), hist_spec, row_spec(FOX_WIDTH), row_spec(D_MODEL),
                row_spec(D_MODEL), row_spec(D_MODEL)] + [_const_spec(c.shape) for c in consts]
    args = [u, hist, attn, sa, sb, x] + consts
    aliases = {}
    if prev is not None:
        in_specs += [pl.BlockSpec(memory_space=pl.ANY)] * 2
        aliases = {len(args): 0, len(args) + 1: 1}
        args += list(prev)
    span = ts + (CONV_WIDTH - 1) // SUBLANES * SUBLANES
    return pl.pallas_call(
        functools.partial(_merge_kernel, ts=ts, tiles_per_seq=nt, n_tiles=n_tiles,
                          zero_first_hist=hist_from_u, alpha=alpha, aliased=prev is not None),
        grid=(grid_tiles,),
        in_specs=in_specs,
        out_specs=(pl.BlockSpec((ts, D_MODEL), lambda i: (off + i, 0)),
                   pl.BlockSpec((ts * ROW_TILES, LANES), lambda i: (off + i, 0))),
        out_shape=(jax.ShapeDtypeStruct((total_rows, D_MODEL), F32),
                   jax.ShapeDtypeStruct((total_rows * ROW_TILES, LANES), F32)),
        scratch_shapes=[pltpu.VMEM((HIST + ts, CONV_CH), F32),
                        pltpu.VMEM((SUBLANES, span, CONV_CH), F32),
                        pltpu.VMEM((ts, CONV_CH), F32)],
        input_output_aliases=aliases,
        compiler_params=pltpu.CompilerParams(
            dimension_semantics=("arbitrary",), vmem_limit_bytes=VMEM_LIMIT),
        name="merge",
    )(*args)


def _router_kernel(mid_ref, wr_hi_ref, wr_lo_ref, br_ref, before_ref,
                   eidx_ref, gate_ref, rank_ref, cnt_ref, carry_scr):
    tr = ROUTER_TILE
    i = pl.program_id(0)

    @pl.when(i == 0)
    def _():
        carry_scr[...] = jnp.zeros_like(carry_scr)

    x = mid_ref[...]
    x_hi = x.astype(BF16)
    x_lo = (x - x_hi.astype(F32)).astype(BF16)
    wr_hi = wr_hi_ref[...]
    logits = _dot_nt(wr_hi, x_hi) + _dot_nt(wr_hi, x_lo) + _dot_nt(wr_lo_ref[...], x_hi)
    scores = jax.nn.sigmoid(logits)
    sel = scores + br_ref[...]

    sel3 = sel.reshape(N_GROUPS, GROUP_SIZE, tr)
    in_group = lax.broadcasted_iota(jnp.int32, sel3.shape, 1)
    m1 = jnp.max(sel3, axis=1, keepdims=True)
    first = jnp.min(jnp.where(sel3 == m1, in_group, GROUP_SIZE), axis=1, keepdims=True)
    m2 = jnp.max(jnp.where(in_group == first, NEG_INF, sel3), axis=1, keepdims=True)
    gs = m1 + m2
    gi = lax.broadcasted_iota(jnp.int32, gs.shape, 0)
    beaten = jnp.zeros(gs.shape, F32)
    for g in range(N_GROUPS):
        other = gs[g:g + 1]
        wins = (other > gs) | ((other == gs) & (g < gi))
        beaten = beaten + jnp.where(wins, 1.0, 0.0)
    drop = jnp.where(beaten < TOPK_GROUPS, 0.0, NEG_INF)
    cur = (sel3 + drop).reshape(N_EXPERTS, tr)

    ei = lax.broadcasted_iota(jnp.int32, (N_EXPERTS, tr), 0)
    idxs, vals = [], []
    picked = jnp.zeros((N_EXPERTS, tr), F32)
    for _ in range(TOP_K):
        m = jnp.max(cur, axis=0, keepdims=True)
        idx = jnp.min(jnp.where(cur == m, ei, N_EXPERTS), axis=0, keepdims=True)
        hit = ei == idx
        vals.append(jnp.sum(jnp.where(hit, scores, 0.0), axis=0, keepdims=True))
        idxs.append(idx)
        picked = picked + jnp.where(hit, 1.0, 0.0)
        cur = jnp.where(hit, NEG_INF, cur)

    total = vals[0]
    for v in vals[1:]:
        total = total + v
    for k in range(TOP_K):
        gate_ref[k:k + 1, :] = vals[k] / total * ROUTED_SCALE
        eidx_ref[k:k + 1, :] = idxs[k]

    ahead = _dot(picked.astype(BF16), before_ref[...]) + carry_scr[:, 0:1]
    for k in range(TOP_K):
        rank = jnp.sum(jnp.where(ei == idxs[k], ahead, 0.0), axis=0, keepdims=True)
        rank_ref[k:k + 1, :] = rank.astype(jnp.int32)
    carry_scr[...] = carry_scr[...] + jnp.sum(picked, axis=1, keepdims=True)
    cnt_ref[...] = carry_scr[...]


def _router(mid, wr_hi, wr_lo, br, before):
    tr = ROUTER_TILE
    tokens = mid.shape[0]
    tok_spec = pl.BlockSpec((TOP_K, tr), lambda i: (0, i))
    return pl.pallas_call(
        _router_kernel,
        grid=(tokens // tr,),
        in_specs=[pl.BlockSpec((tr, D_MODEL), lambda i: (i, 0)),
                  _const_spec((N_EXPERTS, D_MODEL)), _const_spec((N_EXPERTS, D_MODEL)),
                  _const_spec((N_EXPERTS, 1)), _const_spec((tr, tr))],
        out_specs=(tok_spec, tok_spec, tok_spec,
                   pl.BlockSpec((N_EXPERTS, LANES), lambda i: (0, 0))),
        out_shape=(jax.ShapeDtypeStruct((TOP_K, tokens), jnp.int32),
                   jax.ShapeDtypeStruct((TOP_K, tokens), F32),
                   jax.ShapeDtypeStruct((TOP_K, tokens), jnp.int32),
                   jax.ShapeDtypeStruct((N_EXPERTS, LANES), F32)),
        scratch_shapes=[pltpu.VMEM((N_EXPERTS, LANES), F32)],
        compiler_params=pltpu.CompilerParams(
            dimension_semantics=("arbitrary",), vmem_limit_bytes=VMEM_LIMIT),
        name="router",
    )(mid, wr_hi, wr_lo, br, before)


def _dest_kernel(eidx_ref, rank_ref, starts_ref, dest_ref):
    tt = MOVE_TILE
    tokens = eidx_ref.shape[1]
    ei = lax.broadcasted_iota(jnp.int32, (N_EXPERTS, tokens), 0)
    starts = starts_ref[...]
    for k in range(TOP_K):
        hit = ei == eidx_ref[k:k + 1, :]
        start = jnp.sum(jnp.where(hit, starts, 0.0), axis=0, keepdims=True)
        dest = start.astype(jnp.int32) + rank_ref[k:k + 1, :]
        for c in range(tokens // tt):
            dest_ref[c, k:k + 1, :] = dest[:, c * tt:(c + 1) * tt]


def _dest(eidx, rank, starts):
    tt = MOVE_TILE
    tokens = eidx.shape[1]
    step = DEST_TILE
    tok_spec = pl.BlockSpec((TOP_K, step), lambda i: (0, i))
    return pl.pallas_call(
        _dest_kernel,
        grid=(tokens // step,),
        in_specs=[tok_spec, tok_spec, _const_spec((N_EXPERTS, 1))],
        out_specs=pl.BlockSpec((step // tt, TOP_K, tt), lambda i: (i, 0, 0)),
        out_shape=jax.ShapeDtypeStruct((tokens // tt, TOP_K, tt), jnp.int32),
        compiler_params=pltpu.CompilerParams(dimension_semantics=("arbitrary",)),
        name="dest",
    )(eidx, rank, starts)


def _row_slice(ref, row):
    return ref.at[pl.ds(pl.multiple_of(row * ROW_TILES, ROW_TILES), ROW_TILES), :]


def _dispatch_kernel(dest_ref, rows_ref, xs_ref, zeros_scr, sem, *, n_rows):
    tt = MOVE_TILE

    @pl.when(pl.program_id(0) == 0)
    def _():
        zeros_scr[...] = jnp.zeros_like(zeros_scr)
        tail = pltpu.make_async_copy(
            zeros_scr, xs_ref.at[pl.ds(n_rows * ROW_TILES, EXPERT_BLOCK * ROW_TILES), :], sem)
        tail.start()
        tail.wait()

    def issue(t, carry):
        src = _row_slice(rows_ref, t)
        for k in range(TOP_K):
            pltpu.make_async_copy(src, _row_slice(xs_ref, dest_ref[k * tt + t]), sem).start(
                priority=k % 2)
        return carry

    lax.fori_loop(0, tt, issue, 0)
    for _ in range(TOP_K):
        pltpu.make_async_copy(rows_ref, xs_ref.at[pl.ds(0, tt * ROW_TILES), :], sem).wait()


def _dispatch(dest_flat, rows3):
    tt = MOVE_TILE
    tokens = rows3.shape[0] // ROW_TILES
    n_rows = tokens * TOP_K
    return pl.pallas_call(
        functools.partial(_dispatch_kernel, n_rows=n_rows),
        grid=(tokens // tt,),
        in_specs=[pl.BlockSpec((tt * TOP_K,), lambda i: (i,), memory_space=pltpu.SMEM),
                  pl.BlockSpec((tt * ROW_TILES, LANES), lambda i: (i, 0))],
        out_specs=pl.BlockSpec(memory_space=pl.ANY),
        out_shape=jax.ShapeDtypeStruct(((n_rows + EXPERT_BLOCK) * ROW_TILES, LANES), F32),
        scratch_shapes=[pltpu.VMEM((EXPERT_BLOCK * ROW_TILES, LANES), F32),
                        pltpu.SemaphoreType.DMA(())],
        compiler_params=pltpu.CompilerParams(dimension_semantics=("arbitrary",)),
        name="dispatch",
    )(dest_flat, rows3)


_PIECES = tuple(EXPERT_BLOCK >> s for s in range(EXPERT_BLOCK.bit_length()))


def _expert_kernel(cexp_ref, crow_ref, cn_ref, cnew_ref, cnext_ref, nused_ref,
                   xs_hbm, wg_hbm, wu_hbm, wd_hbm, ys_hbm,
                   xbuf, ybuf, wg_buf, wu_buf, wd_buf, wgb, wub, wdb, xsem, ysem, wsem):
    bm = EXPERT_BLOCK
    i = pl.program_id(0)
    n_used = nused_ref[0]
    cur = i % 2

    def weight_copies(e, s):
        return [pltpu.make_async_copy(hbm.at[e], buf.at[s], wsem.at[s])
                for hbm, buf in ((wg_hbm, wg_buf), (wu_hbm, wu_buf), (wd_hbm, wd_buf))]

    def rows_in(j, s):
        first = pl.multiple_of(crow_ref[j] * ROW_TILES, ROW_TILES)
        return pltpu.make_async_copy(xs_hbm.at[pl.ds(first, bm * ROW_TILES), :], xbuf.at[s],
                                     xsem.at[s])

    def rows_out(j, s, act):
        n, row0 = cn_ref[j], crow_ref[j]
        done = jnp.int32(0)
        for p in _PIECES:
            has = (n & p) != 0

            @pl.when(has)
            def _(done=done, p=p):
                src = pl.multiple_of(done * ROW_TILES, ROW_TILES)
                dst = pl.multiple_of((row0 + done) * ROW_TILES, ROW_TILES)
                act(pltpu.make_async_copy(ybuf.at[s, pl.ds(src, p * ROW_TILES), :],
                                          ys_hbm.at[pl.ds(dst, p * ROW_TILES), :], ysem.at[s]))

            done = done + jnp.where(has, p, 0)

    start = lambda c: c.start()
    wait = lambda c: c.wait()

    @pl.when(i == 0)
    def _():
        rows_in(0, 0).start()
        for c in weight_copies(cexp_ref[0], 0):
            c.start()

    @pl.when(i < n_used)
    def _():
        slot = cnew_ref[i]

        @pl.when(slot >= 0)
        def _():
            for c in weight_copies(cexp_ref[i], slot):
                c.wait()
            nxt = cnext_ref[i]

            @pl.when(nxt >= 0)
            def _():
                for c in weight_copies(nxt, 1 - slot):
                    c.start()

            wgb[...] = wg_buf[slot].astype(BF16)
            wub[...] = wu_buf[slot].astype(BF16)
            wdb[...] = wd_buf[slot].astype(BF16)

        @pl.when(i + 1 < n_used)
        def _():
            rows_in(i + 1, 1 - cur).start()

        rows_in(i, cur).wait()

        @pl.when(i >= 2)
        def _():
            rows_out(i - 2, cur, wait)

        x = jnp.concatenate(
            [xbuf[cur, pl.ds(j, bm, stride=ROW_TILES), :] for j in range(ROW_TILES)], axis=1)
        xb = x.astype(BF16)
        g = _dot(xb, wgb[...])
        u = _dot(xb, wub[...])
        h = (g * jax.nn.sigmoid(g) * u).astype(BF16)
        y = _dot(h, wdb[...])
        for j in range(ROW_TILES):
            ybuf[cur, pl.ds(j, bm, stride=ROW_TILES), :] = y[:, j * LANES:(j + 1) * LANES]
        rows_out(i, cur, start)

        @pl.when(i == n_used - 1)
        def _():
            rows_out(i, cur, wait)

            @pl.when(i >= 1)
            def _():
                rows_out(i - 1, 1 - cur, wait)


def _experts(cexp, crow, cn, cnew, cnext, nused, xs, wg, wu, wd, n_rows):
    bm = EXPERT_BLOCK
    any_spec = pl.BlockSpec(memory_space=pl.ANY)
    grid_spec = pltpu.PrefetchScalarGridSpec(
        num_scalar_prefetch=6,
        grid=(cexp.shape[0],),
        in_specs=[any_spec] * 4,
        out_specs=any_spec,
        scratch_shapes=[pltpu.VMEM((2, bm * ROW_TILES, LANES), F32),
                        pltpu.VMEM((2, bm * ROW_TILES, LANES), F32),
                        pltpu.VMEM((2, D_MODEL, D_EXPERT), F32),
                        pltpu.VMEM((2, D_MODEL, D_EXPERT), F32),
                        pltpu.VMEM((2, D_EXPERT, D_MODEL), F32),
                        pltpu.VMEM((D_MODEL, D_EXPERT), BF16),
                        pltpu.VMEM((D_MODEL, D_EXPERT), BF16),
                        pltpu.VMEM((D_EXPERT, D_MODEL), BF16),
                        pltpu.SemaphoreType.DMA((2,)),
                        pltpu.SemaphoreType.DMA((2,)),
                        pltpu.SemaphoreType.DMA((2,))],
    )
    return pl.pallas_call(
        _expert_kernel,
        grid_spec=grid_spec,
        out_shape=jax.ShapeDtypeStruct((n_rows * ROW_TILES, LANES), F32),
        compiler_params=pltpu.CompilerParams(
            dimension_semantics=("arbitrary",), vmem_limit_bytes=VMEM_LIMIT),
        name="experts",
    )(cexp, crow, cn, cnew, cnext, nused, xs, wg, wu, wd)


def _combine_kernel(dest_ref, dest_next_ref, ys_ref, gate_ref, mid_ref, wsg_ref, wsu_ref,
                    wsd_ref, g2_ref, b2_ref, out_a_ref, out_b_ref, buf, gate_scr, routed_scr, sem,
                    *, alpha, steps_a):
    tt = MOVE_TILE
    i = pl.program_id(0)
    n = pl.num_programs(0)
    cur = i % 2

    def start_rows(slots_ref, b):
        def issue(t, carry):
            for k in range(TOP_K):
                pltpu.make_async_copy(_row_slice(ys_ref, slots_ref[k * tt + t]),
                                      _row_slice(buf.at[b], k * tt + t),
                                      sem.at[b]).start(priority=k % 2)
            return carry
        lax.fori_loop(0, tt, issue, 0)

    @pl.when(i == 0)
    def _():
        start_rows(dest_ref, cur)

    @pl.when(i + 1 < n)
    def _():
        start_rows(dest_next_ref, 1 - cur)

    mid = mid_ref[...]
    xb = mid.astype(BF16)
    g = _dot(xb, wsg_ref[...])
    h = (g * jax.nn.sigmoid(g) * _dot(xb, wsu_ref[...])).astype(BF16)
    acc = alpha * mid + _dot(h, wsd_ref[...])

    pltpu.make_async_copy(ys_ref.at[pl.ds(0, TOP_K * tt * ROW_TILES), :], buf.at[cur],
                          sem.at[cur]).wait()
    gates = gate_ref[...]
    for k in range(TOP_K):
        gate_scr[k] = jnp.broadcast_to(gates[:, k:k + 1], (tt, LANES))
    for j in range(ROW_TILES):
        part = jnp.zeros((tt, LANES), F32)
        for k in range(TOP_K):
            part = part + gate_scr[k] * buf[cur, pl.ds(k * tt * ROW_TILES + j, tt,
                                                        stride=ROW_TILES), :]
        routed_scr[:, j * LANES:(j + 1) * LANES] = part
    out = _layer_norm(acc + routed_scr[...], g2_ref[...], b2_ref[...])

    @pl.when(i < steps_a)
    def _():
        out_a_ref[...] = out

    @pl.when(i >= steps_a)
    def _():
        out_b_ref[...] = out


def _combine(dest_flat, ys, gate, mid, wsg, wsu, wsd, g2, b2, *, alpha, rows_a):
    tt = MOVE_TILE
    tokens = mid.shape[0]
    n = tokens // tt
    steps_a = rows_a // tt
    consts = [wsg, wsu, wsd, g2, b2]
    slots_spec = lambda f: pl.BlockSpec((tt * TOP_K,), f, memory_space=pltpu.SMEM)
    return pl.pallas_call(
        functools.partial(_combine_kernel, alpha=alpha, steps_a=steps_a),
        grid=(n,),
        in_specs=[slots_spec(lambda i: (i,)),
                  slots_spec(lambda i: (jnp.minimum(i + 1, n - 1),)),
                  pl.BlockSpec(memory_space=pl.ANY),
                  pl.BlockSpec((tt, TOP_K), lambda i: (i, 0)),
                  pl.BlockSpec((tt, D_MODEL), lambda i: (i, 0))]
                 + [_const_spec(c.shape) for c in consts],
        out_specs=(pl.BlockSpec((tt, D_MODEL), lambda i: (jnp.minimum(i, steps_a - 1), 0)),
                   pl.BlockSpec((tt, D_MODEL), lambda i: (jnp.maximum(i - steps_a, 0), 0))),
        out_shape=(jax.ShapeDtypeStruct((rows_a, D_MODEL), F32),
                   jax.ShapeDtypeStruct((tokens - rows_a, D_MODEL), F32)),
        scratch_shapes=[pltpu.VMEM((2, TOP_K * tt * ROW_TILES, LANES), F32),
                        pltpu.VMEM((TOP_K, tt, LANES), F32),
                        pltpu.VMEM((tt, D_MODEL), F32),
                        pltpu.SemaphoreType.DMA((2,))],
        compiler_params=pltpu.CompilerParams(
            dimension_semantics=("arbitrary",), vmem_limit_bytes=VMEM_LIMIT),
        name="combine",
    )(dest_flat, dest_flat, ys, gate, mid, *consts)


def _tri(n, *, lower):
    r = lax.broadcasted_iota(jnp.int32, (n, n), 0)
    c = lax.broadcasted_iota(jnp.int32, (n, n), 1)
    return jnp.where((c <= r) if lower else (r <= c), 1.0, 0.0).astype(BF16)


def _moe(mid, rows3, w_router, b_router, w_e_gate, w_e_up, w_e_down, wsg, wsu, wsd, g2, b2, alpha,
         rows_a):
    tokens = mid.shape[0]
    bm = EXPERT_BLOCK
    wr_t = w_router.T
    wr_hi = wr_t.astype(BF16)
    wr_lo = (wr_t - wr_hi.astype(F32)).astype(BF16)
    r = lax.broadcasted_iota(jnp.int32, (ROUTER_TILE, ROUTER_TILE), 0)
    c = lax.broadcasted_iota(jnp.int32, (ROUTER_TILE, ROUTER_TILE), 1)
    before = jnp.where(r < c, 1.0, 0.0).astype(BF16)
    eidx, gate, rank, cnt = _router(mid, wr_hi, wr_lo, b_router.reshape(N_EXPERTS, 1), before)

    i32 = lambda a: a.astype(jnp.int32)
    experts = jnp.arange(N_EXPERTS, dtype=jnp.int32)
    counts = i32(cnt[:, 0])
    starts = jnp.cumsum(counts) - counts
    n_ch = (counts + bm - 1) // bm
    ch_ends = jnp.cumsum(n_ch)
    ch_starts = ch_ends - n_ch
    n_used = ch_ends[-1]
    max_chunks = tokens * TOP_K // bm + N_EXPERTS
    ci = jnp.arange(max_chunks, dtype=jnp.int32)
    cc = jnp.minimum(ci, n_used - 1)
    cexp = jnp.minimum(jnp.sum(i32(ch_ends[None, :] <= cc[:, None]), axis=1), N_EXPERTS - 1)
    hot = cexp[:, None] == experts[None, :]
    lookup = lambda table: jnp.sum(jnp.where(hot, table[None, :], 0), axis=1)
    k_in_expert = cc - lookup(ch_starts)
    crow = lookup(starts) + k_in_expert * bm
    used = ci < n_used
    cn = jnp.where(used, jnp.clip(lookup(counts) - k_in_expert * bm, 0, bm), 0)
    is_first = used & (k_in_expert == 0)
    cnew = jnp.where(is_first, (jnp.cumsum(i32(is_first)) - 1) % 2, -1)
    later = (experts[None, :] > experts[:, None]) & (n_ch[None, :] > 0)
    next_expert = jnp.min(jnp.where(later, experts[None, :], N_EXPERTS), axis=1)
    cnext = lookup(jnp.where(next_expert < N_EXPERTS, next_expert, -1))
    cnext = jnp.where(is_first, cnext, -1)
    dest = _dest(eidx, rank, starts.astype(F32).reshape(N_EXPERTS, 1)).reshape(tokens * TOP_K)

    xs = _dispatch(dest, rows3)
    ys = _experts(cexp, i32(crow), i32(cn), i32(cnew), i32(cnext), i32(n_used).reshape(1), xs,
                  w_e_gate, w_e_up, w_e_down, tokens * TOP_K)
    return _combine(dest, ys, gate.T, mid, wsg, wsu, wsd, g2, b2, alpha=alpha, rows_a=rows_a)


def kernel(x_prompt, x_sample, cache_k, cache_v, cache_logf, state_conv, w_in, b_in, conv_w,
           conv_b, conv_ln_g, conv_ln_b, w_a, w_b, b_b, w_out, ln1_g, ln1_b, w_router, b_router,
           w_e_gate, w_e_up, w_e_down, w_s_gate, w_s_up, w_s_down, ln2_g, ln2_b):
    depth = w_in.shape[0]
    alpha = float((2 * depth) ** 0.25)
    batch, seq, _ = x_prompt.shape
    dbatch, dseq, _ = x_sample.shape
    past = cache_k.shape[2]
    rows_p, rows_s = batch * seq, dbatch * dseq
    total = rows_p + rows_s
    assert seq % IN_TILE == 0 and rows_s % IN_TILE == 0 and seq % MERGE_TILE == 0
    assert total % ROUTER_TILE == 0 and dseq == HIST and rows_p % dseq == 0
    assert total % DEST_TILE == 0 and (total * TOP_K) % EXPERT_BLOCK == 0

    hp = x_prompt.reshape(rows_p, D_MODEL)
    hs = x_sample.reshape(rows_s, D_MODEL)
    tri_in = _tri(IN_TILE, lower=True)
    upper_past = _tri(past, lower=False)
    row2 = lambda a: a.reshape(1, -1)
    outs = {n: [] for n in ("kp", "vp", "fp", "cp", "ks", "vs", "fs", "cs")}

    for l in range(depth):
        w = w_in[l]
        b = b_in[l]
        main_cols = lambda a: jnp.concatenate([a[..., :OFF_F], a[..., OFF_GLU:]], axis=-1)
        w_main = main_cols(w).astype(BF16)
        b_main = row2(main_cols(b))
        w_f = jnp.pad(w[:, OFF_F:OFF_GLU], ((0, 0), (0, LANES - FOX_HEADS))).astype(BF16)
        b_f = row2(jnp.pad(b[OFF_F:OFF_GLU], (0, LANES - FOX_HEADS)))
        cw = jnp.pad(conv_w[l], ((0, 1), (0, 0)))
        conv_p = (cw, row2(conv_b[l]), row2(conv_ln_g[l]), row2(conv_ln_b[l]),
                  w_b[l].astype(BF16), row2(b_b[l]))
        wa, wout = w_a[l].astype(BF16), w_out[l].astype(BF16)
        g1, b1 = row2(ln1_g[l]), row2(ln1_b[l])

        q, k, v, kb, vb, logf, c, u, sa, sb = _inproj(
            hp, w_main, b_main, w_f, b_f, tri_in, tiles_per_seq=seq // IN_TILE)
        c_row = c.reshape(batch, seq, FOX_HEADS).transpose(0, 2, 1)
        attn = _attn_prompt(q, kb, vb, c_row, batch=batch, seq=seq)
        mid, rows3 = _merge(u, u, attn, sa, sb, hp, conv_p, wa, wout, g1, b1,
                            n_seq=batch, seq=seq, ts=MERGE_TILE, hist_from_u=True, alpha=alpha,
                            total_rows=total, row_offset=0)
        outs["kp"].append(k.reshape(batch, seq, FOX_HEADS, HEAD_DIM))
        outs["vp"].append(v.reshape(batch, seq, FOX_HEADS, HEAD_DIM))
        outs["fp"].append(logf.reshape(batch, seq, FOX_HEADS))
        outs["cp"].append(u.reshape(batch, seq, CONV_CH)[:, seq - (CONV_WIDTH - 1):])

        q, k, v, kb, vb, logf, _, u, sa, sb = _inproj(
            hs, w_main, b_main, w_f, b_f, tri_in, tiles_per_seq=1)
        logf_t = logf.reshape(dbatch, dseq, FOX_HEADS).transpose(0, 2, 1)
        attn = _attn_sample(
            q, kb, vb, logf, logf_t, cache_k[l].reshape(dbatch, past, FOX_WIDTH),
            cache_v[l].reshape(dbatch, past, FOX_WIDTH), cache_logf[l].transpose(0, 2, 1),
            upper_past, batch=dbatch, t=dseq, past=past)
        hist = jnp.pad(state_conv[l], ((0, 0), (HIST - (CONV_WIDTH - 1), 0), (0, 0)))
        mid, rows3 = _merge(u, hist.reshape(dbatch * HIST, CONV_CH), attn, sa, sb, hs, conv_p,
                            wa, wout, g1, b1, n_seq=dbatch, seq=dseq, ts=dseq, hist_from_u=False,
                            alpha=alpha, total_rows=total, row_offset=rows_p, prev=(mid, rows3))
        outs["ks"].append(k.reshape(dbatch, dseq, FOX_HEADS, HEAD_DIM))
        outs["vs"].append(v.reshape(dbatch, dseq, FOX_HEADS, HEAD_DIM))
        outs["fs"].append(logf.reshape(dbatch, dseq, FOX_HEADS))
        u3 = u.reshape(dbatch, dseq, CONV_CH)
        u_ext = jnp.concatenate([state_conv[l], u3], axis=1)
        outs["cs"].append(u_ext[:, -(CONV_WIDTH - 1):])

        hp, hs = _moe(mid, rows3, w_router[l], b_router[l], w_e_gate[l], w_e_up[l], w_e_down[l],
                      w_s_gate[l].astype(BF16), w_s_up[l].astype(BF16), w_s_down[l].astype(BF16),
                      row2(ln2_g[l]), row2(ln2_b[l]), alpha, rows_p)

    st = lambda n: jnp.stack(outs[n])
    return (hp.reshape(batch, seq, D_MODEL), hs.reshape(dbatch, dseq, D_MODEL),
            st("kp"), st("vp"), st("fp"), st("cp"), st("ks"), st("vs"), st("fs"), st("cs"))
```

```python
import functools

import jax
import jax.numpy as jnp
from jax import lax
from jax.experimental import pallas as pl
from jax.experimental.pallas import tpu as pltpu

D_MODEL = 1024
FOX_HEADS = 8
HEAD_DIM = 64
FOX_WIDTH = FOX_HEADS * HEAD_DIM
ATTN_SCALE = HEAD_DIM ** -0.5
LOG2E = 1.4426950408889634
CONV_CH = D_MODEL // 2
CONV_WIDTH = 31
N_EXPERTS = 256
TOP_K = 8
N_GROUPS = 8
GROUP_SIZE = N_EXPERTS // N_GROUPS
TOPK_GROUPS = 4
D_EXPERT = D_MODEL // 4
ROUTED_SCALE = 2.5
LN_EPS = 1e-5

OFF_K = FOX_WIDTH
OFF_V = 2 * FOX_WIDTH
OFF_F = 3 * FOX_WIDTH
OFF_GLU = OFF_F + FOX_HEADS
OFF_GA = OFF_GLU + 2 * CONV_CH
OFF_GB = OFF_GA + D_MODEL

LANES = 128
SUBLANES = 8
ROW_TILES = D_MODEL // LANES
VMEM_LIMIT = 56 * 1024 * 1024

IN_TILE = 512
ATTN_TILE = 256
MERGE_TILE = 256
CONV_SUB = 32
HIST = 32
ROUTER_TILE = 512
MOVE_TILE = 128
DEST_TILE = 1024
EXPERT_BLOCK = 256

F32 = jnp.float32
BF16 = jnp.bfloat16
NEG_INF = float("-inf")
NT_DIMS = (((1,), (1,)), ((), ()))


def _const_spec(shape):
    nd = len(shape)
    return pl.BlockSpec(shape, lambda *_: (0,) * nd, pipeline_mode=pl.Buffered(1))


def _split3(x):
    hi = x.astype(BF16)
    r1 = x - hi.astype(F32)
    mid = r1.astype(BF16)
    lo = (r1 - mid.astype(F32)).astype(BF16)
    return hi, mid, lo


def _dot(a, b):
    return jnp.dot(a, b, preferred_element_type=F32)


def _dot_nt(a, b):
    return lax.dot_general(a, b, NT_DIMS, preferred_element_type=F32)


def _exact_dot(ones_mat, x, *, ones_on_left):
    acc = None
    for part in _split3(x):
        term = _dot(ones_mat, part) if ones_on_left else _dot(part, ones_mat)
        acc = term if acc is None else acc + term
    return acc


def _layer_norm(x, g, b):
    mu = jnp.mean(x, axis=-1, keepdims=True)
    xc = x - mu
    var = jnp.mean(xc * xc, axis=-1, keepdims=True)
    return xc * lax.rsqrt(var + LN_EPS) * g + b


def _log_sigmoid(z):
    return jnp.minimum(z, 0.0) - jnp.log1p(jnp.exp(-jnp.abs(z)))


def _inproj_kernel(x_ref, w_ref, b_ref, wf_ref, bf_ref, tri_ref,
                   q_ref, k_ref, v_ref, kb_ref, vb_ref, logf_ref, c_ref, u_ref,
                   sa_ref, sb_ref, carry_ref, *, tiles_per_seq):
    i = pl.program_id(0)
    xb = x_ref[...].astype(BF16)

    def proj(c0, c1):
        return _dot(xb, w_ref[:, c0:c1]) + b_ref[:, c0:c1]

    q = proj(0, FOX_WIDTH)
    q_ref[...] = (q * (ATTN_SCALE * LOG2E)).astype(BF16)
    k = proj(FOX_WIDTH, 2 * FOX_WIDTH)
    kb_ref[...] = k.astype(BF16)
    v = proj(2 * FOX_WIDTH, 3 * FOX_WIDTH)
    vb_ref[...] = v.astype(BF16)
    for h in range(FOX_HEADS):
        head_rows = pl.ds(h, IN_TILE, stride=FOX_HEADS)
        k_ref[head_rows, :] = k[:, h * HEAD_DIM:(h + 1) * HEAD_DIM]
        v_ref[head_rows, :] = v[:, h * HEAD_DIM:(h + 1) * HEAD_DIM]

    logf = _log_sigmoid(_dot(xb, wf_ref[...]) + bf_ref[...])
    logf_ref[...] = logf[:, :FOX_HEADS]

    @pl.when(i % tiles_per_seq == 0)
    def _():
        carry_ref[...] = jnp.zeros_like(carry_ref)

    c = _exact_dot(tri_ref[...], logf, ones_on_left=True) + carry_ref[...]
    c_ref[...] = c[:, :FOX_HEADS]
    carry_ref[...] = c[IN_TILE - 1:IN_TILE, :]

    g0 = 3 * FOX_WIDTH
    glu_a = proj(g0, g0 + CONV_CH)
    glu_b = proj(g0 + CONV_CH, g0 + 2 * CONV_CH)
    u_ref[...] = glu_a * jax.nn.sigmoid(glu_b)
    g1 = g0 + 2 * CONV_CH
    sa_ref[...] = jax.nn.sigmoid(proj(g1, g1 + D_MODEL)).astype(BF16)
    sb_ref[...] = jax.nn.sigmoid(proj(g1 + D_MODEL, g1 + 2 * D_MODEL)).astype(BF16)


def _inproj(x, w_main, b_main, w_f, b_f, tri, *, tiles_per_seq):
    rows = x.shape[0]
    n_main = w_main.shape[1]
    row_spec = lambda w: pl.BlockSpec((IN_TILE, w), lambda i: (i, 0))
    out_shapes = (
        jax.ShapeDtypeStruct((rows, FOX_WIDTH), BF16),
        jax.ShapeDtypeStruct((rows * FOX_HEADS, HEAD_DIM), F32),
        jax.ShapeDtypeStruct((rows * FOX_HEADS, HEAD_DIM), F32),
        jax.ShapeDtypeStruct((rows, FOX_WIDTH), BF16),
        jax.ShapeDtypeStruct((rows, FOX_WIDTH), BF16),
        jax.ShapeDtypeStruct((rows, FOX_HEADS), F32),
        jax.ShapeDtypeStruct((rows, FOX_HEADS), F32),
        jax.ShapeDtypeStruct((rows, CONV_CH), F32),
        jax.ShapeDtypeStruct((rows, D_MODEL), BF16),
        jax.ShapeDtypeStruct((rows, D_MODEL), BF16),
    )
    head_spec = pl.BlockSpec((IN_TILE * FOX_HEADS, HEAD_DIM), lambda i: (i, 0))
    out_specs = (row_spec(FOX_WIDTH), head_spec, head_spec) + (row_spec(FOX_WIDTH),) * 2 + (
        row_spec(FOX_HEADS),) * 2 + (row_spec(CONV_CH), row_spec(D_MODEL), row_spec(D_MODEL))
    return pl.pallas_call(
        functools.partial(_inproj_kernel, tiles_per_seq=tiles_per_seq),
        grid=(rows // IN_TILE,),
        in_specs=[row_spec(D_MODEL), _const_spec((D_MODEL, n_main)), _const_spec((1, n_main)),
                  _const_spec((D_MODEL, LANES)), _const_spec((1, LANES)),
                  _const_spec((IN_TILE, IN_TILE))],
        out_specs=out_specs,
        out_shape=out_shapes,
        scratch_shapes=[pltpu.VMEM((1, LANES), F32)],
        compiler_params=pltpu.CompilerParams(
            dimension_semantics=("arbitrary",), vmem_limit_bytes=VMEM_LIMIT),
        name="inproj",
    )(x, w_main, b_main, w_f, b_f, tri)


def _attn_prompt_kernel(q_ref, k_ref, v_ref, crow_ref, o_ref, qm_scr, m_scr, acc_scr):
    t = ATTN_TILE
    i = pl.program_id(1)
    lane = lax.broadcasted_iota(jnp.int32, (t, LANES), 1)
    row = lax.broadcasted_iota(jnp.int32, (t, t), 0)
    col = lax.broadcasted_iota(jnp.int32, (t, t), 1)
    causal = col <= row
    wide = lambda x: jnp.concatenate([x] * (t // LANES), axis=1)
    pair_lanes = lambda pair: slice(pair * LANES, (pair + 1) * LANES)

    for h in range(FOX_HEADS):
        qp = q_ref[:, pair_lanes(h // 2)]
        in_head = (lane < HEAD_DIM) if h % 2 == 0 else (lane >= HEAD_DIM)
        qm_scr[h] = jnp.where(in_head, qp, jnp.zeros_like(qp))
        m_scr[h] = jnp.full((t, LANES), NEG_INF, F32)
        acc_scr[h] = jnp.zeros((t, 2 * LANES), F32)
    ones = jnp.ones((t, LANES), BF16)

    def step(j, masked):
        r0 = pl.multiple_of(j * t, t)
        ck = crow_ref[0, :, pl.ds(r0, t)] * LOG2E
        for pair in range(FOX_HEADS // 2):
            kj = k_ref[pl.ds(r0, t), pair_lanes(pair)]
            vj = jnp.concatenate([v_ref[pl.ds(r0, t), pair_lanes(pair)], ones], axis=1)
            for h in (2 * pair, 2 * pair + 1):
                s = _dot_nt(qm_scr[h], kj) - ck[h:h + 1, :]
                if masked:
                    s = jnp.where(causal, s, NEG_INF)
                m_prev = m_scr[h]
                m_new = jnp.maximum(m_prev, jnp.max(s, axis=1, keepdims=True))
                alpha = jnp.exp2(m_prev - m_new)
                p = jnp.exp2(s - wide(m_new))
                acc_scr[h] = wide(alpha) * acc_scr[h] + _dot(p.astype(BF16), vj)
                m_scr[h] = m_new

    def body(j, carry):
        step(j, False)
        return carry

    lax.fori_loop(0, i, body, 0)
    step(i, True)
    for pair in range(FOX_HEADS // 2):
        o0, o1 = (acc_scr[h, :, :LANES] / acc_scr[h, :, LANES:] for h in (2 * pair, 2 * pair + 1))
        o_ref[:, pair_lanes(pair)] = jnp.where(lane < HEAD_DIM, o0, o1).astype(BF16)


def _attn_prompt(q, kb, vb, c_row, *, batch, seq):
    t = ATTN_TILE
    nq = seq // t
    return pl.pallas_call(
        _attn_prompt_kernel,
        grid=(batch, nq),
        in_specs=[
            pl.BlockSpec((t, FOX_WIDTH), lambda b, i: (b * nq + i, 0)),
            pl.BlockSpec((seq, FOX_WIDTH), lambda b, i: (b, 0)),
            pl.BlockSpec((seq, FOX_WIDTH), lambda b, i: (b, 0)),
            pl.BlockSpec((1, FOX_HEADS, seq), lambda b, i: (b, 0, 0)),
        ],
        out_specs=pl.BlockSpec((t, FOX_WIDTH), lambda b, i: (b * nq + i, 0)),
        out_shape=jax.ShapeDtypeStruct((batch * seq, FOX_WIDTH), BF16),
        scratch_shapes=[pltpu.VMEM((FOX_HEADS, t, LANES), BF16),
                        pltpu.VMEM((FOX_HEADS, t, LANES), F32),
                        pltpu.VMEM((FOX_HEADS, t, 2 * LANES), F32)],
        compiler_params=pltpu.CompilerParams(
            dimension_semantics=("arbitrary", "arbitrary"), vmem_limit_bytes=VMEM_LIMIT),
        name="attn_prompt",
    )(q, kb, vb, c_row)


def _attn_sample_kernel(q_ref, kn_ref, vn_ref, lf_ref, lft_ref, ck_ref, cv_ref, clft_ref,
                        upper_ref, o_ref, *, t, past):
    rows = FOX_HEADS * t
    lane_head = lax.broadcasted_iota(jnp.int32, (t, FOX_WIDTH), 1) // HEAD_DIM
    q = q_ref[...]
    q_stack = jnp.concatenate(
        [jnp.where(lane_head == h, q, jnp.zeros_like(q)) for h in range(FOX_HEADS)], axis=0)

    clf = clft_ref[0]
    prefix = _exact_dot(upper_ref[...], clf, ones_on_left=False)
    to_end = prefix[:, past - 1:past] - prefix
    ri = lax.broadcasted_iota(jnp.int32, (t, t), 0)
    ci = lax.broadcasted_iota(jnp.int32, (t, t), 1)
    lower = jnp.where(ci <= ri, 1.0, 0.0).astype(BF16)
    upper = jnp.where(ri <= ci, 1.0, 0.0).astype(BF16)
    cn_col = _exact_dot(lower, lf_ref[...], ones_on_left=True)
    cn_row = _exact_dot(upper, lft_ref[0], ones_on_left=False)

    stack = lambda f: jnp.concatenate([f(h) for h in range(FOX_HEADS)], axis=0)
    cn_stack = stack(lambda h: cn_col[:, h:h + 1])
    bias_c = stack(lambda h: jnp.broadcast_to(to_end[h:h + 1, :], (t, past)))
    bias_n = stack(lambda h: jnp.broadcast_to(cn_row[h:h + 1, :], (t, t)))

    kc = ck_ref[0].astype(BF16)
    vc = cv_ref[0].astype(BF16)
    s_c = _dot_nt(q_stack, kc) + (bias_c + cn_stack) * LOG2E
    s_n = _dot_nt(q_stack, kn_ref[...]) + (cn_stack - bias_n) * LOG2E
    tq = lax.broadcasted_iota(jnp.int32, (rows, t), 0) % t
    tk = lax.broadcasted_iota(jnp.int32, (rows, t), 1)
    s_n = jnp.where(tk <= tq, s_n, NEG_INF)
    m = jnp.maximum(jnp.max(s_c, axis=1, keepdims=True), jnp.max(s_n, axis=1, keepdims=True))
    p_c = jnp.exp2(s_c - m)
    p_n = jnp.exp2(s_n - m)
    denom = jnp.sum(p_c, axis=1, keepdims=True) + jnp.sum(p_n, axis=1, keepdims=True)
    o_stack = (_dot(p_c.astype(BF16), vc) + _dot(p_n.astype(BF16), vn_ref[...])) / denom
    out = jnp.zeros((t, FOX_WIDTH), F32)
    for h in range(FOX_HEADS):
        out = out + jnp.where(lane_head == h, o_stack[h * t:(h + 1) * t, :], 0.0)
    o_ref[...] = out.astype(BF16)


def _attn_sample(q, kb, vb, logf, logf_t, cache_k, cache_v, cache_logf_t, upper, *, batch, t, past):
    row_spec = lambda w: pl.BlockSpec((t, w), lambda b: (b, 0))
    return pl.pallas_call(
        functools.partial(_attn_sample_kernel, t=t, past=past),
        grid=(batch,),
        in_specs=[
            row_spec(FOX_WIDTH), row_spec(FOX_WIDTH), row_spec(FOX_WIDTH), row_spec(FOX_HEADS),
            pl.BlockSpec((1, FOX_HEADS, t), lambda b: (b, 0, 0)),
            pl.BlockSpec((1, past, FOX_WIDTH), lambda b: (b, 0, 0)),
            pl.BlockSpec((1, past, FOX_WIDTH), lambda b: (b, 0, 0)),
            pl.BlockSpec((1, FOX_HEADS, past), lambda b: (b, 0, 0)),
            _const_spec((past, past)),
        ],
        out_specs=row_spec(FOX_WIDTH),
        out_shape=jax.ShapeDtypeStruct((batch * t, FOX_WIDTH), BF16),
        compiler_params=pltpu.CompilerParams(
            dimension_semantics=("arbitrary",), vmem_limit_bytes=VMEM_LIMIT),
        name="attn_sample",
    )(q, kb, vb, logf, logf_t, cache_k, cache_v, cache_logf_t, upper)


def _merge_kernel(u_ref, hist_ref, attn_ref, sa_ref, sb_ref, x_ref,
                  cw_ref, cb_ref, cg_ref, cbeta_ref, wb_ref, bb_ref, wa_ref, wout_ref,
                  g1_ref, b1_ref, *rest, ts, tiles_per_seq, n_tiles, zero_first_hist, alpha,
                  aliased):
    if aliased:
        rest = rest[2:]
    mid_ref, mid3_ref, ue_scr, ph_scr, h_scr = rest
    i = pl.program_id(0)

    @pl.when(i >= n_tiles)
    def _():
        mid_ref[...] = jnp.zeros_like(mid_ref)
        mid3_ref[...] = jnp.zeros_like(mid3_ref)

    @pl.when(i < n_tiles)
    def _():
        _merge_tile(u_ref, hist_ref, attn_ref, sa_ref, sb_ref, x_ref, cw_ref, cb_ref, cg_ref,
                    cbeta_ref, wb_ref, bb_ref, wa_ref, wout_ref, g1_ref, b1_ref, mid_ref,
                    mid3_ref, ue_scr, ph_scr, h_scr, ts=ts, alpha=alpha,
                    zero_hist=(i % tiles_per_seq == 0) if zero_first_hist else None)


def _merge_tile(u_ref, hist_ref, attn_ref, sa_ref, sb_ref, x_ref, cw_ref, cb_ref, cg_ref,
                cbeta_ref, wb_ref, bb_ref, wa_ref, wout_ref, g1_ref, b1_ref, mid_ref, mid3_ref,
                ue_scr, ph_scr, h_scr, *, ts, alpha, zero_hist):
    hist = hist_ref[...]
    if zero_hist is not None:
        hist = jnp.where(zero_hist, 0.0, hist)
    ue_scr[0:HIST, :] = hist
    ue_scr[HIST:HIST + ts, :] = u_ref[...]
    lead = HIST - (CONV_WIDTH - 1)
    for r in range(SUBLANES):
        span = ts + (CONV_WIDTH - 1 - r) // SUBLANES * SUBLANES
        ph_scr[r, 0:span, :] = ue_scr[lead + r:lead + r + span, :]

    def conv_rows(rt, carry):
        base = pl.multiple_of(rt * CONV_SUB, CONV_SUB)
        acc = jnp.zeros((CONV_SUB, CONV_CH), F32)
        for j in range(CONV_WIDTH):
            r, a = j % SUBLANES, j // SUBLANES
            acc = acc + cw_ref[j:j + 1, :] * ph_scr[r, pl.ds(base + a * SUBLANES, CONV_SUB), :]
        h_scr[pl.ds(base, CONV_SUB), :] = acc
        return carry

    lax.fori_loop(0, ts // CONV_SUB, conv_rows, 0)
    h = _layer_norm(h_scr[...] + cb_ref[...], cg_ref[...], cbeta_ref[...])
    h = h * jax.nn.sigmoid(h)
    conv_out = _dot(h.astype(BF16), wb_ref[...]) + bb_ref[...]
    attn_out = _dot(attn_ref[...], wa_ref[...])
    m = sa_ref[...].astype(F32) * attn_out + sb_ref[...].astype(F32) * conv_out
    z = alpha * x_ref[...] + _dot(m.astype(BF16), wout_ref[...])
    mid = _layer_norm(z, g1_ref[...], b1_ref[...])
    mid_ref[...] = mid
    for j in range(ROW_TILES):
        mid3_ref[pl.ds(j, ts, stride=ROW_TILES), :] = mid[:, j * LANES:(j + 1) * LANES]


def _merge(u, hist, attn, sa, sb, x, conv_p, wa, wout, g1, b1, *, n_seq, seq, ts, hist_from_u,
           alpha, total_rows, row_offset, prev=None):
    nt = seq // ts
    n_tiles = n_seq * nt
    off = row_offset // ts
    grid_tiles = n_tiles if prev is not None else total_rows // ts
    src = lambda i: jnp.minimum(i, n_tiles - 1)
    row_spec = lambda w: pl.BlockSpec((ts, w), lambda i: (src(i), 0))
    if hist_from_u:
        per = ts // HIST
        hist_spec = pl.BlockSpec((HIST, CONV_CH), lambda i: (jnp.maximum(src(i) * per - 1, 0), 0))
    else:
        hist_spec = pl.BlockSpec((HIST, CONV_CH), lambda i: (src(i) // nt, 0))
    cw, cb, cg, cbeta, wb, bb = conv_p
    consts = [cw, cb, cg, cbeta, wb, bb, wa, wout, g1, b1]
    in_specs = [row_spec(CONV_CH), hist_spec, row_spec(FOX_WIDTH), row_spec(D_MODEL),
                row_spec(D_MODEL), row_spec(D_MODEL)] + [_const_spec(c.shape) for c in consts]
    args = [u, hist, attn, sa, sb, x] + consts
    aliases = {}
    if prev is not None:
        in_specs += [pl.BlockSpec(memory_space=pl.ANY)] * 2
        aliases = {len(args): 0, len(args) + 1: 1}
        args += list(prev)
    span = ts + (CONV_WIDTH - 1) // SUBLANES * SUBLANES
    return pl.pallas_call(
        functools.partial(_merge_kernel, ts=ts, tiles_per_seq=nt, n_tiles=n_tiles,
                          zero_first_hist=hist_from_u, alpha=alpha, aliased=prev is not None),
        grid=(grid_tiles,),
        in_specs=in_specs,
        out_specs=(pl.BlockSpec((ts, D_MODEL), lambda i: (off + i, 0)),
                   pl.BlockSpec((ts * ROW_TILES, LANES), lambda i: (off + i, 0))),
        out_shape=(jax.ShapeDtypeStruct((total_rows, D_MODEL), F32),
                   jax.ShapeDtypeStruct((total_rows * ROW_TILES, LANES), F32)),
        scratch_shapes=[pltpu.VMEM((HIST + ts, CONV_CH), F32),
                        pltpu.VMEM((SUBLANES, span, CONV_CH), F32),
                        pltpu.VMEM((ts, CONV_CH), F32)],
        input_output_aliases=aliases,
        compiler_params=pltpu.CompilerParams(
            dimension_semantics=("arbitrary",), vmem_limit_bytes=VMEM_LIMIT),
        name="merge",
    )(*args)


def _router_kernel(mid_ref, wr_hi_ref, wr_lo_ref, br_ref, before_ref,
                   eidx_ref, gate_ref, rank_ref, cnt_ref, carry_scr):
    tr = ROUTER_TILE
    i = pl.program_id(0)

    @pl.when(i == 0)
    def _():
        carry_scr[...] = jnp.zeros_like(carry_scr)

    x = mid_ref[...]
    x_hi = x.astype(BF16)
    x_lo = (x - x_hi.astype(F32)).astype(BF16)
    wr_hi = wr_hi_ref[...]
    logits = _dot_nt(wr_hi, x_hi) + _dot_nt(wr_hi, x_lo) + _dot_nt(wr_lo_ref[...], x_hi)
    scores = jax.nn.sigmoid(logits)
    sel = scores + br_ref[...]

    sel3 = sel.reshape(N_GROUPS, GROUP_SIZE, tr)
    in_group = lax.broadcasted_iota(jnp.int32, sel3.shape, 1)
    m1 = jnp.max(sel3, axis=1, keepdims=True)
    first = jnp.min(jnp.where(sel3 == m1, in_group, GROUP_SIZE), axis=1, keepdims=True)
    m2 = jnp.max(jnp.where(in_group == first, NEG_INF, sel3), axis=1, keepdims=True)
    gs = m1 + m2
    gi = lax.broadcasted_iota(jnp.int32, gs.shape, 0)
    beaten = jnp.zeros(gs.shape, F32)
    for g in range(N_GROUPS):
        other = gs[g:g + 1]
        wins = (other > gs) | ((other == gs) & (g < gi))
        beaten = beaten + jnp.where(wins, 1.0, 0.0)
    drop = jnp.where(beaten < TOPK_GROUPS, 0.0, NEG_INF)
    cur = (sel3 + drop).reshape(N_EXPERTS, tr)

    ei = lax.broadcasted_iota(jnp.int32, (N_EXPERTS, tr), 0)
    idxs, vals = [], []
    picked = jnp.zeros((N_EXPERTS, tr), F32)
    for _ in range(TOP_K):
        m = jnp.max(cur, axis=0, keepdims=True)
        idx = jnp.min(jnp.where(cur == m, ei, N_EXPERTS), axis=0, keepdims=True)
        hit = ei == idx
        vals.append(jnp.sum(jnp.where(hit, scores, 0.0), axis=0, keepdims=True))
        idxs.append(idx)
        picked = picked + jnp.where(hit, 1.0, 0.0)
        cur = jnp.where(hit, NEG_INF, cur)

    total = vals[0]
    for v in vals[1:]:
        total = total + v
    for k in range(TOP_K):
        gate_ref[k:k + 1, :] = vals[k] / total * ROUTED_SCALE
        eidx_ref[k:k + 1, :] = idxs[k]

    ahead = _dot(picked.astype(BF16), before_ref[...]) + carry_scr[:, 0:1]
    for k in range(TOP_K):
        rank = jnp.sum(jnp.where(ei == idxs[k], ahead, 0.0), axis=0, keepdims=True)
        rank_ref[k:k + 1, :] = rank.astype(jnp.int32)
    carry_scr[...] = carry_scr[...] + jnp.sum(picked, axis=1, keepdims=True)
    cnt_ref[...] = carry_scr[...]


def _router(mid, wr_hi, wr_lo, br, before):
    tr = ROUTER_TILE
    tokens = mid.shape[0]
    tok_spec = pl.BlockSpec((TOP_K, tr), lambda i: (0, i))
    return pl.pallas_call(
        _router_kernel,
        grid=(tokens // tr,),
        in_specs=[pl.BlockSpec((tr, D_MODEL), lambda i: (i, 0)),
                  _const_spec((N_EXPERTS, D_MODEL)), _const_spec((N_EXPERTS, D_MODEL)),
                  _const_spec((N_EXPERTS, 1)), _const_spec((tr, tr))],
        out_specs=(tok_spec, tok_spec, tok_spec,
                   pl.BlockSpec((N_EXPERTS, LANES), lambda i: (0, 0))),
        out_shape=(jax.ShapeDtypeStruct((TOP_K, tokens), jnp.int32),
                   jax.ShapeDtypeStruct((TOP_K, tokens), F32),
                   jax.ShapeDtypeStruct((TOP_K, tokens), jnp.int32),
                   jax.ShapeDtypeStruct((N_EXPERTS, LANES), F32)),
        scratch_shapes=[pltpu.VMEM((N_EXPERTS, LANES), F32)],
        compiler_params=pltpu.CompilerParams(
            dimension_semantics=("arbitrary",), vmem_limit_bytes=VMEM_LIMIT),
        name="router",
    )(mid, wr_hi, wr_lo, br, before)


def _dest_kernel(eidx_ref, rank_ref, starts_ref, dest_ref):
    tt = MOVE_TILE
    tokens = eidx_ref.shape[1]
    ei = lax.broadcasted_iota(jnp.int32, (N_EXPERTS, tokens), 0)
    starts = starts_ref[...]
    for k in range(TOP_K):
        hit = ei == eidx_ref[k:k + 1, :]
        start = jnp.sum(jnp.where(hit, starts, 0.0), axis=0, keepdims=True)
        dest = start.astype(jnp.int32) + rank_ref[k:k + 1, :]
        for c in range(tokens // tt):
            dest_ref[c, k:k + 1, :] = dest[:, c * tt:(c + 1) * tt]


def _dest(eidx, rank, starts):
    tt = MOVE_TILE
    tokens = eidx.shape[1]
    step = DEST_TILE
    tok_spec = pl.BlockSpec((TOP_K, step), lambda i: (0, i))
    return pl.pallas_call(
        _dest_kernel,
        grid=(tokens // step,),
        in_specs=[tok_spec, tok_spec, _const_spec((N_EXPERTS, 1))],
        out_specs=pl.BlockSpec((step // tt, TOP_K, tt), lambda i: (i, 0, 0)),
        out_shape=jax.ShapeDtypeStruct((tokens // tt, TOP_K, tt), jnp.int32),
        compiler_params=pltpu.CompilerParams(dimension_semantics=("arbitrary",)),
        name="dest",
    )(eidx, rank, starts)


def _row_slice(ref, row):
    return ref.at[pl.ds(pl.multiple_of(row * ROW_TILES, ROW_TILES), ROW_TILES), :]


def _dispatch_kernel(dest_ref, rows_ref, xs_ref, zeros_scr, sem, *, n_rows):
    tt = MOVE_TILE

    @pl.when(pl.program_id(0) == 0)
    def _():
        zeros_scr[...] = jnp.zeros_like(zeros_scr)
        tail = pltpu.make_async_copy(
            zeros_scr, xs_ref.at[pl.ds(n_rows * ROW_TILES, EXPERT_BLOCK * ROW_TILES), :], sem)
        tail.start()
        tail.wait()

    def issue(t, carry):
        src = _row_slice(rows_ref, t)
        for k in range(TOP_K):
            pltpu.make_async_copy(src, _row_slice(xs_ref, dest_ref[k * tt + t]), sem).start(
                priority=k % 2)
        return carry

    lax.fori_loop(0, tt, issue, 0)
    for _ in range(TOP_K):
        pltpu.make_async_copy(rows_ref, xs_ref.at[pl.ds(0, tt * ROW_TILES), :], sem).wait()


def _dispatch(dest_flat, rows3):
    tt = MOVE_TILE
    tokens = rows3.shape[0] // ROW_TILES
    n_rows = tokens * TOP_K
    return pl.pallas_call(
        functools.partial(_dispatch_kernel, n_rows=n_rows),
        grid=(tokens // tt,),
        in_specs=[pl.BlockSpec((tt * TOP_K,), lambda i: (i,), memory_space=pltpu.SMEM),
                  pl.BlockSpec((tt * ROW_TILES, LANES), lambda i: (i, 0))],
        out_specs=pl.BlockSpec(memory_space=pl.ANY),
        out_shape=jax.ShapeDtypeStruct(((n_rows + EXPERT_BLOCK) * ROW_TILES, LANES), F32),
        scratch_shapes=[pltpu.VMEM((EXPERT_BLOCK * ROW_TILES, LANES), F32),
                        pltpu.SemaphoreType.DMA(())],
        compiler_params=pltpu.CompilerParams(dimension_semantics=("arbitrary",)),
        name="dispatch",
    )(dest_flat, rows3)


_PIECES = tuple(EXPERT_BLOCK >> s for s in range(EXPERT_BLOCK.bit_length()))
ROWS_AHEAD = 3
ROW_SLOTS = ROWS_AHEAD + 1


def _expert_kernel(cexp_ref, crow_ref, cn_ref, cnew_ref, cnext_ref, nused_ref,
                   xs_hbm, wg_hbm, wu_hbm, wd_hbm, ys_hbm,
                   xbuf, ybuf, wg_buf, wu_buf, wd_buf, wgb, wub, wdb, xsem, ysem, wsem):
    bm = EXPERT_BLOCK
    i = pl.program_id(0)
    n_used = nused_ref[0]
    cur = i % 2

    def weight_copies(e, s):
        return [pltpu.make_async_copy(hbm.at[e], buf.at[s], wsem.at[s])
                for hbm, buf in ((wg_hbm, wg_buf), (wu_hbm, wu_buf), (wd_hbm, wd_buf))]

    def rows_in(j, s):
        first = pl.multiple_of(crow_ref[j] * ROW_TILES, ROW_TILES)
        return pltpu.make_async_copy(xs_hbm.at[pl.ds(first, bm * ROW_TILES), :], xbuf.at[s],
                                     xsem.at[s])

    def rows_out(j, s, act):
        n, row0 = cn_ref[j], crow_ref[j]
        done = jnp.int32(0)
        for p in _PIECES:
            has = (n & p) != 0

            @pl.when(has)
            def _(done=done, p=p):
                src = pl.multiple_of(done * ROW_TILES, ROW_TILES)
                dst = pl.multiple_of((row0 + done) * ROW_TILES, ROW_TILES)
                act(pltpu.make_async_copy(ybuf.at[s, pl.ds(src, p * ROW_TILES), :],
                                          ys_hbm.at[pl.ds(dst, p * ROW_TILES), :], ysem.at[s]))

            done = done + jnp.where(has, p, 0)

    start = lambda c: c.start()
    wait = lambda c: c.wait()

    @pl.when(i == 0)
    def _():
        for d in range(ROWS_AHEAD):
            @pl.when(d < n_used)
            def _(d=d):
                rows_in(d, d).start()
        for c in weight_copies(cexp_ref[0], 0):
            c.start()

    @pl.when(i < n_used)
    def _():
        slot = cnew_ref[i]

        @pl.when(slot >= 0)
        def _():
            for c in weight_copies(cexp_ref[i], slot):
                c.wait()
            nxt = cnext_ref[i]

            @pl.when(nxt >= 0)
            def _():
                for c in weight_copies(nxt, 1 - slot):
                    c.start()

            wgb[...] = wg_buf[slot].astype(BF16)
            wub[...] = wu_buf[slot].astype(BF16)
            wdb[...] = wd_buf[slot].astype(BF16)

        @pl.when(i + ROWS_AHEAD < n_used)
        def _():
            rows_in(i + ROWS_AHEAD, (i + ROWS_AHEAD) % ROW_SLOTS).start()

        xcur = i % ROW_SLOTS
        rows_in(i, xcur).wait()

        @pl.when(i >= 2)
        def _():
            rows_out(i - 2, cur, wait)

        x = jnp.concatenate(
            [xbuf[xcur, pl.ds(j, bm, stride=ROW_TILES), :] for j in range(ROW_TILES)], axis=1)
        xb = x.astype(BF16)
        g = _dot(xb, wgb[...])
        u = _dot(xb, wub[...])
        h = (g * jax.nn.sigmoid(g) * u).astype(BF16)
        y = _dot(h, wdb[...])
        for j in range(ROW_TILES):
            ybuf[cur, pl.ds(j, bm, stride=ROW_TILES), :] = y[:, j * LANES:(j + 1) * LANES]
        rows_out(i, cur, start)

        @pl.when(i == n_used - 1)
        def _():
            rows_out(i, cur, wait)

            @pl.when(i >= 1)
            def _():
                rows_out(i - 1, 1 - cur, wait)


def _experts(cexp, crow, cn, cnew, cnext, nused, xs, wg, wu, wd, n_rows):
    bm = EXPERT_BLOCK
    any_spec = pl.BlockSpec(memory_space=pl.ANY)
    grid_spec = pltpu.PrefetchScalarGridSpec(
        num_scalar_prefetch=6,
        grid=(cexp.shape[0],),
        in_specs=[any_spec] * 4,
        out_specs=any_spec,
        scratch_shapes=[pltpu.VMEM((ROW_SLOTS, bm * ROW_TILES, LANES), F32),
                        pltpu.VMEM((2, bm * ROW_TILES, LANES), F32),
                        pltpu.VMEM((2, D_MODEL, D_EXPERT), F32),
                        pltpu.VMEM((2, D_MODEL, D_EXPERT), F32),
                        pltpu.VMEM((2, D_EXPERT, D_MODEL), F32),
                        pltpu.VMEM((D_MODEL, D_EXPERT), BF16),
                        pltpu.VMEM((D_MODEL, D_EXPERT), BF16),
                        pltpu.VMEM((D_EXPERT, D_MODEL), BF16),
                        pltpu.SemaphoreType.DMA((ROW_SLOTS,)),
                        pltpu.SemaphoreType.DMA((2,)),
                        pltpu.SemaphoreType.DMA((2,))],
    )
    return pl.pallas_call(
        _expert_kernel,
        grid_spec=grid_spec,
        out_shape=jax.ShapeDtypeStruct((n_rows * ROW_TILES, LANES), F32),
        compiler_params=pltpu.CompilerParams(
            dimension_semantics=("arbitrary",), vmem_limit_bytes=VMEM_LIMIT),
        name="experts",
    )(cexp, crow, cn, cnew, cnext, nused, xs, wg, wu, wd)


GATHER_AHEAD = 2
GATHER_SLOTS = GATHER_AHEAD + 1


def _combine_kernel(dest0_ref, dest1_ref, dest2_ref, ys_ref, gate_ref, mid_ref, wsg_ref,
                    wsu_ref, wsd_ref, g2_ref, b2_ref, out_a_ref, out_b_ref, buf, gate_scr,
                    routed_scr, sem, *, alpha, steps_a):
    tt = MOVE_TILE
    i = pl.program_id(0)
    n = pl.num_programs(0)
    cur = i % GATHER_SLOTS
    ahead = (i + GATHER_AHEAD) % GATHER_SLOTS

    def row_copy(slots_ref, b, t, k):
        return pltpu.make_async_copy(_row_slice(ys_ref, slots_ref[k * tt + t]),
                                     _row_slice(buf.at[b], k * tt + t), sem.at[b])

    def start_rows(slots_ref, b):
        def issue(t, carry):
            for k in range(TOP_K):
                row_copy(slots_ref, b, t, k).start(priority=k % 2)
            return carry
        lax.fori_loop(0, tt, issue, 0)

    def wait_rows(b):
        pltpu.make_async_copy(ys_ref.at[pl.ds(0, TOP_K * tt * ROW_TILES), :], buf.at[b],
                              sem.at[b]).wait()

    @pl.when(i == 0)
    def _():
        start_rows(dest0_ref, 0)
        start_rows(dest1_ref, 1)

    mid = mid_ref[...]
    xb = mid.astype(BF16)
    g = _dot(xb, wsg_ref[...])
    h = (g * jax.nn.sigmoid(g) * _dot(xb, wsu_ref[...])).astype(BF16)
    acc = alpha * mid + _dot(h, wsd_ref[...])

    wait_rows(cur)
    gates = gate_ref[...]
    for k in range(TOP_K):
        gate_scr[k] = jnp.broadcast_to(gates[:, k:k + 1], (tt, LANES))
    per_block = tt // (ROW_TILES * TOP_K)
    for j in range(ROW_TILES):
        part = jnp.zeros((tt, LANES), F32)
        for k in range(TOP_K):
            part = part + gate_scr[k] * buf[cur, pl.ds(k * tt * ROW_TILES + j, tt,
                                                        stride=ROW_TILES), :]
            t0 = (j * TOP_K + k) * per_block
            for t in range(t0, t0 + per_block):
                for kk in range(TOP_K):
                    row_copy(dest2_ref, ahead, t, kk).start(priority=kk % 2)
        routed_scr[:, j * LANES:(j + 1) * LANES] = part
    out = _layer_norm(acc + routed_scr[...], g2_ref[...], b2_ref[...])

    @pl.when(i == n - 1)
    def _():
        for d in range(1, GATHER_SLOTS):
            wait_rows((i + d) % GATHER_SLOTS)

    @pl.when(i < steps_a)
    def _():
        out_a_ref[...] = out

    @pl.when(i >= steps_a)
    def _():
        out_b_ref[...] = out


def _combine(dest_flat, ys, gate, mid, wsg, wsu, wsd, g2, b2, *, alpha, rows_a):
    tt = MOVE_TILE
    tokens = mid.shape[0]
    n = tokens // tt
    steps_a = rows_a // tt
    consts = [wsg, wsu, wsd, g2, b2]
    slots_spec = lambda f: pl.BlockSpec((tt * TOP_K,), f, memory_space=pltpu.SMEM)
    return pl.pallas_call(
        functools.partial(_combine_kernel, alpha=alpha, steps_a=steps_a),
        grid=(n,),
        in_specs=[slots_spec(lambda i: (i,)),
                  slots_spec(lambda i: (jnp.minimum(i + 1, n - 1),)),
                  slots_spec(lambda i: (jnp.minimum(i + GATHER_AHEAD, n - 1),)),
                  pl.BlockSpec(memory_space=pl.ANY),
                  pl.BlockSpec((tt, TOP_K), lambda i: (i, 0)),
                  pl.BlockSpec((tt, D_MODEL), lambda i: (i, 0))]
                 + [_const_spec(c.shape) for c in consts],
        out_specs=(pl.BlockSpec((tt, D_MODEL), lambda i: (jnp.minimum(i, steps_a - 1), 0)),
                   pl.BlockSpec((tt, D_MODEL), lambda i: (jnp.maximum(i - steps_a, 0), 0))),
        out_shape=(jax.ShapeDtypeStruct((rows_a, D_MODEL), F32),
                   jax.ShapeDtypeStruct((tokens - rows_a, D_MODEL), F32)),
        scratch_shapes=[pltpu.VMEM((GATHER_SLOTS, TOP_K * tt * ROW_TILES, LANES), F32),
                        pltpu.VMEM((TOP_K, tt, LANES), F32),
                        pltpu.VMEM((tt, D_MODEL), F32),
                        pltpu.SemaphoreType.DMA((GATHER_SLOTS,))],
        compiler_params=pltpu.CompilerParams(
            dimension_semantics=("arbitrary",), vmem_limit_bytes=VMEM_LIMIT),
        name="combine",
    )(dest_flat, dest_flat, dest_flat, ys, gate, mid, *consts)


def _tri(n, *, lower):
    r = lax.broadcasted_iota(jnp.int32, (n, n), 0)
    c = lax.broadcasted_iota(jnp.int32, (n, n), 1)
    return jnp.where((c <= r) if lower else (r <= c), 1.0, 0.0).astype(BF16)


def _moe(mid, rows3, w_router, b_router, w_e_gate, w_e_up, w_e_down, wsg, wsu, wsd, g2, b2, alpha,
         rows_a):
    tokens = mid.shape[0]
    bm = EXPERT_BLOCK
    wr_t = w_router.T
    wr_hi = wr_t.astype(BF16)
    wr_lo = (wr_t - wr_hi.astype(F32)).astype(BF16)
    r = lax.broadcasted_iota(jnp.int32, (ROUTER_TILE, ROUTER_TILE), 0)
    c = lax.broadcasted_iota(jnp.int32, (ROUTER_TILE, ROUTER_TILE), 1)
    before = jnp.where(r < c, 1.0, 0.0).astype(BF16)
    eidx, gate, rank, cnt = _router(mid, wr_hi, wr_lo, b_router.reshape(N_EXPERTS, 1), before)

    i32 = lambda a: a.astype(jnp.int32)
    experts = jnp.arange(N_EXPERTS, dtype=jnp.int32)
    counts = i32(cnt[:, 0])
    starts = jnp.cumsum(counts) - counts
    n_ch = (counts + bm - 1) // bm
    ch_ends = jnp.cumsum(n_ch)
    ch_starts = ch_ends - n_ch
    n_used = ch_ends[-1]
    max_chunks = tokens * TOP_K // bm + N_EXPERTS
    ci = jnp.arange(max_chunks, dtype=jnp.int32)
    cc = jnp.minimum(ci, n_used - 1)
    cexp = jnp.minimum(jnp.sum(i32(ch_ends[None, :] <= cc[:, None]), axis=1), N_EXPERTS - 1)
    hot = cexp[:, None] == experts[None, :]
    lookup = lambda table: jnp.sum(jnp.where(hot, table[None, :], 0), axis=1)
    k_in_expert = cc - lookup(ch_starts)
    crow = lookup(starts) + k_in_expert * bm
    used = ci < n_used
    cn = jnp.where(used, jnp.clip(lookup(counts) - k_in_expert * bm, 0, bm), 0)
    is_first = used & (k_in_expert == 0)
    cnew = jnp.where(is_first, (jnp.cumsum(i32(is_first)) - 1) % 2, -1)
    later = (experts[None, :] > experts[:, None]) & (n_ch[None, :] > 0)
    next_expert = jnp.min(jnp.where(later, experts[None, :], N_EXPERTS), axis=1)
    cnext = lookup(jnp.where(next_expert < N_EXPERTS, next_expert, -1))
    cnext = jnp.where(is_first, cnext, -1)
    dest = _dest(eidx, rank, starts.astype(F32).reshape(N_EXPERTS, 1)).reshape(tokens * TOP_K)

    xs = _dispatch(dest, rows3)
    ys = _experts(cexp, i32(crow), i32(cn), i32(cnew), i32(cnext), i32(n_used).reshape(1), xs,
                  w_e_gate, w_e_up, w_e_down, tokens * TOP_K)
    return _combine(dest, ys, gate.T, mid, wsg, wsu, wsd, g2, b2, alpha=alpha, rows_a=rows_a)


def kernel(x_prompt, x_sample, cache_k, cache_v, cache_logf, state_conv, w_in, b_in, conv_w,
           conv_b, conv_ln_g, conv_ln_b, w_a, w_b, b_b, w_out, ln1_g, ln1_b, w_router, b_router,
           w_e_gate, w_e_up, w_e_down, w_s_gate, w_s_up, w_s_down, ln2_g, ln2_b):
    depth = w_in.shape[0]
    alpha = float((2 * depth) ** 0.25)
    batch, seq, _ = x_prompt.shape
    dbatch, dseq, _ = x_sample.shape
    past = cache_k.shape[2]
    rows_p, rows_s = batch * seq, dbatch * dseq
    total = rows_p + rows_s
    assert seq % IN_TILE == 0 and rows_s % IN_TILE == 0 and seq % MERGE_TILE == 0
    assert total % ROUTER_TILE == 0 and dseq == HIST and rows_p % dseq == 0
    assert total % DEST_TILE == 0 and (total * TOP_K) % EXPERT_BLOCK == 0

    hp = x_prompt.reshape(rows_p, D_MODEL)
    hs = x_sample.reshape(rows_s, D_MODEL)
    tri_in = _tri(IN_TILE, lower=True)
    upper_past = _tri(past, lower=False)
    row2 = lambda a: a.reshape(1, -1)
    outs = {n: [] for n in ("kp", "vp", "fp", "cp", "ks", "vs", "fs", "cs")}

    for l in range(depth):
        w = w_in[l]
        b = b_in[l]
        main_cols = lambda a: jnp.concatenate([a[..., :OFF_F], a[..., OFF_GLU:]], axis=-1)
        w_main = main_cols(w).astype(BF16)
        b_main = row2(main_cols(b))
        w_f = jnp.pad(w[:, OFF_F:OFF_GLU], ((0, 0), (0, LANES - FOX_HEADS))).astype(BF16)
        b_f = row2(jnp.pad(b[OFF_F:OFF_GLU], (0, LANES - FOX_HEADS)))
        cw = jnp.pad(conv_w[l], ((0, 1), (0, 0)))
        conv_p = (cw, row2(conv_b[l]), row2(conv_ln_g[l]), row2(conv_ln_b[l]),
                  w_b[l].astype(BF16), row2(b_b[l]))
        wa, wout = w_a[l].astype(BF16), w_out[l].astype(BF16)
        g1, b1 = row2(ln1_g[l]), row2(ln1_b[l])

        q, k, v, kb, vb, logf, c, u, sa, sb = _inproj(
            hp, w_main, b_main, w_f, b_f, tri_in, tiles_per_seq=seq // IN_TILE)
        c_row = c.reshape(batch, seq, FOX_HEADS).transpose(0, 2, 1)
        attn = _attn_prompt(q, kb, vb, c_row, batch=batch, seq=seq)
        mid, rows3 = _merge(u, u, attn, sa, sb, hp, conv_p, wa, wout, g1, b1,
                            n_seq=batch, seq=seq, ts=MERGE_TILE, hist_from_u=True, alpha=alpha,
                            total_rows=total, row_offset=0)
        outs["kp"].append(k.reshape(batch, seq, FOX_HEADS, HEAD_DIM))
        outs["vp"].append(v.reshape(batch, seq, FOX_HEADS, HEAD_DIM))
        outs["fp"].append(logf.reshape(batch, seq, FOX_HEADS))
        outs["cp"].append(u.reshape(batch, seq, CONV_CH)[:, seq - (CONV_WIDTH - 1):])

        q, k, v, kb, vb, logf, _, u, sa, sb = _inproj(
            hs, w_main, b_main, w_f, b_f, tri_in, tiles_per_seq=1)
        logf_t = logf.reshape(dbatch, dseq, FOX_HEADS).transpose(0, 2, 1)
        attn = _attn_sample(
            q, kb, vb, logf, logf_t, cache_k[l].reshape(dbatch, past, FOX_WIDTH),
            cache_v[l].reshape(dbatch, past, FOX_WIDTH), cache_logf[l].transpose(0, 2, 1),
            upper_past, batch=dbatch, t=dseq, past=past)
        hist = jnp.pad(state_conv[l], ((0, 0), (HIST - (CONV_WIDTH - 1), 0), (0, 0)))
        mid, rows3 = _merge(u, hist.reshape(dbatch * HIST, CONV_CH), attn, sa, sb, hs, conv_p,
                            wa, wout, g1, b1, n_seq=dbatch, seq=dseq, ts=dseq, hist_from_u=False,
                            alpha=alpha, total_rows=total, row_offset=rows_p, prev=(mid, rows3))
        outs["ks"].append(k.reshape(dbatch, dseq, FOX_HEADS, HEAD_DIM))
        outs["vs"].append(v.reshape(dbatch, dseq, FOX_HEADS, HEAD_DIM))
        outs["fs"].append(logf.reshape(dbatch, dseq, FOX_HEADS))
        u3 = u.reshape(dbatch, dseq, CONV_CH)
        u_ext = jnp.concatenate([state_conv[l], u3], axis=1)
        outs["cs"].append(u_ext[:, -(CONV_WIDTH - 1):])

        hp, hs = _moe(mid, rows3, w_router[l], b_router[l], w_e_gate[l], w_e_up[l], w_e_down[l],
                      w_s_gate[l].astype(BF16), w_s_up[l].astype(BF16), w_s_down[l].astype(BF16),
                      row2(ln2_g[l]), row2(ln2_b[l]), alpha, rows_p)

    st = lambda n: jnp.stack(outs[n])
    return (hp.reshape(batch, seq, D_MODEL), hs.reshape(dbatch, dseq, D_MODEL),
            st("kp"), st("vp"), st("fp"), st("cp"), st("ks"), st("vs"), st("fs"), st("cs"))
```

```python
import functools

import jax
import jax.numpy as jnp
from jax import lax
from jax.experimental import pallas as pl
from jax.experimental.pallas import tpu as pltpu

D_MODEL = 1024
FOX_HEADS = 8
HEAD_DIM = 64
FOX_WIDTH = FOX_HEADS * HEAD_DIM
ATTN_SCALE = HEAD_DIM ** -0.5
LOG2E = 1.4426950408889634
CONV_CH = D_MODEL // 2
CONV_WIDTH = 31
N_EXPERTS = 256
TOP_K = 8
N_GROUPS = 8
GROUP_SIZE = N_EXPERTS // N_GROUPS
TOPK_GROUPS = 4
D_EXPERT = D_MODEL // 4
ROUTED_SCALE = 2.5
LN_EPS = 1e-5

OFF_K = FOX_WIDTH
OFF_V = 2 * FOX_WIDTH
OFF_F = 3 * FOX_WIDTH
OFF_GLU = OFF_F + FOX_HEADS
OFF_GA = OFF_GLU + 2 * CONV_CH
OFF_GB = OFF_GA + D_MODEL

LANES = 128
SUBLANES = 8
ROW_TILES = D_MODEL // LANES
VMEM_LIMIT = 56 * 1024 * 1024

IN_TILE = 512
ATTN_TILE = 256
MERGE_TILE = 256
CONV_SUB = 32
HIST = 32
ROUTER_TILE = 512
MOVE_TILE = 128
DEST_TILE = 1024
EXPERT_BLOCK = 256

F32 = jnp.float32
BF16 = jnp.bfloat16
NEG_INF = float("-inf")
NT_DIMS = (((1,), (1,)), ((), ()))


def _const_spec(shape):
    nd = len(shape)
    return pl.BlockSpec(shape, lambda *_: (0,) * nd, pipeline_mode=pl.Buffered(1))


def _split3(x):
    hi = x.astype(BF16)
    r1 = x - hi.astype(F32)
    mid = r1.astype(BF16)
    lo = (r1 - mid.astype(F32)).astype(BF16)
    return hi, mid, lo


def _dot(a, b):
    return jnp.dot(a, b, preferred_element_type=F32)


def _dot_nt(a, b):
    return lax.dot_general(a, b, NT_DIMS, preferred_element_type=F32)


def _exact_dot(ones_mat, x, *, ones_on_left):
    acc = None
    for part in _split3(x):
        term = _dot(ones_mat, part) if ones_on_left else _dot(part, ones_mat)
        acc = term if acc is None else acc + term
    return acc


def _layer_norm(x, g, b):
    mu = jnp.mean(x, axis=-1, keepdims=True)
    xc = x - mu
    var = jnp.mean(xc * xc, axis=-1, keepdims=True)
    return xc * lax.rsqrt(var + LN_EPS) * g + b


def _log_sigmoid(z):
    return jnp.minimum(z, 0.0) - jnp.log1p(jnp.exp(-jnp.abs(z)))


def _inproj_kernel(x_ref, w_ref, b_ref, wf_ref, bf_ref, tri_ref,
                   q_ref, k_ref, v_ref, kb_ref, vb_ref, logf_ref, c_ref, u_ref,
                   sa_ref, sb_ref, carry_ref, *, tiles_per_seq):
    i = pl.program_id(0)
    xb = x_ref[...].astype(BF16)

    def proj(c0, c1):
        return _dot(xb, w_ref[:, c0:c1]) + b_ref[:, c0:c1]

    q = proj(0, FOX_WIDTH)
    q_ref[...] = (q * (ATTN_SCALE * LOG2E)).astype(BF16)
    k = proj(FOX_WIDTH, 2 * FOX_WIDTH)
    kb_ref[...] = k.astype(BF16)
    v = proj(2 * FOX_WIDTH, 3 * FOX_WIDTH)
    vb_ref[...] = v.astype(BF16)
    for h in range(FOX_HEADS):
        head_rows = pl.ds(h, IN_TILE, stride=FOX_HEADS)
        k_ref[head_rows, :] = k[:, h * HEAD_DIM:(h + 1) * HEAD_DIM]
        v_ref[head_rows, :] = v[:, h * HEAD_DIM:(h + 1) * HEAD_DIM]

    logf = _log_sigmoid(_dot(xb, wf_ref[...]) + bf_ref[...])
    logf_ref[...] = logf[:, :FOX_HEADS]

    @pl.when(i % tiles_per_seq == 0)
    def _():
        carry_ref[...] = jnp.zeros_like(carry_ref)

    c = _exact_dot(tri_ref[...], logf, ones_on_left=True) + carry_ref[...]
    c_ref[...] = c[:, :FOX_HEADS]
    carry_ref[...] = c[IN_TILE - 1:IN_TILE, :]

    g0 = 3 * FOX_WIDTH
    glu_a = proj(g0, g0 + CONV_CH)
    glu_b = proj(g0 + CONV_CH, g0 + 2 * CONV_CH)
    u_ref[...] = glu_a * jax.nn.sigmoid(glu_b)
    g1 = g0 + 2 * CONV_CH
    sa_ref[...] = jax.nn.sigmoid(proj(g1, g1 + D_MODEL)).astype(BF16)
    sb_ref[...] = jax.nn.sigmoid(proj(g1 + D_MODEL, g1 + 2 * D_MODEL)).astype(BF16)


def _inproj(x, w_main, b_main, w_f, b_f, tri, *, tiles_per_seq):
    rows = x.shape[0]
    n_main = w_main.shape[1]
    row_spec = lambda w: pl.BlockSpec((IN_TILE, w), lambda i: (i, 0))
    out_shapes = (
        jax.ShapeDtypeStruct((rows, FOX_WIDTH), BF16),
        jax.ShapeDtypeStruct((rows * FOX_HEADS, HEAD_DIM), F32),
        jax.ShapeDtypeStruct((rows * FOX_HEADS, HEAD_DIM), F32),
        jax.ShapeDtypeStruct((rows, FOX_WIDTH), BF16),
        jax.ShapeDtypeStruct((rows, FOX_WIDTH), BF16),
        jax.ShapeDtypeStruct((rows, FOX_HEADS), F32),
        jax.ShapeDtypeStruct((rows, FOX_HEADS), F32),
        jax.ShapeDtypeStruct((rows, CONV_CH), F32),
        jax.ShapeDtypeStruct((rows, D_MODEL), BF16),
        jax.ShapeDtypeStruct((rows, D_MODEL), BF16),
    )
    head_spec = pl.BlockSpec((IN_TILE * FOX_HEADS, HEAD_DIM), lambda i: (i, 0))
    out_specs = (row_spec(FOX_WIDTH), head_spec, head_spec) + (row_spec(FOX_WIDTH),) * 2 + (
        row_spec(FOX_HEADS),) * 2 + (row_spec(CONV_CH), row_spec(D_MODEL), row_spec(D_MODEL))
    return pl.pallas_call(
        functools.partial(_inproj_kernel, tiles_per_seq=tiles_per_seq),
        grid=(rows // IN_TILE,),
        in_specs=[row_spec(D_MODEL), _const_spec((D_MODEL, n_main)), _const_spec((1, n_main)),
                  _const_spec((D_MODEL, LANES)), _const_spec((1, LANES)),
                  _const_spec((IN_TILE, IN_TILE))],
        out_specs=out_specs,
        out_shape=out_shapes,
        scratch_shapes=[pltpu.VMEM((1, LANES), F32)],
        compiler_params=pltpu.CompilerParams(
            dimension_semantics=("arbitrary",), vmem_limit_bytes=VMEM_LIMIT),
        name="inproj",
    )(x, w_main, b_main, w_f, b_f, tri)


def _attn_prompt_kernel(q_ref, k_ref, v_ref, crow_ref, o_ref, qm_scr, m_scr, acc_scr):
    t = ATTN_TILE
    i = pl.program_id(1)
    lane = lax.broadcasted_iota(jnp.int32, (t, LANES), 1)
    row = lax.broadcasted_iota(jnp.int32, (t, t), 0)
    col = lax.broadcasted_iota(jnp.int32, (t, t), 1)
    causal = col <= row
    wide = lambda x: jnp.concatenate([x] * (t // LANES), axis=1)
    pair_lanes = lambda pair: slice(pair * LANES, (pair + 1) * LANES)

    for h in range(FOX_HEADS):
        qp = q_ref[:, pair_lanes(h // 2)]
        in_head = (lane < HEAD_DIM) if h % 2 == 0 else (lane >= HEAD_DIM)
        qm_scr[h] = jnp.where(in_head, qp, jnp.zeros_like(qp))
        m_scr[h] = jnp.full((t, LANES), NEG_INF, F32)
        acc_scr[h] = jnp.zeros((t, 2 * LANES), F32)
    ones = jnp.ones((t, LANES), BF16)

    def step(j, masked):
        r0 = pl.multiple_of(j * t, t)
        ck = crow_ref[0, :, pl.ds(r0, t)] * LOG2E
        for pair in range(FOX_HEADS // 2):
            kj = k_ref[pl.ds(r0, t), pair_lanes(pair)]
            vj = jnp.concatenate([v_ref[pl.ds(r0, t), pair_lanes(pair)], ones], axis=1)
            for h in (2 * pair, 2 * pair + 1):
                s = _dot_nt(qm_scr[h], kj) - ck[h:h + 1, :]
                if masked:
                    s = jnp.where(causal, s, NEG_INF)
                m_prev = m_scr[h]
                m_new = jnp.maximum(m_prev, jnp.max(s, axis=1, keepdims=True))
                alpha = jnp.exp2(m_prev - m_new)
                p = jnp.exp2(s - wide(m_new))
                acc_scr[h] = wide(alpha) * acc_scr[h] + _dot(p.astype(BF16), vj)
                m_scr[h] = m_new

    def body(jj, carry):
        step(2 * jj, False)
        step(2 * jj + 1, False)
        return carry

    lax.fori_loop(0, i // 2, body, 0)

    @pl.when(i % 2 == 1)
    def _():
        step(i - 1, False)

    step(i, True)
    for pair in range(FOX_HEADS // 2):
        o0, o1 = (acc_scr[h, :, :LANES] / acc_scr[h, :, LANES:] for h in (2 * pair, 2 * pair + 1))
        o_ref[:, pair_lanes(pair)] = jnp.where(lane < HEAD_DIM, o0, o1).astype(BF16)


def _attn_prompt(q, kb, vb, c_row, *, batch, seq):
    t = ATTN_TILE
    nq = seq // t
    return pl.pallas_call(
        _attn_prompt_kernel,
        grid=(batch, nq),
        in_specs=[
            pl.BlockSpec((t, FOX_WIDTH), lambda b, i: (b * nq + i, 0)),
            pl.BlockSpec((seq, FOX_WIDTH), lambda b, i: (b, 0)),
            pl.BlockSpec((seq, FOX_WIDTH), lambda b, i: (b, 0)),
            pl.BlockSpec((1, FOX_HEADS, seq), lambda b, i: (b, 0, 0)),
        ],
        out_specs=pl.BlockSpec((t, FOX_WIDTH), lambda b, i: (b * nq + i, 0)),
        out_shape=jax.ShapeDtypeStruct((batch * seq, FOX_WIDTH), BF16),
        scratch_shapes=[pltpu.VMEM((FOX_HEADS, t, LANES), BF16),
                        pltpu.VMEM((FOX_HEADS, t, LANES), F32),
                        pltpu.VMEM((FOX_HEADS, t, 2 * LANES), F32)],
        compiler_params=pltpu.CompilerParams(
            dimension_semantics=("arbitrary", "arbitrary"), vmem_limit_bytes=VMEM_LIMIT),
        name="attn_prompt",
    )(q, kb, vb, c_row)


def _attn_sample_kernel(q_ref, kn_ref, vn_ref, lf_ref, lft_ref, ck_ref, cv_ref, clft_ref,
                        upper_ref, o_ref, *, t, past):
    rows = FOX_HEADS * t
    lane_head = lax.broadcasted_iota(jnp.int32, (t, FOX_WIDTH), 1) // HEAD_DIM
    q = q_ref[...]
    q_stack = jnp.concatenate(
        [jnp.where(lane_head == h, q, jnp.zeros_like(q)) for h in range(FOX_HEADS)], axis=0)

    clf = clft_ref[0]
    prefix = _exact_dot(upper_ref[...], clf, ones_on_left=False)
    to_end = prefix[:, past - 1:past] - prefix
    ri = lax.broadcasted_iota(jnp.int32, (t, t), 0)
    ci = lax.broadcasted_iota(jnp.int32, (t, t), 1)
    lower = jnp.where(ci <= ri, 1.0, 0.0).astype(BF16)
    upper = jnp.where(ri <= ci, 1.0, 0.0).astype(BF16)
    cn_col = _exact_dot(lower, lf_ref[...], ones_on_left=True)
    cn_row = _exact_dot(upper, lft_ref[0], ones_on_left=False)

    stack = lambda f: jnp.concatenate([f(h) for h in range(FOX_HEADS)], axis=0)
    cn_stack = stack(lambda h: cn_col[:, h:h + 1])
    bias_c = stack(lambda h: jnp.broadcast_to(to_end[h:h + 1, :], (t, past)))
    bias_n = stack(lambda h: jnp.broadcast_to(cn_row[h:h + 1, :], (t, t)))

    kc = ck_ref[0].astype(BF16)
    vc = cv_ref[0].astype(BF16)
    s_c = _dot_nt(q_stack, kc) + (bias_c + cn_stack) * LOG2E
    s_n = _dot_nt(q_stack, kn_ref[...]) + (cn_stack - bias_n) * LOG2E
    tq = lax.broadcasted_iota(jnp.int32, (rows, t), 0) % t
    tk = lax.broadcasted_iota(jnp.int32, (rows, t), 1)
    s_n = jnp.where(tk <= tq, s_n, NEG_INF)
    m = jnp.maximum(jnp.max(s_c, axis=1, keepdims=True), jnp.max(s_n, axis=1, keepdims=True))
    p_c = jnp.exp2(s_c - m)
    p_n = jnp.exp2(s_n - m)
    denom = jnp.sum(p_c, axis=1, keepdims=True) + jnp.sum(p_n, axis=1, keepdims=True)
    o_stack = (_dot(p_c.astype(BF16), vc) + _dot(p_n.astype(BF16), vn_ref[...])) / denom
    out = jnp.zeros((t, FOX_WIDTH), F32)
    for h in range(FOX_HEADS):
        out = out + jnp.where(lane_head == h, o_stack[h * t:(h + 1) * t, :], 0.0)
    o_ref[...] = out.astype(BF16)


def _attn_sample(q, kb, vb, logf, logf_t, cache_k, cache_v, cache_logf_t, upper, *, batch, t, past):
    row_spec = lambda w: pl.BlockSpec((t, w), lambda b: (b, 0))
    return pl.pallas_call(
        functools.partial(_attn_sample_kernel, t=t, past=past),
        grid=(batch,),
        in_specs=[
            row_spec(FOX_WIDTH), row_spec(FOX_WIDTH), row_spec(FOX_WIDTH), row_spec(FOX_HEADS),
            pl.BlockSpec((1, FOX_HEADS, t), lambda b: (b, 0, 0)),
            pl.BlockSpec((1, past, FOX_WIDTH), lambda b: (b, 0, 0)),
            pl.BlockSpec((1, past, FOX_WIDTH), lambda b: (b, 0, 0)),
            pl.BlockSpec((1, FOX_HEADS, past), lambda b: (b, 0, 0)),
            _const_spec((past, past)),
        ],
        out_specs=row_spec(FOX_WIDTH),
        out_shape=jax.ShapeDtypeStruct((batch * t, FOX_WIDTH), BF16),
        compiler_params=pltpu.CompilerParams(
            dimension_semantics=("arbitrary",), vmem_limit_bytes=VMEM_LIMIT),
        name="attn_sample",
    )(q, kb, vb, logf, logf_t, cache_k, cache_v, cache_logf_t, upper)


def _merge_kernel(u_ref, hist_ref, attn_ref, sa_ref, sb_ref, x_ref,
                  cw_ref, cb_ref, cg_ref, cbeta_ref, wb_ref, bb_ref, wa_ref, wout_ref,
                  g1_ref, b1_ref, *rest, ts, tiles_per_seq, n_tiles, zero_first_hist, alpha,
                  aliased):
    if aliased:
        rest = rest[2:]
    mid_ref, mid3_ref, ue_scr, ph_scr, h_scr = rest
    i = pl.program_id(0)

    @pl.when(i >= n_tiles)
    def _():
        mid_ref[...] = jnp.zeros_like(mid_ref)
        mid3_ref[...] = jnp.zeros_like(mid3_ref)

    @pl.when(i < n_tiles)
    def _():
        _merge_tile(u_ref, hist_ref, attn_ref, sa_ref, sb_ref, x_ref, cw_ref, cb_ref, cg_ref,
                    cbeta_ref, wb_ref, bb_ref, wa_ref, wout_ref, g1_ref, b1_ref, mid_ref,
                    mid3_ref, ue_scr, ph_scr, h_scr, ts=ts, alpha=alpha,
                    zero_hist=(i % tiles_per_seq == 0) if zero_first_hist else None)


def _merge_tile(u_ref, hist_ref, attn_ref, sa_ref, sb_ref, x_ref, cw_ref, cb_ref, cg_ref,
                cbeta_ref, wb_ref, bb_ref, wa_ref, wout_ref, g1_ref, b1_ref, mid_ref, mid3_ref,
                ue_scr, ph_scr, h_scr, *, ts, alpha, zero_hist):
    hist = hist_ref[...]
    if zero_hist is not None:
        hist = jnp.where(zero_hist, 0.0, hist)
    ue_scr[0:HIST, :] = hist
    ue_scr[HIST:HIST + ts, :] = u_ref[...]
    lead = HIST - (CONV_WIDTH - 1)
    for r in range(SUBLANES):
        span = ts + (CONV_WIDTH - 1 - r) // SUBLANES * SUBLANES
        ph_scr[r, 0:span, :] = ue_scr[lead + r:lead + r + span, :]

    def conv_rows(rt, carry):
        base = pl.multiple_of(rt * CONV_SUB, CONV_SUB)
        acc = jnp.zeros((CONV_SUB, CONV_CH), F32)
        for j in range(CONV_WIDTH):
            r, a = j % SUBLANES, j // SUBLANES
            acc = acc + cw_ref[j:j + 1, :] * ph_scr[r, pl.ds(base + a * SUBLANES, CONV_SUB), :]
        h_scr[pl.ds(base, CONV_SUB), :] = acc
        return carry

    lax.fori_loop(0, ts // CONV_SUB, conv_rows, 0)
    h = _layer_norm(h_scr[...] + cb_ref[...], cg_ref[...], cbeta_ref[...])
    h = h * jax.nn.sigmoid(h)
    conv_out = _dot(h.astype(BF16), wb_ref[...]) + bb_ref[...]
    attn_out = _dot(attn_ref[...], wa_ref[...])
    m = sa_ref[...].astype(F32) * attn_out + sb_ref[...].astype(F32) * conv_out
    z = alpha * x_ref[...] + _dot(m.astype(BF16), wout_ref[...])
    mid = _layer_norm(z, g1_ref[...], b1_ref[...])
    mid_ref[...] = mid
    for j in range(ROW_TILES):
        mid3_ref[pl.ds(j, ts, stride=ROW_TILES), :] = mid[:, j * LANES:(j + 1) * LANES]


def _merge(u, hist, attn, sa, sb, x, conv_p, wa, wout, g1, b1, *, n_seq, seq, ts, hist_from_u,
           alpha, total_rows, row_offset, prev=None):
    nt = seq // ts
    n_tiles = n_seq * nt
    off = row_offset // ts
    grid_tiles = n_tiles if prev is not None else total_rows // ts
    src = lambda i: jnp.minimum(i, n_tiles - 1)
    row_spec = lambda w: pl.BlockSpec((ts, w), lambda i: (src(i), 0))
    if hist_from_u:
        per = ts // HIST
        hist_spec = pl.BlockSpec((HIST, CONV_CH), lambda i: (jnp.maximum(src(i) * per - 1, 0), 0))
    else:
        hist_spec = pl.BlockSpec((HIST, CONV_CH), lambda i: (src(i) // nt, 0))
    cw, cb, cg, cbeta, wb, bb = conv_p
    consts = [cw, cb, cg, cbeta, wb, bb, wa, wout, g1, b1]
    in_specs = [row_spec(CONV_CH), hist_spec, row_spec(FOX_WIDTH), row_spec(D_MODEL),
                row_spec(D_MODEL), row_spec(D_MODEL)] + [_const_spec(c.shape) for c in consts]
    args = [u, hist, attn, sa, sb, x] + consts
    aliases = {}
    if prev is not None:
        in_specs += [pl.BlockSpec(memory_space=pl.ANY)] * 2
        aliases = {len(args): 0, len(args) + 1: 1}
        args += list(prev)
    span = ts + (CONV_WIDTH - 1) // SUBLANES * SUBLANES
    return pl.pallas_call(
        functools.partial(_merge_kernel, ts=ts, tiles_per_seq=nt, n_tiles=n_tiles,
                          zero_first_hist=hist_from_u, alpha=alpha, aliased=prev is not None),
        grid=(grid_tiles,),
        in_specs=in_specs,
        out_specs=(pl.BlockSpec((ts, D_MODEL), lambda i: (off + i, 0)),
                   pl.BlockSpec((ts * ROW_TILES, LANES), lambda i: (off + i, 0))),
        out_shape=(jax.ShapeDtypeStruct((total_rows, D_MODEL), F32),
                   jax.ShapeDtypeStruct((total_rows * ROW_TILES, LANES), F32)),
        scratch_shapes=[pltpu.VMEM((HIST + ts, CONV_CH), F32),
                        pltpu.VMEM((SUBLANES, span, CONV_CH), F32),
                        pltpu.VMEM((ts, CONV_CH), F32)],
        input_output_aliases=aliases,
        compiler_params=pltpu.CompilerParams(
            dimension_semantics=("arbitrary",), vmem_limit_bytes=VMEM_LIMIT),
        name="merge",
    )(*args)


def _router_kernel(mid_ref, wr_hi_ref, wr_lo_ref, br_ref, before_ref,
                   eidx_ref, gate_ref, rank_ref, cnt_ref, carry_scr):
    tr = ROUTER_TILE
    i = pl.program_id(0)

    @pl.when(i == 0)
    def _():
        carry_scr[...] = jnp.zeros_like(carry_scr)

    x = mid_ref[...]
    x_hi = x.astype(BF16)
    x_lo = (x - x_hi.astype(F32)).astype(BF16)
    wr_hi = wr_hi_ref[...]
    logits = _dot_nt(wr_hi, x_hi) + _dot_nt(wr_hi, x_lo) + _dot_nt(wr_lo_ref[...], x_hi)
    scores = jax.nn.sigmoid(logits)
    sel = scores + br_ref[...]

    sel3 = sel.reshape(N_GROUPS, GROUP_SIZE, tr)
    in_group = lax.broadcasted_iota(jnp.int32, sel3.shape, 1)
    m1 = jnp.max(sel3, axis=1, keepdims=True)
    first = jnp.min(jnp.where(sel3 == m1, in_group, GROUP_SIZE), axis=1, keepdims=True)
    m2 = jnp.max(jnp.where(in_group == first, NEG_INF, sel3), axis=1, keepdims=True)
    gs = m1 + m2
    gi = lax.broadcasted_iota(jnp.int32, gs.shape, 0)
    beaten = jnp.zeros(gs.shape, F32)
    for g in range(N_GROUPS):
        other = gs[g:g + 1]
        wins = (other > gs) | ((other == gs) & (g < gi))
        beaten = beaten + jnp.where(wins, 1.0, 0.0)
    drop = jnp.where(beaten < TOPK_GROUPS, 0.0, NEG_INF)
    cur = (sel3 + drop).reshape(N_EXPERTS, tr)

    ei = lax.broadcasted_iota(jnp.int32, (N_EXPERTS, tr), 0)
    idxs, vals = [], []
    picked = jnp.zeros((N_EXPERTS, tr), F32)
    for _ in range(TOP_K):
        m = jnp.max(cur, axis=0, keepdims=True)
        idx = jnp.min(jnp.where(cur == m, ei, N_EXPERTS), axis=0, keepdims=True)
        hit = ei == idx
        vals.append(jnp.sum(jnp.where(hit, scores, 0.0), axis=0, keepdims=True))
        idxs.append(idx)
        picked = picked + jnp.where(hit, 1.0, 0.0)
        cur = jnp.where(hit, NEG_INF, cur)

    total = vals[0]
    for v in vals[1:]:
        total = total + v
    for k in range(TOP_K):
        gate_ref[k:k + 1, :] = vals[k] / total * ROUTED_SCALE
        eidx_ref[k:k + 1, :] = idxs[k]

    ahead = _dot(picked.astype(BF16), before_ref[...]) + carry_scr[:, 0:1]
    for k in range(TOP_K):
        rank = jnp.sum(jnp.where(ei == idxs[k], ahead, 0.0), axis=0, keepdims=True)
        rank_ref[k:k + 1, :] = rank.astype(jnp.int32)
    carry_scr[...] = carry_scr[...] + jnp.sum(picked, axis=1, keepdims=True)
    cnt_ref[...] = carry_scr[...]


def _router(mid, wr_hi, wr_lo, br, before):
    tr = ROUTER_TILE
    tokens = mid.shape[0]
    tok_spec = pl.BlockSpec((TOP_K, tr), lambda i: (0, i))
    return pl.pallas_call(
        _router_kernel,
        grid=(tokens // tr,),
        in_specs=[pl.BlockSpec((tr, D_MODEL), lambda i: (i, 0)),
                  _const_spec((N_EXPERTS, D_MODEL)), _const_spec((N_EXPERTS, D_MODEL)),
                  _const_spec((N_EXPERTS, 1)), _const_spec((tr, tr))],
        out_specs=(tok_spec, tok_spec, tok_spec,
                   pl.BlockSpec((N_EXPERTS, LANES), lambda i: (0, 0))),
        out_shape=(jax.ShapeDtypeStruct((TOP_K, tokens), jnp.int32),
                   jax.ShapeDtypeStruct((TOP_K, tokens), F32),
                   jax.ShapeDtypeStruct((TOP_K, tokens), jnp.int32),
                   jax.ShapeDtypeStruct((N_EXPERTS, LANES), F32)),
        scratch_shapes=[pltpu.VMEM((N_EXPERTS, LANES), F32)],
        compiler_params=pltpu.CompilerParams(
            dimension_semantics=("arbitrary",), vmem_limit_bytes=VMEM_LIMIT),
        name="router",
    )(mid, wr_hi, wr_lo, br, before)


def _dest_kernel(eidx_ref, rank_ref, starts_ref, dest_ref):
    tt = MOVE_TILE
    tokens = eidx_ref.shape[1]
    ei = lax.broadcasted_iota(jnp.int32, (N_EXPERTS, tokens), 0)
    starts = starts_ref[...]
    for k in range(TOP_K):
        hit = ei == eidx_ref[k:k + 1, :]
        start = jnp.sum(jnp.where(hit, starts, 0.0), axis=0, keepdims=True)
        dest = start.astype(jnp.int32) + rank_ref[k:k + 1, :]
        for c in range(tokens // tt):
            dest_ref[c, k:k + 1, :] = dest[:, c * tt:(c + 1) * tt]


def _dest(eidx, rank, starts):
    tt = MOVE_TILE
    tokens = eidx.shape[1]
    step = DEST_TILE
    tok_spec = pl.BlockSpec((TOP_K, step), lambda i: (0, i))
    return pl.pallas_call(
        _dest_kernel,
        grid=(tokens // step,),
        in_specs=[tok_spec, tok_spec, _const_spec((N_EXPERTS, 1))],
        out_specs=pl.BlockSpec((step // tt, TOP_K, tt), lambda i: (i, 0, 0)),
        out_shape=jax.ShapeDtypeStruct((tokens // tt, TOP_K, tt), jnp.int32),
        compiler_params=pltpu.CompilerParams(dimension_semantics=("arbitrary",)),
        name="dest",
    )(eidx, rank, starts)


def _row_slice(ref, row):
    return ref.at[pl.ds(pl.multiple_of(row * ROW_TILES, ROW_TILES), ROW_TILES), :]


def _dispatch_kernel(dest_ref, rows_ref, xs_ref, zeros_scr, sem, *, n_rows):
    tt = MOVE_TILE

    @pl.when(pl.program_id(0) == 0)
    def _():
        zeros_scr[...] = jnp.zeros_like(zeros_scr)
        tail = pltpu.make_async_copy(
            zeros_scr, xs_ref.at[pl.ds(n_rows * ROW_TILES, EXPERT_BLOCK * ROW_TILES), :], sem)
        tail.start()
        tail.wait()

    def issue(t, carry):
        src = _row_slice(rows_ref, t)
        for k in range(TOP_K):
            pltpu.make_async_copy(src, _row_slice(xs_ref, dest_ref[k * tt + t]), sem).start(
                priority=k % 2)
        return carry

    lax.fori_loop(0, tt, issue, 0)
    for _ in range(TOP_K):
        pltpu.make_async_copy(rows_ref, xs_ref.at[pl.ds(0, tt * ROW_TILES), :], sem).wait()


def _dispatch(dest_flat, rows3):
    tt = MOVE_TILE
    tokens = rows3.shape[0] // ROW_TILES
    n_rows = tokens * TOP_K
    return pl.pallas_call(
        functools.partial(_dispatch_kernel, n_rows=n_rows),
        grid=(tokens // tt,),
        in_specs=[pl.BlockSpec((tt * TOP_K,), lambda i: (i,), memory_space=pltpu.SMEM),
                  pl.BlockSpec((tt * ROW_TILES, LANES), lambda i: (i, 0))],
        out_specs=pl.BlockSpec(memory_space=pl.ANY),
        out_shape=jax.ShapeDtypeStruct(((n_rows + EXPERT_BLOCK) * ROW_TILES, LANES), F32),
        scratch_shapes=[pltpu.VMEM((EXPERT_BLOCK * ROW_TILES, LANES), F32),
                        pltpu.SemaphoreType.DMA(())],
        compiler_params=pltpu.CompilerParams(dimension_semantics=("arbitrary",)),
        name="dispatch",
    )(dest_flat, rows3)


_PIECES = tuple(EXPERT_BLOCK >> s for s in range(EXPERT_BLOCK.bit_length()))
CHUNKS_PER_STEP = 2
ROWS_AHEAD = 6
ROW_SLOTS = ROWS_AHEAD + CHUNKS_PER_STEP
OUT_SLOTS = 2 * CHUNKS_PER_STEP


def _expert_kernel(cexp_ref, crow_ref, cn_ref, cnew_ref, cnext_ref, cw_ref, nsteps_ref,
                   xs_hbm, wg_hbm, wu_hbm, wd_hbm, ys_hbm,
                   xbuf, ybuf, wg_buf, wu_buf, wd_buf, wgb, wub, wdb, xsem, ysem, wsem):
    bm = EXPERT_BLOCK
    s = pl.program_id(0)
    n_steps = nsteps_ref[0]
    n_chunks = n_steps * CHUNKS_PER_STEP
    chunks = [s * CHUNKS_PER_STEP + c for c in range(CHUNKS_PER_STEP)]

    def weight_copies(e, s):
        return [pltpu.make_async_copy(hbm.at[e], buf.at[s], wsem.at[s])
                for hbm, buf in ((wg_hbm, wg_buf), (wu_hbm, wu_buf), (wd_hbm, wd_buf))]

    def rows_in(j, s):
        first = pl.multiple_of(crow_ref[j] * ROW_TILES, ROW_TILES)
        return pltpu.make_async_copy(xs_hbm.at[pl.ds(first, bm * ROW_TILES), :], xbuf.at[s],
                                     xsem.at[s])

    def rows_out(j, s, act):
        n, row0 = cn_ref[j], crow_ref[j]

        def piece(done, p):
            src = pl.multiple_of(done * ROW_TILES, ROW_TILES)
            dst = pl.multiple_of((row0 + done) * ROW_TILES, ROW_TILES)
            act(pltpu.make_async_copy(ybuf.at[s, pl.ds(src, p * ROW_TILES), :],
                                      ys_hbm.at[pl.ds(dst, p * ROW_TILES), :], ysem.at[s]))

        @pl.when(n == bm)
        def _():
            piece(jnp.int32(0), bm)

        @pl.when(n < bm)
        def _():
            done = jnp.int32(0)
            for p in _PIECES[1:]:
                has = (n & p) != 0

                @pl.when(has)
                def _(done=done, p=p):
                    piece(done, p)

                done = done + jnp.where(has, p, 0)

    start = lambda c: c.start()
    wait = lambda c: c.wait()

    @pl.when(s == 0)
    def _():
        for d in range(ROWS_AHEAD):
            @pl.when(d < n_chunks)
            def _(d=d):
                rows_in(d, d).start()
        for c in weight_copies(cexp_ref[0], 0):
            c.start()

    @pl.when(s < n_steps)
    def _():
        for j in chunks:
            slot = cnew_ref[j]

            @pl.when(slot >= 0)
            def _(j=j, slot=slot):
                for c in weight_copies(cexp_ref[j], slot):
                    c.wait()
                nxt = cnext_ref[j]

                @pl.when(nxt >= 0)
                def _():
                    for c in weight_copies(nxt, 1 - slot):
                        c.start()

                wgb[slot] = wg_buf[slot].astype(BF16)
                wub[slot] = wu_buf[slot].astype(BF16)
                wdb[slot] = wd_buf[slot].astype(BF16)

        for j in chunks:
            @pl.when(j + ROWS_AHEAD < n_chunks)
            def _(j=j):
                rows_in(j + ROWS_AHEAD, (j + ROWS_AHEAD) % ROW_SLOTS).start()

        for j in chunks:
            rows_in(j, j % ROW_SLOTS).wait()

        for j in chunks:
            @pl.when(j >= OUT_SLOTS)
            def _(j=j):
                rows_out(j - OUT_SLOTS, j % OUT_SLOTS, wait)

        for j in chunks:
            xs_slot, w_slot, y_slot = j % ROW_SLOTS, cw_ref[j], j % OUT_SLOTS
            x = jnp.concatenate(
                [xbuf[xs_slot, pl.ds(r, bm, stride=ROW_TILES), :] for r in range(ROW_TILES)],
                axis=1)
            xb = x.astype(BF16)
            g = _dot(xb, wgb[w_slot])
            u = _dot(xb, wub[w_slot])
            h = (g * jax.nn.sigmoid(g) * u).astype(BF16)
            y = _dot(h, wdb[w_slot])
            for r in range(ROW_TILES):
                ybuf[y_slot, pl.ds(r, bm, stride=ROW_TILES), :] = y[:, r * LANES:(r + 1) * LANES]

        for j in chunks:
            rows_out(j, j % OUT_SLOTS, start)

        @pl.when(s == n_steps - 1)
        def _():
            for d in range(OUT_SLOTS):
                last = n_chunks - 1 - d

                @pl.when(last >= 0)
                def _(last=last):
                    rows_out(last, last % OUT_SLOTS, wait)


def _experts(cexp, crow, cn, cnew, cnext, cw, nsteps, xs, wg, wu, wd, n_rows):
    bm = EXPERT_BLOCK
    any_spec = pl.BlockSpec(memory_space=pl.ANY)
    grid_spec = pltpu.PrefetchScalarGridSpec(
        num_scalar_prefetch=7,
        grid=(cexp.shape[0] // CHUNKS_PER_STEP,),
        in_specs=[any_spec] * 4,
        out_specs=any_spec,
        scratch_shapes=[pltpu.VMEM((ROW_SLOTS, bm * ROW_TILES, LANES), F32),
                        pltpu.VMEM((OUT_SLOTS, bm * ROW_TILES, LANES), F32),
                        pltpu.VMEM((2, D_MODEL, D_EXPERT), F32),
                        pltpu.VMEM((2, D_MODEL, D_EXPERT), F32),
                        pltpu.VMEM((2, D_EXPERT, D_MODEL), F32),
                        pltpu.VMEM((2, D_MODEL, D_EXPERT), BF16),
                        pltpu.VMEM((2, D_MODEL, D_EXPERT), BF16),
                        pltpu.VMEM((2, D_EXPERT, D_MODEL), BF16),
                        pltpu.SemaphoreType.DMA((ROW_SLOTS,)),
                        pltpu.SemaphoreType.DMA((OUT_SLOTS,)),
                        pltpu.SemaphoreType.DMA((2,))],
    )
    return pl.pallas_call(
        _expert_kernel,
        grid_spec=grid_spec,
        out_shape=jax.ShapeDtypeStruct((n_rows * ROW_TILES, LANES), F32),
        compiler_params=pltpu.CompilerParams(
            dimension_semantics=("arbitrary",), vmem_limit_bytes=VMEM_LIMIT),
        name="experts",
    )(cexp, crow, cn, cnew, cnext, cw, nsteps, xs, wg, wu, wd)


GATHER_AHEAD = 2
GATHER_SLOTS = GATHER_AHEAD + 1


def _combine_kernel(dest0_ref, dest1_ref, dest2_ref, ys_ref, gate_ref, mid_ref, wsg_ref,
                    wsu_ref, wsd_ref, g2_ref, b2_ref, out_a_ref, out_b_ref, buf, gate_scr,
                    routed_scr, sem, *, alpha, steps_a):
    tt = MOVE_TILE
    i = pl.program_id(0)
    n = pl.num_programs(0)
    cur = i % GATHER_SLOTS
    ahead = (i + GATHER_AHEAD) % GATHER_SLOTS

    def row_copy(slots_ref, b, t, k):
        return pltpu.make_async_copy(_row_slice(ys_ref, slots_ref[k * tt + t]),
                                     _row_slice(buf.at[b], k * tt + t), sem.at[b])

    def start_rows(slots_ref, b):
        def issue(t, carry):
            for k in range(TOP_K):
                row_copy(slots_ref, b, t, k).start(priority=k % 2)
            return carry
        lax.fori_loop(0, tt, issue, 0)

    def wait_rows(b):
        pltpu.make_async_copy(ys_ref.at[pl.ds(0, TOP_K * tt * ROW_TILES), :], buf.at[b],
                              sem.at[b]).wait()

    @pl.when(i == 0)
    def _():
        start_rows(dest0_ref, 0)
        start_rows(dest1_ref, 1)

    mid = mid_ref[...]
    xb = mid.astype(BF16)
    g = _dot(xb, wsg_ref[...])
    h = (g * jax.nn.sigmoid(g) * _dot(xb, wsu_ref[...])).astype(BF16)
    acc = alpha * mid + _dot(h, wsd_ref[...])

    wait_rows(cur)
    gates = gate_ref[...]
    for k in range(TOP_K):
        gate_scr[k] = jnp.broadcast_to(gates[:, k:k + 1], (tt, LANES))
    per_block = tt // (ROW_TILES * TOP_K)
    for j in range(ROW_TILES):
        part = jnp.zeros((tt, LANES), F32)
        for k in range(TOP_K):
            part = part + gate_scr[k] * buf[cur, pl.ds(k * tt * ROW_TILES + j, tt,
                                                        stride=ROW_TILES), :]
            t0 = (j * TOP_K + k) * per_block
            for t in range(t0, t0 + per_block):
                for kk in range(TOP_K):
                    row_copy(dest2_ref, ahead, t, kk).start(priority=kk % 2)
        routed_scr[:, j * LANES:(j + 1) * LANES] = part
    out = _layer_norm(acc + routed_scr[...], g2_ref[...], b2_ref[...])

    @pl.when(i == n - 1)
    def _():
        for d in range(1, GATHER_SLOTS):
            wait_rows((i + d) % GATHER_SLOTS)

    @pl.when(i < steps_a)
    def _():
        out_a_ref[...] = out

    @pl.when(i >= steps_a)
    def _():
        out_b_ref[...] = out


def _combine(dest_flat, ys, gate, mid, wsg, wsu, wsd, g2, b2, *, alpha, rows_a):
    tt = MOVE_TILE
    tokens = mid.shape[0]
    n = tokens // tt
    steps_a = rows_a // tt
    consts = [wsg, wsu, wsd, g2, b2]
    slots_spec = lambda f: pl.BlockSpec((tt * TOP_K,), f, memory_space=pltpu.SMEM)
    return pl.pallas_call(
        functools.partial(_combine_kernel, alpha=alpha, steps_a=steps_a),
        grid=(n,),
        in_specs=[slots_spec(lambda i: (i,)),
                  slots_spec(lambda i: (jnp.minimum(i + 1, n - 1),)),
                  slots_spec(lambda i: (jnp.minimum(i + GATHER_AHEAD, n - 1),)),
                  pl.BlockSpec(memory_space=pl.ANY),
                  pl.BlockSpec((tt, TOP_K), lambda i: (i, 0)),
                  pl.BlockSpec((tt, D_MODEL), lambda i: (i, 0))]
                 + [_const_spec(c.shape) for c in consts],
        out_specs=(pl.BlockSpec((tt, D_MODEL), lambda i: (jnp.minimum(i, steps_a - 1), 0)),
                   pl.BlockSpec((tt, D_MODEL), lambda i: (jnp.maximum(i - steps_a, 0), 0))),
        out_shape=(jax.ShapeDtypeStruct((rows_a, D_MODEL), F32),
                   jax.ShapeDtypeStruct((tokens - rows_a, D_MODEL), F32)),
        scratch_shapes=[pltpu.VMEM((GATHER_SLOTS, TOP_K * tt * ROW_TILES, LANES), F32),
                        pltpu.VMEM((TOP_K, tt, LANES), F32),
                        pltpu.VMEM((tt, D_MODEL), F32),
                        pltpu.SemaphoreType.DMA((GATHER_SLOTS,))],
        compiler_params=pltpu.CompilerParams(
            dimension_semantics=("arbitrary",), vmem_limit_bytes=VMEM_LIMIT),
        name="combine",
    )(dest_flat, dest_flat, dest_flat, ys, gate, mid, *consts)


def _tri(n, *, lower):
    r = lax.broadcasted_iota(jnp.int32, (n, n), 0)
    c = lax.broadcasted_iota(jnp.int32, (n, n), 1)
    return jnp.where((c <= r) if lower else (r <= c), 1.0, 0.0).astype(BF16)


def _moe(mid, rows3, w_router, b_router, w_e_gate, w_e_up, w_e_down, wsg, wsu, wsd, g2, b2, alpha,
         rows_a):
    tokens = mid.shape[0]
    bm = EXPERT_BLOCK
    wr_t = w_router.T
    wr_hi = wr_t.astype(BF16)
    wr_lo = (wr_t - wr_hi.astype(F32)).astype(BF16)
    r = lax.broadcasted_iota(jnp.int32, (ROUTER_TILE, ROUTER_TILE), 0)
    c = lax.broadcasted_iota(jnp.int32, (ROUTER_TILE, ROUTER_TILE), 1)
    before = jnp.where(r < c, 1.0, 0.0).astype(BF16)
    eidx, gate, rank, cnt = _router(mid, wr_hi, wr_lo, b_router.reshape(N_EXPERTS, 1), before)

    i32 = lambda a: a.astype(jnp.int32)
    experts = jnp.arange(N_EXPERTS, dtype=jnp.int32)
    counts = i32(cnt[:, 0])
    starts = jnp.cumsum(counts) - counts
    n_ch = (counts + bm - 1) // bm
    ch_ends = jnp.cumsum(n_ch)
    ch_starts = ch_ends - n_ch
    n_used = ch_ends[-1]
    max_chunks = tokens * TOP_K // bm + N_EXPERTS
    ci = jnp.arange(max_chunks, dtype=jnp.int32)
    cc = jnp.minimum(ci, n_used - 1)
    cexp = jnp.minimum(jnp.sum(i32(ch_ends[None, :] <= cc[:, None]), axis=1), N_EXPERTS - 1)
    hot = cexp[:, None] == experts[None, :]
    lookup = lambda table: jnp.sum(jnp.where(hot, table[None, :], 0), axis=1)
    k_in_expert = cc - lookup(ch_starts)
    crow = lookup(starts) + k_in_expert * bm
    used = ci < n_used
    cn = jnp.where(used, jnp.clip(lookup(counts) - k_in_expert * bm, 0, bm), 0)
    is_first = used & (k_in_expert == 0)
    cw = (jnp.cumsum(i32(is_first)) - 1) % 2
    cnew = jnp.where(is_first, cw, -1)
    later = (experts[None, :] > experts[:, None]) & (n_ch[None, :] > 0)
    next_expert = jnp.min(jnp.where(later, experts[None, :], N_EXPERTS), axis=1)
    cnext = lookup(jnp.where(next_expert < N_EXPERTS, next_expert, -1))
    cnext = jnp.where(is_first, cnext, -1)
    dest = _dest(eidx, rank, starts.astype(F32).reshape(N_EXPERTS, 1)).reshape(tokens * TOP_K)

    xs = _dispatch(dest, rows3)
    n_steps = (n_used + CHUNKS_PER_STEP - 1) // CHUNKS_PER_STEP
    ys = _experts(cexp, i32(crow), i32(cn), i32(cnew), i32(cnext), i32(cw),
                  i32(n_steps).reshape(1), xs, w_e_gate, w_e_up, w_e_down, tokens * TOP_K)
    return _combine(dest, ys, gate.T, mid, wsg, wsu, wsd, g2, b2, alpha=alpha, rows_a=rows_a)


def kernel(x_prompt, x_sample, cache_k, cache_v, cache_logf, state_conv, w_in, b_in, conv_w,
           conv_b, conv_ln_g, conv_ln_b, w_a, w_b, b_b, w_out, ln1_g, ln1_b, w_router, b_router,
           w_e_gate, w_e_up, w_e_down, w_s_gate, w_s_up, w_s_down, ln2_g, ln2_b):
    depth = w_in.shape[0]
    alpha = float((2 * depth) ** 0.25)
    batch, seq, _ = x_prompt.shape
    dbatch, dseq, _ = x_sample.shape
    past = cache_k.shape[2]
    rows_p, rows_s = batch * seq, dbatch * dseq
    total = rows_p + rows_s
    assert seq % IN_TILE == 0 and rows_s % IN_TILE == 0 and seq % MERGE_TILE == 0
    assert total % ROUTER_TILE == 0 and dseq == HIST and rows_p % dseq == 0
    assert total % DEST_TILE == 0 and (total * TOP_K) % EXPERT_BLOCK == 0

    hp = x_prompt.reshape(rows_p, D_MODEL)
    hs = x_sample.reshape(rows_s, D_MODEL)
    tri_in = _tri(IN_TILE, lower=True)
    upper_past = _tri(past, lower=False)
    row2 = lambda a: a.reshape(1, -1)
    outs = {n: [] for n in ("kp", "vp", "fp", "cp", "ks", "vs", "fs", "cs")}

    for l in range(depth):
        w = w_in[l]
        b = b_in[l]
        main_cols = lambda a: jnp.concatenate([a[..., :OFF_F], a[..., OFF_GLU:]], axis=-1)
        w_main = main_cols(w).astype(BF16)
        b_main = row2(main_cols(b))
        w_f = jnp.pad(w[:, OFF_F:OFF_GLU], ((0, 0), (0, LANES - FOX_HEADS))).astype(BF16)
        b_f = row2(jnp.pad(b[OFF_F:OFF_GLU], (0, LANES - FOX_HEADS)))
        cw = jnp.pad(conv_w[l], ((0, 1), (0, 0)))
        conv_p = (cw, row2(conv_b[l]), row2(conv_ln_g[l]), row2(conv_ln_b[l]),
                  w_b[l].astype(BF16), row2(b_b[l]))
        wa, wout = w_a[l].astype(BF16), w_out[l].astype(BF16)
        g1, b1 = row2(ln1_g[l]), row2(ln1_b[l])

        q, k, v, kb, vb, logf, c, u, sa, sb = _inproj(
            hp, w_main, b_main, w_f, b_f, tri_in, tiles_per_seq=seq // IN_TILE)
        c_row = c.reshape(batch, seq, FOX_HEADS).transpose(0, 2, 1)
        attn = _attn_prompt(q, kb, vb, c_row, batch=batch, seq=seq)
        mid, rows3 = _merge(u, u, attn, sa, sb, hp, conv_p, wa, wout, g1, b1,
                            n_seq=batch, seq=seq, ts=MERGE_TILE, hist_from_u=True, alpha=alpha,
                            total_rows=total, row_offset=0)
        outs["kp"].append(k.reshape(batch, seq, FOX_HEADS, HEAD_DIM))
        outs["vp"].append(v.reshape(batch, seq, FOX_HEADS, HEAD_DIM))
        outs["fp"].append(logf.reshape(batch, seq, FOX_HEADS))
        outs["cp"].append(u.reshape(batch, seq, CONV_CH)[:, seq - (CONV_WIDTH - 1):])

        q, k, v, kb, vb, logf, _, u, sa, sb = _inproj(
            hs, w_main, b_main, w_f, b_f, tri_in, tiles_per_seq=1)
        logf_t = logf.reshape(dbatch, dseq, FOX_HEADS).transpose(0, 2, 1)
        attn = _attn_sample(
            q, kb, vb, logf, logf_t, cache_k[l].reshape(dbatch, past, FOX_WIDTH),
            cache_v[l].reshape(dbatch, past, FOX_WIDTH), cache_logf[l].transpose(0, 2, 1),
            upper_past, batch=dbatch, t=dseq, past=past)
        hist = jnp.pad(state_conv[l], ((0, 0), (HIST - (CONV_WIDTH - 1), 0), (0, 0)))
        mid, rows3 = _merge(u, hist.reshape(dbatch * HIST, CONV_CH), attn, sa, sb, hs, conv_p,
                            wa, wout, g1, b1, n_seq=dbatch, seq=dseq, ts=dseq, hist_from_u=False,
                            alpha=alpha, total_rows=total, row_offset=rows_p, prev=(mid, rows3))
        outs["ks"].append(k.reshape(dbatch, dseq, FOX_HEADS, HEAD_DIM))
        outs["vs"].append(v.reshape(dbatch, dseq, FOX_HEADS, HEAD_DIM))
        outs["fs"].append(logf.reshape(dbatch, dseq, FOX_HEADS))
        u3 = u.reshape(dbatch, dseq, CONV_CH)
        u_ext = jnp.concatenate([state_conv[l], u3], axis=1)
        outs["cs"].append(u_ext[:, -(CONV_WIDTH - 1):])

        hp, hs = _moe(mid, rows3, w_router[l], b_router[l], w_e_gate[l], w_e_up[l], w_e_down[l],
                      w_s_gate[l].astype(BF16), w_s_up[l].astype(BF16), w_s_down[l].astype(BF16),
                      row2(ln2_g[l]), row2(ln2_b[l]), alpha, rows_p)

    st = lambda n: jnp.stack(outs[n])
    return (hp.reshape(batch, seq, D_MODEL), hs.reshape(dbatch, dseq, D_MODEL),
            st("kp"), st("vp"), st("fp"), st("cp"), st("ks"), st("vs"), st("fs"), st("cs"))
```

```python
import functools

import jax
import jax.numpy as jnp
from jax import lax
from jax.experimental import pallas as pl
from jax.experimental.pallas import tpu as pltpu

D_MODEL = 1024
FOX_HEADS = 8
HEAD_DIM = 64
FOX_WIDTH = FOX_HEADS * HEAD_DIM
ATTN_SCALE = HEAD_DIM ** -0.5
LOG2E = 1.4426950408889634
CONV_CH = D_MODEL // 2
CONV_WIDTH = 31
N_EXPERTS = 256
TOP_K = 8
N_GROUPS = 8
GROUP_SIZE = N_EXPERTS // N_GROUPS
TOPK_GROUPS = 4
D_EXPERT = D_MODEL // 4
ROUTED_SCALE = 2.5
LN_EPS = 1e-5

OFF_K = FOX_WIDTH
OFF_V = 2 * FOX_WIDTH
OFF_F = 3 * FOX_WIDTH
OFF_GLU = OFF_F + FOX_HEADS
OFF_GA = OFF_GLU + 2 * CONV_CH
OFF_GB = OFF_GA + D_MODEL

LANES = 128
SUBLANES = 8
ROW_TILES = D_MODEL // LANES
VMEM_LIMIT = 56 * 1024 * 1024

IN_TILE = 512
ATTN_TILE = 256
MERGE_TILE = 256
CONV_SUB = 32
HIST = 32
ROUTER_TILE = 512
MOVE_TILE = 128
DEST_TILE = 1024
EXPERT_BLOCK = 256

F32 = jnp.float32
BF16 = jnp.bfloat16
NEG_INF = float("-inf")
NT_DIMS = (((1,), (1,)), ((), ()))


def _const_spec(shape):
    nd = len(shape)
    return pl.BlockSpec(shape, lambda *_: (0,) * nd, pipeline_mode=pl.Buffered(1))


def _split3(x):
    hi = x.astype(BF16)
    r1 = x - hi.astype(F32)
    mid = r1.astype(BF16)
    lo = (r1 - mid.astype(F32)).astype(BF16)
    return hi, mid, lo


def _dot(a, b):
    return jnp.dot(a, b, preferred_element_type=F32)


def _dot_nt(a, b):
    return lax.dot_general(a, b, NT_DIMS, preferred_element_type=F32)


def _exact_dot(ones_mat, x, *, ones_on_left):
    acc = None
    for part in _split3(x):
        term = _dot(ones_mat, part) if ones_on_left else _dot(part, ones_mat)
        acc = term if acc is None else acc + term
    return acc


def _layer_norm(x, g, b):
    mu = jnp.mean(x, axis=-1, keepdims=True)
    xc = x - mu
    var = jnp.mean(xc * xc, axis=-1, keepdims=True)
    return xc * lax.rsqrt(var + LN_EPS) * g + b


def _log_sigmoid(z):
    return jnp.minimum(z, 0.0) - jnp.log1p(jnp.exp(-jnp.abs(z)))


def _inproj_kernel(x_ref, w_ref, b_ref, wf_ref, bf_ref, tri_ref,
                   q_ref, k_ref, v_ref, kb_ref, vb_ref, logf_ref, c_ref, u_ref,
                   sa_ref, sb_ref, carry_ref, *, tiles_per_seq):
    i = pl.program_id(0)
    xb = x_ref[...].astype(BF16)

    def proj(c0, c1):
        return _dot(xb, w_ref[:, c0:c1]) + b_ref[:, c0:c1]

    q = proj(0, FOX_WIDTH)
    q_ref[...] = (q * (ATTN_SCALE * LOG2E)).astype(BF16)
    k = proj(FOX_WIDTH, 2 * FOX_WIDTH)
    kb_ref[...] = k.astype(BF16)
    v = proj(2 * FOX_WIDTH, 3 * FOX_WIDTH)
    vb_ref[...] = v.astype(BF16)
    for h in range(FOX_HEADS):
        head_rows = pl.ds(h, IN_TILE, stride=FOX_HEADS)
        k_ref[head_rows, :] = k[:, h * HEAD_DIM:(h + 1) * HEAD_DIM]
        v_ref[head_rows, :] = v[:, h * HEAD_DIM:(h + 1) * HEAD_DIM]

    logf = _log_sigmoid(_dot(xb, wf_ref[...]) + bf_ref[...])
    logf_ref[...] = logf[:, :FOX_HEADS]

    @pl.when(i % tiles_per_seq == 0)
    def _():
        carry_ref[...] = jnp.zeros_like(carry_ref)

    c = _exact_dot(tri_ref[...], logf, ones_on_left=True) + carry_ref[...]
    c_ref[...] = c[:, :FOX_HEADS]
    carry_ref[...] = c[IN_TILE - 1:IN_TILE, :]

    g0 = 3 * FOX_WIDTH
    glu_a = proj(g0, g0 + CONV_CH)
    glu_b = proj(g0 + CONV_CH, g0 + 2 * CONV_CH)
    u_ref[...] = glu_a * jax.nn.sigmoid(glu_b)
    g1 = g0 + 2 * CONV_CH
    sa_ref[...] = jax.nn.sigmoid(proj(g1, g1 + D_MODEL)).astype(BF16)
    sb_ref[...] = jax.nn.sigmoid(proj(g1 + D_MODEL, g1 + 2 * D_MODEL)).astype(BF16)


def _inproj(x, w_main, b_main, w_f, b_f, tri, *, tiles_per_seq):
    rows = x.shape[0]
    n_main = w_main.shape[1]
    row_spec = lambda w: pl.BlockSpec((IN_TILE, w), lambda i: (i, 0))
    out_shapes = (
        jax.ShapeDtypeStruct((rows, FOX_WIDTH), BF16),
        jax.ShapeDtypeStruct((rows * FOX_HEADS, HEAD_DIM), F32),
        jax.ShapeDtypeStruct((rows * FOX_HEADS, HEAD_DIM), F32),
        jax.ShapeDtypeStruct((rows, FOX_WIDTH), BF16),
        jax.ShapeDtypeStruct((rows, FOX_WIDTH), BF16),
        jax.ShapeDtypeStruct((rows, FOX_HEADS), F32),
        jax.ShapeDtypeStruct((rows, FOX_HEADS), F32),
        jax.ShapeDtypeStruct((rows, CONV_CH), F32),
        jax.ShapeDtypeStruct((rows, D_MODEL), BF16),
        jax.ShapeDtypeStruct((rows, D_MODEL), BF16),
    )
    head_spec = pl.BlockSpec((IN_TILE * FOX_HEADS, HEAD_DIM), lambda i: (i, 0))
    out_specs = (row_spec(FOX_WIDTH), head_spec, head_spec) + (row_spec(FOX_WIDTH),) * 2 + (
        row_spec(FOX_HEADS),) * 2 + (row_spec(CONV_CH), row_spec(D_MODEL), row_spec(D_MODEL))
    return pl.pallas_call(
        functools.partial(_inproj_kernel, tiles_per_seq=tiles_per_seq),
        grid=(rows // IN_TILE,),
        in_specs=[row_spec(D_MODEL), _const_spec((D_MODEL, n_main)), _const_spec((1, n_main)),
                  _const_spec((D_MODEL, LANES)), _const_spec((1, LANES)),
                  _const_spec((IN_TILE, IN_TILE))],
        out_specs=out_specs,
        out_shape=out_shapes,
        scratch_shapes=[pltpu.VMEM((1, LANES), F32)],
        compiler_params=pltpu.CompilerParams(
            dimension_semantics=("arbitrary",), vmem_limit_bytes=VMEM_LIMIT),
        name="inproj",
    )(x, w_main, b_main, w_f, b_f, tri)


def _attn_prompt_kernel(q_ref, k_ref, v_ref, crow_ref, o_ref, qm_scr, m_scr, acc_scr):
    t = ATTN_TILE
    i = pl.program_id(1)
    lane = lax.broadcasted_iota(jnp.int32, (t, LANES), 1)
    row = lax.broadcasted_iota(jnp.int32, (t, t), 0)
    col = lax.broadcasted_iota(jnp.int32, (t, t), 1)
    causal = col <= row
    wide = lambda x: jnp.concatenate([x] * (t // LANES), axis=1)
    pair_lanes = lambda pair: slice(pair * LANES, (pair + 1) * LANES)

    for h in range(FOX_HEADS):
        qp = q_ref[:, pair_lanes(h // 2)]
        in_head = (lane < HEAD_DIM) if h % 2 == 0 else (lane >= HEAD_DIM)
        qm_scr[h] = jnp.where(in_head, qp, jnp.zeros_like(qp))
        m_scr[h] = jnp.full((t, LANES), NEG_INF, F32)
        acc_scr[h] = jnp.zeros((t, 2 * LANES), F32)
    ones = jnp.ones((t, LANES), BF16)

    def step(j, masked):
        r0 = pl.multiple_of(j * t, t)
        ck = crow_ref[0, :, pl.ds(r0, t)] * LOG2E
        for pair in range(FOX_HEADS // 2):
            kj = k_ref[pl.ds(r0, t), pair_lanes(pair)]
            vj = jnp.concatenate([v_ref[pl.ds(r0, t), pair_lanes(pair)], ones], axis=1)
            for h in (2 * pair, 2 * pair + 1):
                s = _dot_nt(qm_scr[h], kj) - ck[h:h + 1, :]
                if masked:
                    s = jnp.where(causal, s, NEG_INF)
                m_prev = m_scr[h]
                m_new = jnp.maximum(m_prev, jnp.max(s, axis=1, keepdims=True))
                alpha = jnp.exp2(m_prev - m_new)
                p = jnp.exp2(s - wide(m_new))
                acc_scr[h] = wide(alpha) * acc_scr[h] + _dot(p.astype(BF16), vj)
                m_scr[h] = m_new

    def body(jj, carry):
        step(2 * jj, False)
        step(2 * jj + 1, False)
        return carry

    lax.fori_loop(0, i // 2, body, 0)

    @pl.when(i % 2 == 1)
    def _():
        step(i - 1, False)

    step(i, True)
    for pair in range(FOX_HEADS // 2):
        o0, o1 = (acc_scr[h, :, :LANES] / acc_scr[h, :, LANES:] for h in (2 * pair, 2 * pair + 1))
        o_ref[:, pair_lanes(pair)] = jnp.where(lane < HEAD_DIM, o0, o1).astype(BF16)


def _attn_prompt(q, kb, vb, c_row, *, batch, seq):
    t = ATTN_TILE
    nq = seq // t
    return pl.pallas_call(
        _attn_prompt_kernel,
        grid=(batch, nq),
        in_specs=[
            pl.BlockSpec((t, FOX_WIDTH), lambda b, i: (b * nq + i, 0)),
            pl.BlockSpec((seq, FOX_WIDTH), lambda b, i: (b, 0)),
            pl.BlockSpec((seq, FOX_WIDTH), lambda b, i: (b, 0)),
            pl.BlockSpec((1, FOX_HEADS, seq), lambda b, i: (b, 0, 0)),
        ],
        out_specs=pl.BlockSpec((t, FOX_WIDTH), lambda b, i: (b * nq + i, 0)),
        out_shape=jax.ShapeDtypeStruct((batch * seq, FOX_WIDTH), BF16),
        scratch_shapes=[pltpu.VMEM((FOX_HEADS, t, LANES), BF16),
                        pltpu.VMEM((FOX_HEADS, t, LANES), F32),
                        pltpu.VMEM((FOX_HEADS, t, 2 * LANES), F32)],
        compiler_params=pltpu.CompilerParams(
            dimension_semantics=("arbitrary", "arbitrary"), vmem_limit_bytes=VMEM_LIMIT),
        name="attn_prompt",
    )(q, kb, vb, c_row)


def _attn_sample_kernel(q_ref, kn_ref, vn_ref, lf_ref, lft_ref, ck_ref, cv_ref, clft_ref,
                        upper_ref, o_ref, *, t, past):
    rows = FOX_HEADS * t
    lane_head = lax.broadcasted_iota(jnp.int32, (t, FOX_WIDTH), 1) // HEAD_DIM
    q = q_ref[...]
    q_stack = jnp.concatenate(
        [jnp.where(lane_head == h, q, jnp.zeros_like(q)) for h in range(FOX_HEADS)], axis=0)

    clf = clft_ref[0]
    prefix = _exact_dot(upper_ref[...], clf, ones_on_left=False)
    to_end = prefix[:, past - 1:past] - prefix
    ri = lax.broadcasted_iota(jnp.int32, (t, t), 0)
    ci = lax.broadcasted_iota(jnp.int32, (t, t), 1)
    lower = jnp.where(ci <= ri, 1.0, 0.0).astype(BF16)
    upper = jnp.where(ri <= ci, 1.0, 0.0).astype(BF16)
    cn_col = _exact_dot(lower, lf_ref[...], ones_on_left=True)
    cn_row = _exact_dot(upper, lft_ref[0], ones_on_left=False)

    stack = lambda f: jnp.concatenate([f(h) for h in range(FOX_HEADS)], axis=0)
    cn_stack = stack(lambda h: cn_col[:, h:h + 1])
    bias_c = stack(lambda h: jnp.broadcast_to(to_end[h:h + 1, :], (t, past)))
    bias_n = stack(lambda h: jnp.broadcast_to(cn_row[h:h + 1, :], (t, t)))

    kc = ck_ref[0].astype(BF16)
    vc = cv_ref[0].astype(BF16)
    s_c = _dot_nt(q_stack, kc) + (bias_c + cn_stack) * LOG2E
    s_n = _dot_nt(q_stack, kn_ref[...]) + (cn_stack - bias_n) * LOG2E
    tq = lax.broadcasted_iota(jnp.int32, (rows, t), 0) % t
    tk = lax.broadcasted_iota(jnp.int32, (rows, t), 1)
    s_n = jnp.where(tk <= tq, s_n, NEG_INF)
    m = jnp.maximum(jnp.max(s_c, axis=1, keepdims=True), jnp.max(s_n, axis=1, keepdims=True))
    p_c = jnp.exp2(s_c - m)
    p_n = jnp.exp2(s_n - m)
    denom = jnp.sum(p_c, axis=1, keepdims=True) + jnp.sum(p_n, axis=1, keepdims=True)
    o_stack = (_dot(p_c.astype(BF16), vc) + _dot(p_n.astype(BF16), vn_ref[...])) / denom
    out = jnp.zeros((t, FOX_WIDTH), F32)
    for h in range(FOX_HEADS):
        out = out + jnp.where(lane_head == h, o_stack[h * t:(h + 1) * t, :], 0.0)
    o_ref[...] = out.astype(BF16)


def _attn_sample(q, kb, vb, logf, logf_t, cache_k, cache_v, cache_logf_t, upper, *, batch, t, past):
    row_spec = lambda w: pl.BlockSpec((t, w), lambda b: (b, 0))
    return pl.pallas_call(
        functools.partial(_attn_sample_kernel, t=t, past=past),
        grid=(batch,),
        in_specs=[
            row_spec(FOX_WIDTH), row_spec(FOX_WIDTH), row_spec(FOX_WIDTH), row_spec(FOX_HEADS),
            pl.BlockSpec((1, FOX_HEADS, t), lambda b: (b, 0, 0)),
            pl.BlockSpec((1, past, FOX_WIDTH), lambda b: (b, 0, 0)),
            pl.BlockSpec((1, past, FOX_WIDTH), lambda b: (b, 0, 0)),
            pl.BlockSpec((1, FOX_HEADS, past), lambda b: (b, 0, 0)),
            _const_spec((past, past)),
        ],
        out_specs=row_spec(FOX_WIDTH),
        out_shape=jax.ShapeDtypeStruct((batch * t, FOX_WIDTH), BF16),
        compiler_params=pltpu.CompilerParams(
            dimension_semantics=("arbitrary",), vmem_limit_bytes=VMEM_LIMIT),
        name="attn_sample",
    )(q, kb, vb, logf, logf_t, cache_k, cache_v, cache_logf_t, upper)


def _merge_kernel(u_ref, hist_ref, attn_ref, sa_ref, sb_ref, x_ref,
                  cw_ref, cb_ref, cg_ref, cbeta_ref, wb_ref, bb_ref, wa_ref, wout_ref,
                  g1_ref, b1_ref, *rest, ts, tiles_per_seq, n_tiles, zero_first_hist, alpha,
                  aliased):
    if aliased:
        rest = rest[2:]
    mid_ref, mid3_ref, ue_scr, ph_scr, h_scr = rest
    i = pl.program_id(0)

    @pl.when(i >= n_tiles)
    def _():
        mid_ref[...] = jnp.zeros_like(mid_ref)
        mid3_ref[...] = jnp.zeros_like(mid3_ref)

    @pl.when(i < n_tiles)
    def _():
        _merge_tile(u_ref, hist_ref, attn_ref, sa_ref, sb_ref, x_ref, cw_ref, cb_ref, cg_ref,
                    cbeta_ref, wb_ref, bb_ref, wa_ref, wout_ref, g1_ref, b1_ref, mid_ref,
                    mid3_ref, ue_scr, ph_scr, h_scr, ts=ts, alpha=alpha,
                    zero_hist=(i % tiles_per_seq == 0) if zero_first_hist else None)


def _merge_tile(u_ref, hist_ref, attn_ref, sa_ref, sb_ref, x_ref, cw_ref, cb_ref, cg_ref,
                cbeta_ref, wb_ref, bb_ref, wa_ref, wout_ref, g1_ref, b1_ref, mid_ref, mid3_ref,
                ue_scr, ph_scr, h_scr, *, ts, alpha, zero_hist):
    hist = hist_ref[...]
    if zero_hist is not None:
        hist = jnp.where(zero_hist, 0.0, hist)
    ue_scr[0:HIST, :] = hist
    ue_scr[HIST:HIST + ts, :] = u_ref[...]
    lead = HIST - (CONV_WIDTH - 1)
    for r in range(SUBLANES):
        span = ts + (CONV_WIDTH - 1 - r) // SUBLANES * SUBLANES
        ph_scr[r, 0:span, :] = ue_scr[lead + r:lead + r + span, :]

    def conv_rows(rt, carry):
        base = pl.multiple_of(rt * CONV_SUB, CONV_SUB)
        acc = jnp.zeros((CONV_SUB, CONV_CH), F32)
        for j in range(CONV_WIDTH):
            r, a = j % SUBLANES, j // SUBLANES
            acc = acc + cw_ref[j:j + 1, :] * ph_scr[r, pl.ds(base + a * SUBLANES, CONV_SUB), :]
        h_scr[pl.ds(base, CONV_SUB), :] = acc
        return carry

    lax.fori_loop(0, ts // CONV_SUB, conv_rows, 0)
    h = _layer_norm(h_scr[...] + cb_ref[...], cg_ref[...], cbeta_ref[...])
    h = h * jax.nn.sigmoid(h)
    conv_out = _dot(h.astype(BF16), wb_ref[...]) + bb_ref[...]
    attn_out = _dot(attn_ref[...], wa_ref[...])
    m = sa_ref[...].astype(F32) * attn_out + sb_ref[...].astype(F32) * conv_out
    z = alpha * x_ref[...] + _dot(m.astype(BF16), wout_ref[...])
    mid = _layer_norm(z, g1_ref[...], b1_ref[...])
    mid_ref[...] = mid
    for j in range(ROW_TILES):
        mid3_ref[pl.ds(j, ts, stride=ROW_TILES), :] = mid[:, j * LANES:(j + 1) * LANES]


def _merge(u, hist, attn, sa, sb, x, conv_p, wa, wout, g1, b1, *, n_seq, seq, ts, hist_from_u,
           alpha, total_rows, row_offset, prev=None):
    nt = seq // ts
    n_tiles = n_seq * nt
    off = row_offset // ts
    grid_tiles = n_tiles if prev is not None else total_rows // ts
    src = lambda i: jnp.minimum(i, n_tiles - 1)
    row_spec = lambda w: pl.BlockSpec((ts, w), lambda i: (src(i), 0))
    if hist_from_u:
        per = ts // HIST
        hist_spec = pl.BlockSpec((HIST, CONV_CH), lambda i: (jnp.maximum(src(i) * per - 1, 0), 0))
    else:
        hist_spec = pl.BlockSpec((HIST, CONV_CH), lambda i: (src(i) // nt, 0))
    cw, cb, cg, cbeta, wb, bb = conv_p
    consts = [cw, cb, cg, cbeta, wb, bb, wa, wout, g1, b1]
    in_specs = [row_spec(CONV_CH), hist_spec, row_spec(FOX_WIDTH), row_spec(D_MODEL),
                row_spec(D_MODEL), row_spec(D_MODEL)] + [_const_spec(c.shape) for c in consts]
    args = [u, hist, attn, sa, sb, x] + consts
    aliases = {}
    if prev is not None:
        in_specs += [pl.BlockSpec(memory_space=pl.ANY)] * 2
        aliases = {len(args): 0, len(args) + 1: 1}
        args += list(prev)
    span = ts + (CONV_WIDTH - 1) // SUBLANES * SUBLANES
    return pl.pallas_call(
        functools.partial(_merge_kernel, ts=ts, tiles_per_seq=nt, n_tiles=n_tiles,
                          zero_first_hist=hist_from_u, alpha=alpha, aliased=prev is not None),
        grid=(grid_tiles,),
        in_specs=in_specs,
        out_specs=(pl.BlockSpec((ts, D_MODEL), lambda i: (off + i, 0)),
                   pl.BlockSpec((ts * ROW_TILES, LANES), lambda i: (off + i, 0))),
        out_shape=(jax.ShapeDtypeStruct((total_rows, D_MODEL), F32),
                   jax.ShapeDtypeStruct((total_rows * ROW_TILES, LANES), F32)),
        scratch_shapes=[pltpu.VMEM((HIST + ts, CONV_CH), F32),
                        pltpu.VMEM((SUBLANES, span, CONV_CH), F32),
                        pltpu.VMEM((ts, CONV_CH), F32)],
        input_output_aliases=aliases,
        compiler_params=pltpu.CompilerParams(
            dimension_semantics=("arbitrary",), vmem_limit_bytes=VMEM_LIMIT),
        name="merge",
    )(*args)


def _router_kernel(mid_ref, wr_hi_ref, wr_lo_ref, br_ref, before_ref,
                   eidx_ref, gate_ref, rank_ref, cnt_ref, carry_scr):
    tr = ROUTER_TILE
    i = pl.program_id(0)

    @pl.when(i == 0)
    def _():
        carry_scr[...] = jnp.zeros_like(carry_scr)

    x = mid_ref[...]
    x_hi = x.astype(BF16)
    x_lo = (x - x_hi.astype(F32)).astype(BF16)
    wr_hi = wr_hi_ref[...]
    logits = _dot_nt(wr_hi, x_hi) + _dot_nt(wr_hi, x_lo) + _dot_nt(wr_lo_ref[...], x_hi)
    scores = jax.nn.sigmoid(logits)
    sel = scores + br_ref[...]

    sel3 = sel.reshape(N_GROUPS, GROUP_SIZE, tr)
    in_group = lax.broadcasted_iota(jnp.int32, sel3.shape, 1)
    m1 = jnp.max(sel3, axis=1, keepdims=True)
    first = jnp.min(jnp.where(sel3 == m1, in_group, GROUP_SIZE), axis=1, keepdims=True)
    m2 = jnp.max(jnp.where(in_group == first, NEG_INF, sel3), axis=1, keepdims=True)
    gs = m1 + m2
    gi = lax.broadcasted_iota(jnp.int32, gs.shape, 0)
    beaten = jnp.zeros(gs.shape, F32)
    for g in range(N_GROUPS):
        other = gs[g:g + 1]
        wins = (other > gs) | ((other == gs) & (g < gi))
        beaten = beaten + jnp.where(wins, 1.0, 0.0)
    drop = jnp.where(beaten < TOPK_GROUPS, 0.0, NEG_INF)
    cur = (sel3 + drop).reshape(N_EXPERTS, tr)

    ei = lax.broadcasted_iota(jnp.int32, (N_EXPERTS, tr), 0)
    idxs, vals = [], []
    picked = jnp.zeros((N_EXPERTS, tr), F32)
    for _ in range(TOP_K):
        m = jnp.max(cur, axis=0, keepdims=True)
        idx = jnp.min(jnp.where(cur == m, ei, N_EXPERTS), axis=0, keepdims=True)
        hit = ei == idx
        vals.append(jnp.sum(jnp.where(hit, scores, 0.0), axis=0, keepdims=True))
        idxs.append(idx)
        picked = picked + jnp.where(hit, 1.0, 0.0)
        cur = jnp.where(hit, NEG_INF, cur)

    total = vals[0]
    for v in vals[1:]:
        total = total + v
    for k in range(TOP_K):
        gate_ref[k:k + 1, :] = vals[k] / total * ROUTED_SCALE
        eidx_ref[k:k + 1, :] = idxs[k]

    ahead = _dot(picked.astype(BF16), before_ref[...]) + carry_scr[:, 0:1]
    for k in range(TOP_K):
        rank = jnp.sum(jnp.where(ei == idxs[k], ahead, 0.0), axis=0, keepdims=True)
        rank_ref[k:k + 1, :] = rank.astype(jnp.int32)
    carry_scr[...] = carry_scr[...] + jnp.sum(picked, axis=1, keepdims=True)
    cnt_ref[...] = carry_scr[...]


def _router(mid, wr_hi, wr_lo, br, before):
    tr = ROUTER_TILE
    tokens = mid.shape[0]
    tok_spec = pl.BlockSpec((TOP_K, tr), lambda i: (0, i))
    return pl.pallas_call(
        _router_kernel,
        grid=(tokens // tr,),
        in_specs=[pl.BlockSpec((tr, D_MODEL), lambda i: (i, 0)),
                  _const_spec((N_EXPERTS, D_MODEL)), _const_spec((N_EXPERTS, D_MODEL)),
                  _const_spec((N_EXPERTS, 1)), _const_spec((tr, tr))],
        out_specs=(tok_spec, tok_spec, tok_spec,
                   pl.BlockSpec((N_EXPERTS, LANES), lambda i: (0, 0))),
        out_shape=(jax.ShapeDtypeStruct((TOP_K, tokens), jnp.int32),
                   jax.ShapeDtypeStruct((TOP_K, tokens), F32),
                   jax.ShapeDtypeStruct((TOP_K, tokens), jnp.int32),
                   jax.ShapeDtypeStruct((N_EXPERTS, LANES), F32)),
        scratch_shapes=[pltpu.VMEM((N_EXPERTS, LANES), F32)],
        compiler_params=pltpu.CompilerParams(
            dimension_semantics=("arbitrary",), vmem_limit_bytes=VMEM_LIMIT),
        name="router",
    )(mid, wr_hi, wr_lo, br, before)


def _dest_kernel(eidx_ref, rank_ref, starts_ref, dest_ref):
    tt = MOVE_TILE
    tokens = eidx_ref.shape[1]
    ei = lax.broadcasted_iota(jnp.int32, (N_EXPERTS, tokens), 0)
    starts = starts_ref[...]
    for k in range(TOP_K):
        hit = ei == eidx_ref[k:k + 1, :]
        start = jnp.sum(jnp.where(hit, starts, 0.0), axis=0, keepdims=True)
        dest = start.astype(jnp.int32) + rank_ref[k:k + 1, :]
        for c in range(tokens // tt):
            dest_ref[c, k:k + 1, :] = dest[:, c * tt:(c + 1) * tt]


def _dest(eidx, rank, starts):
    tt = MOVE_TILE
    tokens = eidx.shape[1]
    step = DEST_TILE
    tok_spec = pl.BlockSpec((TOP_K, step), lambda i: (0, i))
    return pl.pallas_call(
        _dest_kernel,
        grid=(tokens // step,),
        in_specs=[tok_spec, tok_spec, _const_spec((N_EXPERTS, 1))],
        out_specs=pl.BlockSpec((step // tt, TOP_K, tt), lambda i: (i, 0, 0)),
        out_shape=jax.ShapeDtypeStruct((tokens // tt, TOP_K, tt), jnp.int32),
        compiler_params=pltpu.CompilerParams(dimension_semantics=("arbitrary",)),
        name="dest",
    )(eidx, rank, starts)


def _row_slice(ref, row):
    return ref.at[pl.ds(pl.multiple_of(row * ROW_TILES, ROW_TILES), ROW_TILES), :]


def _dispatch_kernel(dest_ref, rows_ref, xs_ref, zeros_scr, sem, *, n_rows):
    tt = MOVE_TILE

    @pl.when(pl.program_id(0) == 0)
    def _():
        zeros_scr[...] = jnp.zeros_like(zeros_scr)
        tail = pltpu.make_async_copy(
            zeros_scr, xs_ref.at[pl.ds(n_rows * ROW_TILES, EXPERT_BLOCK * ROW_TILES), :], sem)
        tail.start()
        tail.wait()

    def issue(t, carry):
        src = _row_slice(rows_ref, t)
        for k in range(TOP_K):
            pltpu.make_async_copy(src, _row_slice(xs_ref, dest_ref[k * tt + t]), sem).start(
                priority=k % 2)
        return carry

    lax.fori_loop(0, tt, issue, 0)
    for _ in range(TOP_K):
        pltpu.make_async_copy(rows_ref, xs_ref.at[pl.ds(0, tt * ROW_TILES), :], sem).wait()


def _dispatch(dest_flat, rows3):
    tt = MOVE_TILE
    tokens = rows3.shape[0] // ROW_TILES
    n_rows = tokens * TOP_K
    return pl.pallas_call(
        functools.partial(_dispatch_kernel, n_rows=n_rows),
        grid=(tokens // tt,),
        in_specs=[pl.BlockSpec((tt * TOP_K,), lambda i: (i,), memory_space=pltpu.SMEM),
                  pl.BlockSpec((tt * ROW_TILES, LANES), lambda i: (i, 0))],
        out_specs=pl.BlockSpec(memory_space=pl.ANY),
        out_shape=jax.ShapeDtypeStruct(((n_rows + EXPERT_BLOCK) * ROW_TILES, LANES), F32),
        scratch_shapes=[pltpu.VMEM((EXPERT_BLOCK * ROW_TILES, LANES), F32),
                        pltpu.SemaphoreType.DMA(())],
        compiler_params=pltpu.CompilerParams(dimension_semantics=("arbitrary",)),
        name="dispatch",
    )(dest_flat, rows3)


_PIECES = tuple(EXPERT_BLOCK >> s for s in range(EXPERT_BLOCK.bit_length()))
CHUNKS_PER_STEP = 2
ROWS_AHEAD = 6
ROW_SLOTS = ROWS_AHEAD + CHUNKS_PER_STEP
OUT_SLOTS = 2 * CHUNKS_PER_STEP


def _expert_kernel(cexp_ref, crow_ref, cn_ref, cnew_ref, cnext_ref, cw_ref, nsteps_ref,
                   xs_hbm, wg_hbm, wu_hbm, wd_hbm, ys_hbm,
                   xbuf, ybuf, wg_buf, wu_buf, wd_buf, wgb, wub, wdb, xsem, ysem, wsem):
    bm = EXPERT_BLOCK
    s = pl.program_id(0)
    n_steps = nsteps_ref[0]
    n_chunks = n_steps * CHUNKS_PER_STEP
    chunks = [s * CHUNKS_PER_STEP + c for c in range(CHUNKS_PER_STEP)]

    def weight_copies(e, s):
        return [pltpu.make_async_copy(hbm.at[e], buf.at[s], wsem.at[s])
                for hbm, buf in ((wg_hbm, wg_buf), (wu_hbm, wu_buf), (wd_hbm, wd_buf))]

    def rows_in(j, s):
        first = pl.multiple_of(crow_ref[j] * ROW_TILES, ROW_TILES)
        return pltpu.make_async_copy(xs_hbm.at[pl.ds(first, bm * ROW_TILES), :], xbuf.at[s],
                                     xsem.at[s])

    def rows_out(j, s, act):
        n, row0 = cn_ref[j], crow_ref[j]

        def piece(done, p):
            src = pl.multiple_of(done * ROW_TILES, ROW_TILES)
            dst = pl.multiple_of((row0 + done) * ROW_TILES, ROW_TILES)
            act(pltpu.make_async_copy(ybuf.at[s, pl.ds(src, p * ROW_TILES), :],
                                      ys_hbm.at[pl.ds(dst, p * ROW_TILES), :], ysem.at[s]))

        @pl.when(n == bm)
        def _():
            piece(jnp.int32(0), bm)

        @pl.when(n < bm)
        def _():
            done = jnp.int32(0)
            for p in _PIECES[1:]:
                has = (n & p) != 0

                @pl.when(has)
                def _(done=done, p=p):
                    piece(done, p)

                done = done + jnp.where(has, p, 0)

    start = lambda c: c.start()
    wait = lambda c: c.wait()

    @pl.when(s == 0)
    def _():
        for d in range(ROWS_AHEAD):
            @pl.when(d < n_chunks)
            def _(d=d):
                rows_in(d, d).start()
        for c in weight_copies(cexp_ref[0], 0):
            c.start()

    @pl.when(s < n_steps)
    def _():
        for j in chunks:
            slot = cnew_ref[j]

            @pl.when(slot >= 0)
            def _(j=j, slot=slot):
                for c in weight_copies(cexp_ref[j], slot):
                    c.wait()
                nxt = cnext_ref[j]

                @pl.when(nxt >= 0)
                def _():
                    for c in weight_copies(nxt, 1 - slot):
                        c.start()

                wgb[slot] = wg_buf[slot].astype(BF16)
                wub[slot] = wu_buf[slot].astype(BF16)
                wdb[slot] = wd_buf[slot].astype(BF16)

        for j in chunks:
            @pl.when(j + ROWS_AHEAD < n_chunks)
            def _(j=j):
                rows_in(j + ROWS_AHEAD, (j + ROWS_AHEAD) % ROW_SLOTS).start()

        for j in chunks:
            rows_in(j, j % ROW_SLOTS).wait()

        for j in chunks:
            @pl.when(j >= OUT_SLOTS)
            def _(j=j):
                rows_out(j - OUT_SLOTS, j % OUT_SLOTS, wait)

        for j in chunks:
            xs_slot, w_slot, y_slot = j % ROW_SLOTS, cw_ref[j], j % OUT_SLOTS
            x = jnp.concatenate(
                [xbuf[xs_slot, pl.ds(r, bm, stride=ROW_TILES), :] for r in range(ROW_TILES)],
                axis=1)
            xb = x.astype(BF16)
            g = _dot(xb, wgb[w_slot])
            u = _dot(xb, wub[w_slot])
            h = (g * jax.nn.sigmoid(g) * u).astype(BF16)
            y = _dot(h, wdb[w_slot])
            for r in range(ROW_TILES):
                ybuf[y_slot, pl.ds(r, bm, stride=ROW_TILES), :] = y[:, r * LANES:(r + 1) * LANES]

        for j in chunks:
            rows_out(j, j % OUT_SLOTS, start)

        @pl.when(s == n_steps - 1)
        def _():
            for d in range(OUT_SLOTS):
                last = n_chunks - 1 - d

                @pl.when(last >= 0)
                def _(last=last):
                    rows_out(last, last % OUT_SLOTS, wait)


def _experts(cexp, crow, cn, cnew, cnext, cw, nsteps, xs, wg, wu, wd, n_rows):
    bm = EXPERT_BLOCK
    any_spec = pl.BlockSpec(memory_space=pl.ANY)
    grid_spec = pltpu.PrefetchScalarGridSpec(
        num_scalar_prefetch=7,
        grid=(cexp.shape[0] // CHUNKS_PER_STEP,),
        in_specs=[any_spec] * 4,
        out_specs=any_spec,
        scratch_shapes=[pltpu.VMEM((ROW_SLOTS, bm * ROW_TILES, LANES), F32),
                        pltpu.VMEM((OUT_SLOTS, bm * ROW_TILES, LANES), F32),
                        pltpu.VMEM((2, D_MODEL, D_EXPERT), F32),
                        pltpu.VMEM((2, D_MODEL, D_EXPERT), F32),
                        pltpu.VMEM((2, D_EXPERT, D_MODEL), F32),
                        pltpu.VMEM((2, D_MODEL, D_EXPERT), BF16),
                        pltpu.VMEM((2, D_MODEL, D_EXPERT), BF16),
                        pltpu.VMEM((2, D_EXPERT, D_MODEL), BF16),
                        pltpu.SemaphoreType.DMA((ROW_SLOTS,)),
                        pltpu.SemaphoreType.DMA((OUT_SLOTS,)),
                        pltpu.SemaphoreType.DMA((2,))],
    )
    return pl.pallas_call(
        _expert_kernel,
        grid_spec=grid_spec,
        out_shape=jax.ShapeDtypeStruct((n_rows * ROW_TILES, LANES), F32),
        compiler_params=pltpu.CompilerParams(
            dimension_semantics=("arbitrary",), vmem_limit_bytes=VMEM_LIMIT),
        name="experts",
    )(cexp, crow, cn, cnew, cnext, cw, nsteps, xs, wg, wu, wd)


GATHER_SLOTS = 4


def _combine_kernel(dest_ref, dest_next_ref, ys_ref, gate_ref, mid_ref, wsg_ref, wsu_ref,
                    wsd_ref, g2_ref, b2_ref, out_a_ref, out_b_ref, *scratch, alpha, steps_a):
    tt = MOVE_TILE
    ns = GATHER_SLOTS
    per_tile = tt * TOP_K
    g = pl.program_id(0)
    ng = pl.num_programs(0)
    bufs, (gate_scr, routed_scr, sem) = scratch[:ns], scratch[ns:]

    def row_copy(slots_ref, tile, b, t, k):
        src = _row_slice(ys_ref, slots_ref[tile * per_tile + k * tt + t])
        if isinstance(t, int):
            dst = bufs[b].at[pl.ds((k * tt + t) * ROW_TILES, ROW_TILES), :]
        else:
            dst = _row_slice(bufs[b], k * tt + t)
        return pltpu.make_async_copy(src, dst, sem.at[b])

    def wait_rows(b):
        pltpu.make_async_copy(ys_ref.at[pl.ds(0, per_tile * ROW_TILES), :], bufs[b],
                              sem.at[b]).wait()

    @pl.when(g == 0)
    def _():
        for r in range(ns - 1):
            def issue(t, carry, r=r):
                for k in range(TOP_K):
                    row_copy(dest_ref, r, r, t, k).start(priority=k % 2)
                return carry
            lax.fori_loop(0, tt, issue, 0)

    per_block = tt // (ROW_TILES * TOP_K)
    for r in range(ns):
        rows = slice(r * tt, (r + 1) * tt)
        ahead_ref, ahead_tile = (dest_ref, ns - 1) if r == 0 else (dest_next_ref, r - 1)
        ahead_buf = (r + ns - 1) % ns

        mid = mid_ref[rows, :]
        xb = mid.astype(BF16)
        gs = _dot(xb, wsg_ref[...])
        h = (gs * jax.nn.sigmoid(gs) * _dot(xb, wsu_ref[...])).astype(BF16)
        acc = alpha * mid + _dot(h, wsd_ref[...])

        wait_rows(r)
        gates = gate_ref[rows, :]
        for k in range(TOP_K):
            gate_scr[k] = jnp.broadcast_to(gates[:, k:k + 1], (tt, LANES))
        for j in range(ROW_TILES):
            part = jnp.zeros((tt, LANES), F32)
            for k in range(TOP_K):
                part = part + gate_scr[k] * bufs[r][pl.ds(k * tt * ROW_TILES + j, tt,
                                                           stride=ROW_TILES), :]
                t0 = (j * TOP_K + k) * per_block
                for t in range(t0, t0 + per_block):
                    for kk in range(TOP_K):
                        row_copy(ahead_ref, ahead_tile, ahead_buf, t, kk).start(priority=kk % 2)
            routed_scr[:, j * LANES:(j + 1) * LANES] = part
        out = _layer_norm(acc + routed_scr[...], g2_ref[...], b2_ref[...])

        @pl.when(g < steps_a)
        def _(out=out, rows=rows):
            out_a_ref[rows, :] = out

        @pl.when(g >= steps_a)
        def _(out=out, rows=rows):
            out_b_ref[rows, :] = out

    @pl.when(g == ng - 1)
    def _():
        for b in range(ns - 1):
            wait_rows(b)


def _combine(dest_flat, ys, gate, mid, wsg, wsu, wsd, g2, b2, *, alpha, rows_a):
    tt = MOVE_TILE
    rows = tt * GATHER_SLOTS
    tokens = mid.shape[0]
    n = tokens // rows
    steps_a = rows_a // rows
    consts = [wsg, wsu, wsd, g2, b2]
    slots_spec = lambda f: pl.BlockSpec((rows * TOP_K,), f, memory_space=pltpu.SMEM)
    return pl.pallas_call(
        functools.partial(_combine_kernel, alpha=alpha, steps_a=steps_a),
        grid=(n,),
        in_specs=[slots_spec(lambda i: (i,)),
                  slots_spec(lambda i: (jnp.minimum(i + 1, n - 1),)),
                  pl.BlockSpec(memory_space=pl.ANY),
                  pl.BlockSpec((rows, TOP_K), lambda i: (i, 0)),
                  pl.BlockSpec((rows, D_MODEL), lambda i: (i, 0))]
                 + [_const_spec(c.shape) for c in consts],
        out_specs=(pl.BlockSpec((rows, D_MODEL), lambda i: (jnp.minimum(i, steps_a - 1), 0)),
                   pl.BlockSpec((rows, D_MODEL), lambda i: (jnp.maximum(i - steps_a, 0), 0))),
        out_shape=(jax.ShapeDtypeStruct((rows_a, D_MODEL), F32),
                   jax.ShapeDtypeStruct((tokens - rows_a, D_MODEL), F32)),
        scratch_shapes=[pltpu.VMEM((TOP_K * tt * ROW_TILES, LANES), F32)] * GATHER_SLOTS
                       + [pltpu.VMEM((TOP_K, tt, LANES), F32),
                          pltpu.VMEM((tt, D_MODEL), F32),
                          pltpu.SemaphoreType.DMA((GATHER_SLOTS,))],
        compiler_params=pltpu.CompilerParams(
            dimension_semantics=("arbitrary",), vmem_limit_bytes=VMEM_LIMIT),
        name="combine",
    )(dest_flat, dest_flat, ys, gate, mid, *consts)


def _tri(n, *, lower):
    r = lax.broadcasted_iota(jnp.int32, (n, n), 0)
    c = lax.broadcasted_iota(jnp.int32, (n, n), 1)
    return jnp.where((c <= r) if lower else (r <= c), 1.0, 0.0).astype(BF16)


def _moe(mid, rows3, w_router, b_router, w_e_gate, w_e_up, w_e_down, wsg, wsu, wsd, g2, b2, alpha,
         rows_a):
    tokens = mid.shape[0]
    bm = EXPERT_BLOCK
    wr_t = w_router.T
    wr_hi = wr_t.astype(BF16)
    wr_lo = (wr_t - wr_hi.astype(F32)).astype(BF16)
    r = lax.broadcasted_iota(jnp.int32, (ROUTER_TILE, ROUTER_TILE), 0)
    c = lax.broadcasted_iota(jnp.int32, (ROUTER_TILE, ROUTER_TILE), 1)
    before = jnp.where(r < c, 1.0, 0.0).astype(BF16)
    eidx, gate, rank, cnt = _router(mid, wr_hi, wr_lo, b_router.reshape(N_EXPERTS, 1), before)

    i32 = lambda a: a.astype(jnp.int32)
    experts = jnp.arange(N_EXPERTS, dtype=jnp.int32)
    counts = i32(cnt[:, 0])
    starts = jnp.cumsum(counts) - counts
    n_ch = (counts + bm - 1) // bm
    ch_ends = jnp.cumsum(n_ch)
    ch_starts = ch_ends - n_ch
    n_used = ch_ends[-1]
    max_chunks = tokens * TOP_K // bm + N_EXPERTS
    ci = jnp.arange(max_chunks, dtype=jnp.int32)
    cc = jnp.minimum(ci, n_used - 1)
    cexp = jnp.minimum(jnp.sum(i32(ch_ends[None, :] <= cc[:, None]), axis=1), N_EXPERTS - 1)
    hot = cexp[:, None] == experts[None, :]
    lookup = lambda table: jnp.sum(jnp.where(hot, table[None, :], 0), axis=1)
    k_in_expert = cc - lookup(ch_starts)
    crow = lookup(starts) + k_in_expert * bm
    used = ci < n_used
    cn = jnp.where(used, jnp.clip(lookup(counts) - k_in_expert * bm, 0, bm), 0)
    is_first = used & (k_in_expert == 0)
    cw = (jnp.cumsum(i32(is_first)) - 1) % 2
    cnew = jnp.where(is_first, cw, -1)
    later = (experts[None, :] > experts[:, None]) & (n_ch[None, :] > 0)
    next_expert = jnp.min(jnp.where(later, experts[None, :], N_EXPERTS), axis=1)
    cnext = lookup(jnp.where(next_expert < N_EXPERTS, next_expert, -1))
    cnext = jnp.where(is_first, cnext, -1)
    dest = _dest(eidx, rank, starts.astype(F32).reshape(N_EXPERTS, 1)).reshape(tokens * TOP_K)

    xs = _dispatch(dest, rows3)
    n_steps = (n_used + CHUNKS_PER_STEP - 1) // CHUNKS_PER_STEP
    ys = _experts(cexp, i32(crow), i32(cn), i32(cnew), i32(cnext), i32(cw),
                  i32(n_steps).reshape(1), xs, w_e_gate, w_e_up, w_e_down, tokens * TOP_K)
    return _combine(dest, ys, gate.T, mid, wsg, wsu, wsd, g2, b2, alpha=alpha, rows_a=rows_a)


def kernel(x_prompt, x_sample, cache_k, cache_v, cache_logf, state_conv, w_in, b_in, conv_w,
           conv_b, conv_ln_g, conv_ln_b, w_a, w_b, b_b, w_out, ln1_g, ln1_b, w_router, b_router,
           w_e_gate, w_e_up, w_e_down, w_s_gate, w_s_up, w_s_down, ln2_g, ln2_b):
    depth = w_in.shape[0]
    alpha = float((2 * depth) ** 0.25)
    batch, seq, _ = x_prompt.shape
    dbatch, dseq, _ = x_sample.shape
    past = cache_k.shape[2]
    rows_p, rows_s = batch * seq, dbatch * dseq
    total = rows_p + rows_s
    assert seq % IN_TILE == 0 and rows_s % IN_TILE == 0 and seq % MERGE_TILE == 0
    assert total % ROUTER_TILE == 0 and dseq == HIST and rows_p % dseq == 0
    assert total % DEST_TILE == 0 and (total * TOP_K) % EXPERT_BLOCK == 0
    assert rows_p % (MOVE_TILE * GATHER_SLOTS) == 0 and rows_s % (MOVE_TILE * GATHER_SLOTS) == 0

    hp = x_prompt.reshape(rows_p, D_MODEL)
    hs = x_sample.reshape(rows_s, D_MODEL)
    tri_in = _tri(IN_TILE, lower=True)
    upper_past = _tri(past, lower=False)
    row2 = lambda a: a.reshape(1, -1)
    outs = {n: [] for n in ("kp", "vp", "fp", "cp", "ks", "vs", "fs", "cs")}

    for l in range(depth):
        w = w_in[l]
        b = b_in[l]
        main_cols = lambda a: jnp.concatenate([a[..., :OFF_F], a[..., OFF_GLU:]], axis=-1)
        w_main = main_cols(w).astype(BF16)
        b_main = row2(main_cols(b))
        w_f = jnp.pad(w[:, OFF_F:OFF_GLU], ((0, 0), (0, LANES - FOX_HEADS))).astype(BF16)
        b_f = row2(jnp.pad(b[OFF_F:OFF_GLU], (0, LANES - FOX_HEADS)))
        cw = jnp.pad(conv_w[l], ((0, 1), (0, 0)))
        conv_p = (cw, row2(conv_b[l]), row2(conv_ln_g[l]), row2(conv_ln_b[l]),
                  w_b[l].astype(BF16), row2(b_b[l]))
        wa, wout = w_a[l].astype(BF16), w_out[l].astype(BF16)
        g1, b1 = row2(ln1_g[l]), row2(ln1_b[l])

        q, k, v, kb, vb, logf, c, u, sa, sb = _inproj(
            hp, w_main, b_main, w_f, b_f, tri_in, tiles_per_seq=seq // IN_TILE)
        c_row = c.reshape(batch, seq, FOX_HEADS).transpose(0, 2, 1)
        attn = _attn_prompt(q, kb, vb, c_row, batch=batch, seq=seq)
        mid, rows3 = _merge(u, u, attn, sa, sb, hp, conv_p, wa, wout, g1, b1,
                            n_seq=batch, seq=seq, ts=MERGE_TILE, hist_from_u=True, alpha=alpha,
                            total_rows=total, row_offset=0)
        outs["kp"].append(k.reshape(batch, seq, FOX_HEADS, HEAD_DIM))
        outs["vp"].append(v.reshape(batch, seq, FOX_HEADS, HEAD_DIM))
        outs["fp"].append(logf.reshape(batch, seq, FOX_HEADS))
        outs["cp"].append(u.reshape(batch, seq, CONV_CH)[:, seq - (CONV_WIDTH - 1):])

        q, k, v, kb, vb, logf, _, u, sa, sb = _inproj(
            hs, w_main, b_main, w_f, b_f, tri_in, tiles_per_seq=1)
        logf_t = logf.reshape(dbatch, dseq, FOX_HEADS).transpose(0, 2, 1)
        attn = _attn_sample(
            q, kb, vb, logf, logf_t, cache_k[l].reshape(dbatch, past, FOX_WIDTH),
            cache_v[l].reshape(dbatch, past, FOX_WIDTH), cache_logf[l].transpose(0, 2, 1),
            upper_past, batch=dbatch, t=dseq, past=past)
        hist = jnp.pad(state_conv[l], ((0, 0), (HIST - (CONV_WIDTH - 1), 0), (0, 0)))
        mid, rows3 = _merge(u, hist.reshape(dbatch * HIST, CONV_CH), attn, sa, sb, hs, conv_p,
                            wa, wout, g1, b1, n_seq=dbatch, seq=dseq, ts=dseq, hist_from_u=False,
                            alpha=alpha, total_rows=total, row_offset=rows_p, prev=(mid, rows3))
        outs["ks"].append(k.reshape(dbatch, dseq, FOX_HEADS, HEAD_DIM))
        outs["vs"].append(v.reshape(dbatch, dseq, FOX_HEADS, HEAD_DIM))
        outs["fs"].append(logf.reshape(dbatch, dseq, FOX_HEADS))
        u3 = u.reshape(dbatch, dseq, CONV_CH)
        u_ext = jnp.concatenate([state_conv[l], u3], axis=1)
        outs["cs"].append(u_ext[:, -(CONV_WIDTH - 1):])

        hp, hs = _moe(mid, rows3, w_router[l], b_router[l], w_e_gate[l], w_e_up[l], w_e_down[l],
                      w_s_gate[l].astype(BF16), w_s_up[l].astype(BF16), w_s_down[l].astype(BF16),
                      row2(ln2_g[l]), row2(ln2_b[l]), alpha, rows_p)

    st = lambda n: jnp.stack(outs[n])
    return (hp.reshape(batch, seq, D_MODEL), hs.reshape(dbatch, dseq, D_MODEL),
            st("kp"), st("vp"), st("fp"), st("cp"), st("ks"), st("vs"), st("fs"), st("cs"))
```

```python
import functools

import jax
import jax.numpy as jnp
from jax import lax
from jax.experimental import pallas as pl
from jax.experimental.pallas import tpu as pltpu

D_MODEL = 1024
FOX_HEADS = 8
HEAD_DIM = 64
FOX_WIDTH = FOX_HEADS * HEAD_DIM
ATTN_SCALE = HEAD_DIM ** -0.5
LOG2E = 1.4426950408889634
CONV_CH = D_MODEL // 2
CONV_WIDTH = 31
N_EXPERTS = 256
TOP_K = 8
N_GROUPS = 8
GROUP_SIZE = N_EXPERTS // N_GROUPS
TOPK_GROUPS = 4
D_EXPERT = D_MODEL // 4
ROUTED_SCALE = 2.5
LN_EPS = 1e-5

OFF_K = FOX_WIDTH
OFF_V = 2 * FOX_WIDTH
OFF_F = 3 * FOX_WIDTH
OFF_GLU = OFF_F + FOX_HEADS
OFF_GA = OFF_GLU + 2 * CONV_CH
OFF_GB = OFF_GA + D_MODEL

LANES = 128
SUBLANES = 8
ROW_TILES = D_MODEL // LANES
VMEM_LIMIT = 56 * 1024 * 1024

IN_TILE = 512
ATTN_TILE = 256
MERGE_TILE = 512
MERGE_PARTS = 2
CONV_SUB = 32
HIST = 32
ROUTER_TILE = 512
MOVE_TILE = 128
DEST_TILE = 1024
EXPERT_BLOCK = 256

F32 = jnp.float32
BF16 = jnp.bfloat16
NEG_INF = float("-inf")
NT_DIMS = (((1,), (1,)), ((), ()))


def _const_spec(shape):
    nd = len(shape)
    return pl.BlockSpec(shape, lambda *_: (0,) * nd, pipeline_mode=pl.Buffered(1))


def _split3(x):
    hi = x.astype(BF16)
    r1 = x - hi.astype(F32)
    mid = r1.astype(BF16)
    lo = (r1 - mid.astype(F32)).astype(BF16)
    return hi, mid, lo


def _dot(a, b):
    return jnp.dot(a, b, preferred_element_type=F32)


def _dot_nt(a, b):
    return lax.dot_general(a, b, NT_DIMS, preferred_element_type=F32)


def _exact_dot(ones_mat, x, *, ones_on_left):
    acc = None
    for part in _split3(x):
        term = _dot(ones_mat, part) if ones_on_left else _dot(part, ones_mat)
        acc = term if acc is None else acc + term
    return acc


def _layer_norm(x, g, b):
    mu = jnp.mean(x, axis=-1, keepdims=True)
    xc = x - mu
    var = jnp.mean(xc * xc, axis=-1, keepdims=True)
    return xc * lax.rsqrt(var + LN_EPS) * g + b


def _log_sigmoid(z):
    return jnp.minimum(z, 0.0) - jnp.log1p(jnp.exp(-jnp.abs(z)))


def _inproj_kernel(x_ref, w_ref, b_ref, wf_ref, bf_ref, tri_ref,
                   q_ref, k_ref, v_ref, kb_ref, vb_ref, logf_ref, c_ref, u_ref,
                   sa_ref, sb_ref, carry_ref, *, tiles_per_seq):
    i = pl.program_id(0)
    xb = x_ref[...].astype(BF16)

    def proj(c0, c1):
        return _dot(xb, w_ref[:, c0:c1]) + b_ref[:, c0:c1]

    q = proj(0, FOX_WIDTH)
    q_ref[...] = (q * (ATTN_SCALE * LOG2E)).astype(BF16)
    k = proj(FOX_WIDTH, 2 * FOX_WIDTH)
    kb_ref[...] = k.astype(BF16)
    v = proj(2 * FOX_WIDTH, 3 * FOX_WIDTH)
    vb_ref[...] = v.astype(BF16)
    for h in range(FOX_HEADS):
        head_rows = pl.ds(h, IN_TILE, stride=FOX_HEADS)
        k_ref[head_rows, :] = k[:, h * HEAD_DIM:(h + 1) * HEAD_DIM]
        v_ref[head_rows, :] = v[:, h * HEAD_DIM:(h + 1) * HEAD_DIM]

    logf = _log_sigmoid(_dot(xb, wf_ref[...]) + bf_ref[...])
    logf_ref[...] = logf[:, :FOX_HEADS]

    @pl.when(i % tiles_per_seq == 0)
    def _():
        carry_ref[...] = jnp.zeros_like(carry_ref)

    c = _exact_dot(tri_ref[...], logf, ones_on_left=True) + carry_ref[...]
    c_ref[...] = c[:, :FOX_HEADS]
    carry_ref[...] = c[IN_TILE - 1:IN_TILE, :]

    g0 = 3 * FOX_WIDTH
    glu_a = proj(g0, g0 + CONV_CH)
    glu_b = proj(g0 + CONV_CH, g0 + 2 * CONV_CH)
    u_ref[...] = glu_a * jax.nn.sigmoid(glu_b)
    g1 = g0 + 2 * CONV_CH
    sa_ref[...] = jax.nn.sigmoid(proj(g1, g1 + D_MODEL)).astype(BF16)
    sb_ref[...] = jax.nn.sigmoid(proj(g1 + D_MODEL, g1 + 2 * D_MODEL)).astype(BF16)


def _inproj(x, w_main, b_main, w_f, b_f, tri, *, tiles_per_seq):
    rows = x.shape[0]
    n_main = w_main.shape[1]
    row_spec = lambda w: pl.BlockSpec((IN_TILE, w), lambda i: (i, 0))
    out_shapes = (
        jax.ShapeDtypeStruct((rows, FOX_WIDTH), BF16),
        jax.ShapeDtypeStruct((rows * FOX_HEADS, HEAD_DIM), F32),
        jax.ShapeDtypeStruct((rows * FOX_HEADS, HEAD_DIM), F32),
        jax.ShapeDtypeStruct((rows, FOX_WIDTH), BF16),
        jax.ShapeDtypeStruct((rows, FOX_WIDTH), BF16),
        jax.ShapeDtypeStruct((rows, FOX_HEADS), F32),
        jax.ShapeDtypeStruct((rows, FOX_HEADS), F32),
        jax.ShapeDtypeStruct((rows, CONV_CH), F32),
        jax.ShapeDtypeStruct((rows, D_MODEL), BF16),
        jax.ShapeDtypeStruct((rows, D_MODEL), BF16),
    )
    head_spec = pl.BlockSpec((IN_TILE * FOX_HEADS, HEAD_DIM), lambda i: (i, 0))
    out_specs = (row_spec(FOX_WIDTH), head_spec, head_spec) + (row_spec(FOX_WIDTH),) * 2 + (
        row_spec(FOX_HEADS),) * 2 + (row_spec(CONV_CH), row_spec(D_MODEL), row_spec(D_MODEL))
    return pl.pallas_call(
        functools.partial(_inproj_kernel, tiles_per_seq=tiles_per_seq),
        grid=(rows // IN_TILE,),
        in_specs=[row_spec(D_MODEL), _const_spec((D_MODEL, n_main)), _const_spec((1, n_main)),
                  _const_spec((D_MODEL, LANES)), _const_spec((1, LANES)),
                  _const_spec((IN_TILE, IN_TILE))],
        out_specs=out_specs,
        out_shape=out_shapes,
        scratch_shapes=[pltpu.VMEM((1, LANES), F32)],
        compiler_params=pltpu.CompilerParams(
            dimension_semantics=("arbitrary",), vmem_limit_bytes=VMEM_LIMIT),
        name="inproj",
    )(x, w_main, b_main, w_f, b_f, tri)


def _attn_prompt_kernel(q_ref, k_ref, v_ref, crow_ref, o_ref, qm_scr, m_scr, acc_scr):
    t = ATTN_TILE
    i = pl.program_id(1)
    lane = lax.broadcasted_iota(jnp.int32, (t, LANES), 1)
    row = lax.broadcasted_iota(jnp.int32, (t, t), 0)
    col = lax.broadcasted_iota(jnp.int32, (t, t), 1)
    causal = col <= row
    wide = lambda x: jnp.concatenate([x] * (t // LANES), axis=1)
    pair_lanes = lambda pair: slice(pair * LANES, (pair + 1) * LANES)

    for h in range(FOX_HEADS):
        qp = q_ref[:, pair_lanes(h // 2)]
        in_head = (lane < HEAD_DIM) if h % 2 == 0 else (lane >= HEAD_DIM)
        qm_scr[h] = jnp.where(in_head, qp, jnp.zeros_like(qp))
        m_scr[h] = jnp.full((t, LANES), NEG_INF, F32)
        acc_scr[h] = jnp.zeros((t, 2 * LANES), F32)
    ones = jnp.ones((t, LANES), BF16)

    def step(j, masked):
        r0 = pl.multiple_of(j * t, t)
        ck = crow_ref[0, :, pl.ds(r0, t)] * LOG2E
        for pair in range(FOX_HEADS // 2):
            kj = k_ref[pl.ds(r0, t), pair_lanes(pair)]
            vj = jnp.concatenate([v_ref[pl.ds(r0, t), pair_lanes(pair)], ones], axis=1)
            for h in (2 * pair, 2 * pair + 1):
                s = _dot_nt(qm_scr[h], kj) - ck[h:h + 1, :]
                if masked:
                    s = jnp.where(causal, s, NEG_INF)
                m_prev = m_scr[h]
                m_new = jnp.maximum(m_prev, jnp.max(s, axis=1, keepdims=True))
                alpha = jnp.exp2(m_prev - m_new)
                p = jnp.exp2(s - wide(m_new))
                acc_scr[h] = wide(alpha) * acc_scr[h] + _dot(p.astype(BF16), vj)
                m_scr[h] = m_new

    def body(jj, carry):
        step(2 * jj, False)
        step(2 * jj + 1, False)
        return carry

    lax.fori_loop(0, i // 2, body, 0)

    @pl.when(i % 2 == 1)
    def _():
        step(i - 1, False)

    step(i, True)
    for pair in range(FOX_HEADS // 2):
        o0, o1 = (acc_scr[h, :, :LANES] / acc_scr[h, :, LANES:] for h in (2 * pair, 2 * pair + 1))
        o_ref[:, pair_lanes(pair)] = jnp.where(lane < HEAD_DIM, o0, o1).astype(BF16)


def _attn_prompt(q, kb, vb, c_row, *, batch, seq):
    t = ATTN_TILE
    nq = seq // t
    return pl.pallas_call(
        _attn_prompt_kernel,
        grid=(batch, nq),
        in_specs=[
            pl.BlockSpec((t, FOX_WIDTH), lambda b, i: (b * nq + i, 0)),
            pl.BlockSpec((seq, FOX_WIDTH), lambda b, i: (b, 0)),
            pl.BlockSpec((seq, FOX_WIDTH), lambda b, i: (b, 0)),
            pl.BlockSpec((1, FOX_HEADS, seq), lambda b, i: (b, 0, 0)),
        ],
        out_specs=pl.BlockSpec((t, FOX_WIDTH), lambda b, i: (b * nq + i, 0)),
        out_shape=jax.ShapeDtypeStruct((batch * seq, FOX_WIDTH), BF16),
        scratch_shapes=[pltpu.VMEM((FOX_HEADS, t, LANES), BF16),
                        pltpu.VMEM((FOX_HEADS, t, LANES), F32),
                        pltpu.VMEM((FOX_HEADS, t, 2 * LANES), F32)],
        compiler_params=pltpu.CompilerParams(
            dimension_semantics=("arbitrary", "arbitrary"), vmem_limit_bytes=VMEM_LIMIT),
        name="attn_prompt",
    )(q, kb, vb, c_row)


def _attn_sample_kernel(q_ref, kn_ref, vn_ref, lf_ref, lft_ref, ck_ref, cv_ref, clft_ref,
                        upper_ref, o_ref, *, t, past):
    rows = FOX_HEADS * t
    lane_head = lax.broadcasted_iota(jnp.int32, (t, FOX_WIDTH), 1) // HEAD_DIM
    q = q_ref[...]
    q_stack = jnp.concatenate(
        [jnp.where(lane_head == h, q, jnp.zeros_like(q)) for h in range(FOX_HEADS)], axis=0)

    clf = clft_ref[0]
    prefix = _exact_dot(upper_ref[...], clf, ones_on_left=False)
    to_end = prefix[:, past - 1:past] - prefix
    ri = lax.broadcasted_iota(jnp.int32, (t, t), 0)
    ci = lax.broadcasted_iota(jnp.int32, (t, t), 1)
    lower = jnp.where(ci <= ri, 1.0, 0.0).astype(BF16)
    upper = jnp.where(ri <= ci, 1.0, 0.0).astype(BF16)
    cn_col = _exact_dot(lower, lf_ref[...], ones_on_left=True)
    cn_row = _exact_dot(upper, lft_ref[0], ones_on_left=False)

    stack = lambda f: jnp.concatenate([f(h) for h in range(FOX_HEADS)], axis=0)
    cn_stack = stack(lambda h: cn_col[:, h:h + 1])
    bias_c = stack(lambda h: jnp.broadcast_to(to_end[h:h + 1, :], (t, past)))
    bias_n = stack(lambda h: jnp.broadcast_to(cn_row[h:h + 1, :], (t, t)))

    kc = ck_ref[0].astype(BF16)
    vc = cv_ref[0].astype(BF16)
    s_c = _dot_nt(q_stack, kc) + (bias_c + cn_stack) * LOG2E
    s_n = _dot_nt(q_stack, kn_ref[...]) + (cn_stack - bias_n) * LOG2E
    tq = lax.broadcasted_iota(jnp.int32, (rows, t), 0) % t
    tk = lax.broadcasted_iota(jnp.int32, (rows, t), 1)
    s_n = jnp.where(tk <= tq, s_n, NEG_INF)
    m = jnp.maximum(jnp.max(s_c, axis=1, keepdims=True), jnp.max(s_n, axis=1, keepdims=True))
    p_c = jnp.exp2(s_c - m)
    p_n = jnp.exp2(s_n - m)
    denom = jnp.sum(p_c, axis=1, keepdims=True) + jnp.sum(p_n, axis=1, keepdims=True)
    o_stack = (_dot(p_c.astype(BF16), vc) + _dot(p_n.astype(BF16), vn_ref[...])) / denom
    out = jnp.zeros((t, FOX_WIDTH), F32)
    for h in range(FOX_HEADS):
        out = out + jnp.where(lane_head == h, o_stack[h * t:(h + 1) * t, :], 0.0)
    o_ref[...] = out.astype(BF16)


def _attn_sample(q, kb, vb, logf, logf_t, cache_k, cache_v, cache_logf_t, upper, *, batch, t, past):
    row_spec = lambda w: pl.BlockSpec((t, w), lambda b: (b, 0))
    return pl.pallas_call(
        functools.partial(_attn_sample_kernel, t=t, past=past),
        grid=(batch,),
        in_specs=[
            row_spec(FOX_WIDTH), row_spec(FOX_WIDTH), row_spec(FOX_WIDTH), row_spec(FOX_HEADS),
            pl.BlockSpec((1, FOX_HEADS, t), lambda b: (b, 0, 0)),
            pl.BlockSpec((1, past, FOX_WIDTH), lambda b: (b, 0, 0)),
            pl.BlockSpec((1, past, FOX_WIDTH), lambda b: (b, 0, 0)),
            pl.BlockSpec((1, FOX_HEADS, past), lambda b: (b, 0, 0)),
            _const_spec((past, past)),
        ],
        out_specs=row_spec(FOX_WIDTH),
        out_shape=jax.ShapeDtypeStruct((batch * t, FOX_WIDTH), BF16),
        compiler_params=pltpu.CompilerParams(
            dimension_semantics=("arbitrary",), vmem_limit_bytes=VMEM_LIMIT),
        name="attn_sample",
    )(q, kb, vb, logf, logf_t, cache_k, cache_v, cache_logf_t, upper)


def _merge_kernel(u_ref, hist_ref, attn_ref, sa_ref, sb_ref, x_ref,
                  cw_ref, cb_ref, cg_ref, cbeta_ref, wb_ref, bb_ref, wa_ref, wout_ref,
                  g1_ref, b1_ref, *rest, ts, parts, tiles_per_seq, n_tiles, zero_first_hist, alpha,
                  aliased):
    if aliased:
        rest = rest[2:]
    mid_ref, mid3_ref = rest[:2]
    scratch = rest[2:]
    i = pl.program_id(0)
    th = ts // parts

    @pl.when(i >= n_tiles)
    def _():
        mid_ref[...] = jnp.zeros_like(mid_ref)
        mid3_ref[...] = jnp.zeros_like(mid3_ref)

    @pl.when(i < n_tiles)
    def _():
        for p in range(parts):
            r0 = p * th
            if p == 0:
                hist = hist_ref[...]
                if zero_first_hist:
                    hist = jnp.where(i % tiles_per_seq == 0, 0.0, hist)
            else:
                hist = u_ref[r0 - HIST:r0, :]
            _merge_rows(hist, r0, th, u_ref, attn_ref, sa_ref, sb_ref, x_ref, cw_ref, cb_ref,
                        cg_ref, cbeta_ref, wb_ref, bb_ref, wa_ref, wout_ref, g1_ref, b1_ref,
                        mid_ref, mid3_ref, *scratch[3 * p:3 * p + 3], alpha=alpha)


def _merge_rows(hist, r0, th, u_ref, attn_ref, sa_ref, sb_ref, x_ref, cw_ref, cb_ref, cg_ref,
                cbeta_ref, wb_ref, bb_ref, wa_ref, wout_ref, g1_ref, b1_ref, mid_ref, mid3_ref,
                ue_scr, ph_scr, h_scr, *, alpha):
    rows = slice(r0, r0 + th)
    ue_scr[0:HIST, :] = hist
    ue_scr[HIST:HIST + th, :] = u_ref[rows, :]
    lead = HIST - (CONV_WIDTH - 1)
    for r in range(SUBLANES):
        span = th + (CONV_WIDTH - 1 - r) // SUBLANES * SUBLANES
        ph_scr[r, 0:span, :] = ue_scr[lead + r:lead + r + span, :]

    for rt in range(th // CONV_SUB):
        base = rt * CONV_SUB
        acc = jnp.zeros((CONV_SUB, CONV_CH), F32)
        for j in range(CONV_WIDTH):
            r, a = j % SUBLANES, j // SUBLANES
            start = base + a * SUBLANES
            acc = acc + cw_ref[j:j + 1, :] * ph_scr[r, start:start + CONV_SUB, :]
        h_scr[base:base + CONV_SUB, :] = acc

    h = _layer_norm(h_scr[...] + cb_ref[...], cg_ref[...], cbeta_ref[...])
    h = h * jax.nn.sigmoid(h)
    conv_out = _dot(h.astype(BF16), wb_ref[...]) + bb_ref[...]
    attn_out = _dot(attn_ref[rows, :], wa_ref[...])
    m = sa_ref[rows, :].astype(F32) * attn_out + sb_ref[rows, :].astype(F32) * conv_out
    z = alpha * x_ref[rows, :] + _dot(m.astype(BF16), wout_ref[...])
    mid = _layer_norm(z, g1_ref[...], b1_ref[...])
    mid_ref[rows, :] = mid
    for j in range(ROW_TILES):
        mid3_ref[pl.ds(r0 * ROW_TILES + j, th, stride=ROW_TILES), :] = (
            mid[:, j * LANES:(j + 1) * LANES])


def _merge(u, hist, attn, sa, sb, x, conv_p, wa, wout, g1, b1, *, n_seq, seq, ts, parts,
           hist_from_u, alpha, total_rows, row_offset, prev=None):
    nt = seq // ts
    n_tiles = n_seq * nt
    off = row_offset // ts
    grid_tiles = n_tiles if prev is not None else total_rows // ts
    src = lambda i: jnp.minimum(i, n_tiles - 1)
    row_spec = lambda w: pl.BlockSpec((ts, w), lambda i: (src(i), 0))
    if hist_from_u:
        per = ts // HIST
        hist_spec = pl.BlockSpec((HIST, CONV_CH), lambda i: (jnp.maximum(src(i) * per - 1, 0), 0))
    else:
        hist_spec = pl.BlockSpec((HIST, CONV_CH), lambda i: (src(i) // nt, 0))
    cw, cb, cg, cbeta, wb, bb = conv_p
    consts = [cw, cb, cg, cbeta, wb, bb, wa, wout, g1, b1]
    in_specs = [row_spec(CONV_CH), hist_spec, row_spec(FOX_WIDTH), row_spec(D_MODEL),
                row_spec(D_MODEL), row_spec(D_MODEL)] + [_const_spec(c.shape) for c in consts]
    args = [u, hist, attn, sa, sb, x] + consts
    aliases = {}
    if prev is not None:
        in_specs += [pl.BlockSpec(memory_space=pl.ANY)] * 2
        aliases = {len(args): 0, len(args) + 1: 1}
        args += list(prev)
    th = ts // parts
    span = th + (CONV_WIDTH - 1) // SUBLANES * SUBLANES
    return pl.pallas_call(
        functools.partial(_merge_kernel, ts=ts, parts=parts, tiles_per_seq=nt, n_tiles=n_tiles,
                          zero_first_hist=hist_from_u, alpha=alpha, aliased=prev is not None),
        grid=(grid_tiles,),
        in_specs=in_specs,
        out_specs=(pl.BlockSpec((ts, D_MODEL), lambda i: (off + i, 0)),
                   pl.BlockSpec((ts * ROW_TILES, LANES), lambda i: (off + i, 0))),
        out_shape=(jax.ShapeDtypeStruct((total_rows, D_MODEL), F32),
                   jax.ShapeDtypeStruct((total_rows * ROW_TILES, LANES), F32)),
        scratch_shapes=[pltpu.VMEM((HIST + th, CONV_CH), F32),
                        pltpu.VMEM((SUBLANES, span, CONV_CH), F32),
                        pltpu.VMEM((th, CONV_CH), F32)] * parts,
        input_output_aliases=aliases,
        compiler_params=pltpu.CompilerParams(
            dimension_semantics=("arbitrary",), vmem_limit_bytes=VMEM_LIMIT),
        name="merge",
    )(*args)


def _router_kernel(mid_ref, wr_hi_ref, wr_lo_ref, br_ref, before_ref,
                   eidx_ref, gate_ref, rank_ref, cnt_ref, carry_scr):
    tr = ROUTER_TILE
    i = pl.program_id(0)

    @pl.when(i == 0)
    def _():
        carry_scr[...] = jnp.zeros_like(carry_scr)

    x = mid_ref[...]
    x_hi = x.astype(BF16)
    x_lo = (x - x_hi.astype(F32)).astype(BF16)
    wr_hi = wr_hi_ref[...]
    logits = _dot_nt(wr_hi, x_hi) + _dot_nt(wr_hi, x_lo) + _dot_nt(wr_lo_ref[...], x_hi)
    scores = jax.nn.sigmoid(logits)
    sel = scores + br_ref[...]

    sel3 = sel.reshape(N_GROUPS, GROUP_SIZE, tr)
    in_group = lax.broadcasted_iota(jnp.int32, sel3.shape, 1)
    m1 = jnp.max(sel3, axis=1, keepdims=True)
    first = jnp.min(jnp.where(sel3 == m1, in_group, GROUP_SIZE), axis=1, keepdims=True)
    m2 = jnp.max(jnp.where(in_group == first, NEG_INF, sel3), axis=1, keepdims=True)
    gs = m1 + m2
    gi = lax.broadcasted_iota(jnp.int32, gs.shape, 0)
    beaten = jnp.zeros(gs.shape, F32)
    for g in range(N_GROUPS):
        other = gs[g:g + 1]
        wins = (other > gs) | ((other == gs) & (g < gi))
        beaten = beaten + jnp.where(wins, 1.0, 0.0)
    drop = jnp.where(beaten < TOPK_GROUPS, 0.0, NEG_INF)
    cur = (sel3 + drop).reshape(N_EXPERTS, tr)

    ei = lax.broadcasted_iota(jnp.int32, (N_EXPERTS, tr), 0)
    idxs, vals = [], []
    picked = jnp.zeros((N_EXPERTS, tr), F32)
    for _ in range(TOP_K):
        m = jnp.max(cur, axis=0, keepdims=True)
        idx = jnp.min(jnp.where(cur == m, ei, N_EXPERTS), axis=0, keepdims=True)
        hit = ei == idx
        vals.append(jnp.sum(jnp.where(hit, scores, 0.0), axis=0, keepdims=True))
        idxs.append(idx)
        picked = picked + jnp.where(hit, 1.0, 0.0)
        cur = jnp.where(hit, NEG_INF, cur)

    total = vals[0]
    for v in vals[1:]:
        total = total + v
    for k in range(TOP_K):
        gate_ref[k:k + 1, :] = vals[k] / total * ROUTED_SCALE
        eidx_ref[k:k + 1, :] = idxs[k]

    ahead = _dot(picked.astype(BF16), before_ref[...]) + carry_scr[:, 0:1]
    for k in range(TOP_K):
        rank = jnp.sum(jnp.where(ei == idxs[k], ahead, 0.0), axis=0, keepdims=True)
        rank_ref[k:k + 1, :] = rank.astype(jnp.int32)
    carry_scr[...] = carry_scr[...] + jnp.sum(picked, axis=1, keepdims=True)
    cnt_ref[...] = carry_scr[...]


def _router(mid, wr_hi, wr_lo, br, before):
    tr = ROUTER_TILE
    tokens = mid.shape[0]
    tok_spec = pl.BlockSpec((TOP_K, tr), lambda i: (0, i))
    return pl.pallas_call(
        _router_kernel,
        grid=(tokens // tr,),
        in_specs=[pl.BlockSpec((tr, D_MODEL), lambda i: (i, 0)),
                  _const_spec((N_EXPERTS, D_MODEL)), _const_spec((N_EXPERTS, D_MODEL)),
                  _const_spec((N_EXPERTS, 1)), _const_spec((tr, tr))],
        out_specs=(tok_spec, tok_spec, tok_spec,
                   pl.BlockSpec((N_EXPERTS, LANES), lambda i: (0, 0))),
        out_shape=(jax.ShapeDtypeStruct((TOP_K, tokens), jnp.int32),
                   jax.ShapeDtypeStruct((TOP_K, tokens), F32),
                   jax.ShapeDtypeStruct((TOP_K, tokens), jnp.int32),
                   jax.ShapeDtypeStruct((N_EXPERTS, LANES), F32)),
        scratch_shapes=[pltpu.VMEM((N_EXPERTS, LANES), F32)],
        compiler_params=pltpu.CompilerParams(
            dimension_semantics=("arbitrary",), vmem_limit_bytes=VMEM_LIMIT),
        name="router",
    )(mid, wr_hi, wr_lo, br, before)


def _dest_kernel(eidx_ref, rank_ref, starts_ref, dest_ref):
    tt = MOVE_TILE
    tokens = eidx_ref.shape[1]
    ei = lax.broadcasted_iota(jnp.int32, (N_EXPERTS, tokens), 0)
    starts = starts_ref[...]
    for k in range(TOP_K):
        hit = ei == eidx_ref[k:k + 1, :]
        start = jnp.sum(jnp.where(hit, starts, 0.0), axis=0, keepdims=True)
        dest = start.astype(jnp.int32) + rank_ref[k:k + 1, :]
        for c in range(tokens // tt):
            dest_ref[c, k:k + 1, :] = dest[:, c * tt:(c + 1) * tt]


def _dest(eidx, rank, starts):
    tt = MOVE_TILE
    tokens = eidx.shape[1]
    step = DEST_TILE
    tok_spec = pl.BlockSpec((TOP_K, step), lambda i: (0, i))
    return pl.pallas_call(
        _dest_kernel,
        grid=(tokens // step,),
        in_specs=[tok_spec, tok_spec, _const_spec((N_EXPERTS, 1))],
        out_specs=pl.BlockSpec((step // tt, TOP_K, tt), lambda i: (i, 0, 0)),
        out_shape=jax.ShapeDtypeStruct((tokens // tt, TOP_K, tt), jnp.int32),
        compiler_params=pltpu.CompilerParams(dimension_semantics=("arbitrary",)),
        name="dest",
    )(eidx, rank, starts)


def _row_slice(ref, row):
    return ref.at[pl.ds(pl.multiple_of(row * ROW_TILES, ROW_TILES), ROW_TILES), :]


def _dispatch_kernel(dest_ref, rows_ref, xs_ref, zeros_scr, sem, *, n_rows):
    tt = MOVE_TILE

    @pl.when(pl.program_id(0) == 0)
    def _():
        zeros_scr[...] = jnp.zeros_like(zeros_scr)
        tail = pltpu.make_async_copy(
            zeros_scr, xs_ref.at[pl.ds(n_rows * ROW_TILES, EXPERT_BLOCK * ROW_TILES), :], sem)
        tail.start()
        tail.wait()

    def issue(t, carry):
        src = _row_slice(rows_ref, t)
        for k in range(TOP_K):
            pltpu.make_async_copy(src, _row_slice(xs_ref, dest_ref[k * tt + t]), sem).start(
                priority=k % 2)
        return carry

    lax.fori_loop(0, tt, issue, 0)
    for _ in range(TOP_K):
        pltpu.make_async_copy(rows_ref, xs_ref.at[pl.ds(0, tt * ROW_TILES), :], sem).wait()


def _dispatch(dest_flat, rows3):
    tt = MOVE_TILE
    tokens = rows3.shape[0] // ROW_TILES
    n_rows = tokens * TOP_K
    return pl.pallas_call(
        functools.partial(_dispatch_kernel, n_rows=n_rows),
        grid=(tokens // tt,),
        in_specs=[pl.BlockSpec((tt * TOP_K,), lambda i: (i,), memory_space=pltpu.SMEM),
                  pl.BlockSpec((tt * ROW_TILES, LANES), lambda i: (i, 0))],
        out_specs=pl.BlockSpec(memory_space=pl.ANY),
        out_shape=jax.ShapeDtypeStruct(((n_rows + EXPERT_BLOCK) * ROW_TILES, LANES), F32),
        scratch_shapes=[pltpu.VMEM((EXPERT_BLOCK * ROW_TILES, LANES), F32),
                        pltpu.SemaphoreType.DMA(())],
        compiler_params=pltpu.CompilerParams(dimension_semantics=("arbitrary",)),
        name="dispatch",
    )(dest_flat, rows3)


_PIECES = tuple(EXPERT_BLOCK >> s for s in range(EXPERT_BLOCK.bit_length()))
CHUNKS_PER_STEP = 2
ROWS_AHEAD = 6
ROW_SLOTS = ROWS_AHEAD + CHUNKS_PER_STEP
OUT_SLOTS = 2 * CHUNKS_PER_STEP


def _expert_kernel(cexp_ref, crow_ref, cn_ref, cnew_ref, cnext_ref, cw_ref, nsteps_ref,
                   xs_hbm, wg_hbm, wu_hbm, wd_hbm, ys_hbm,
                   xbuf, ybuf, wg_buf, wu_buf, wd_buf, wgb, wub, wdb, xsem, ysem, wsem):
    bm = EXPERT_BLOCK
    s = pl.program_id(0)
    n_steps = nsteps_ref[0]
    n_chunks = n_steps * CHUNKS_PER_STEP
    chunks = [s * CHUNKS_PER_STEP + c for c in range(CHUNKS_PER_STEP)]

    def weight_copies(e, s):
        return [pltpu.make_async_copy(hbm.at[e], buf.at[s], wsem.at[s])
                for hbm, buf in ((wg_hbm, wg_buf), (wu_hbm, wu_buf), (wd_hbm, wd_buf))]

    def rows_in(j, s):
        first = pl.multiple_of(crow_ref[j] * ROW_TILES, ROW_TILES)
        return pltpu.make_async_copy(xs_hbm.at[pl.ds(first, bm * ROW_TILES), :], xbuf.at[s],
                                     xsem.at[s])

    def rows_out(j, s, act):
        n, row0 = cn_ref[j], crow_ref[j]

        def piece(done, p):
            src = pl.multiple_of(done * ROW_TILES, ROW_TILES)
            dst = pl.multiple_of((row0 + done) * ROW_TILES, ROW_TILES)
            act(pltpu.make_async_copy(ybuf.at[s, pl.ds(src, p * ROW_TILES), :],
                                      ys_hbm.at[pl.ds(dst, p * ROW_TILES), :], ysem.at[s]))

        @pl.when(n == bm)
        def _():
            piece(jnp.int32(0), bm)

        @pl.when(n < bm)
        def _():
            done = jnp.int32(0)
            for p in _PIECES[1:]:
                has = (n & p) != 0

                @pl.when(has)
                def _(done=done, p=p):
                    piece(done, p)

                done = done + jnp.where(has, p, 0)

    start = lambda c: c.start()
    wait = lambda c: c.wait()

    @pl.when(s == 0)
    def _():
        for d in range(ROWS_AHEAD):
            @pl.when(d < n_chunks)
            def _(d=d):
                rows_in(d, d).start()
        for c in weight_copies(cexp_ref[0], 0):
            c.start()

    @pl.when(s < n_steps)
    def _():
        for j in chunks:
            slot = cnew_ref[j]

            @pl.when(slot >= 0)
            def _(j=j, slot=slot):
                for c in weight_copies(cexp_ref[j], slot):
                    c.wait()
                nxt = cnext_ref[j]

                @pl.when(nxt >= 0)
                def _():
                    for c in weight_copies(nxt, 1 - slot):
                        c.start()

                wgb[slot] = wg_buf[slot].astype(BF16)
                wub[slot] = wu_buf[slot].astype(BF16)
                wdb[slot] = wd_buf[slot].astype(BF16)

        for j in chunks:
            @pl.when(j + ROWS_AHEAD < n_chunks)
            def _(j=j):
                rows_in(j + ROWS_AHEAD, (j + ROWS_AHEAD) % ROW_SLOTS).start()

        for j in chunks:
            rows_in(j, j % ROW_SLOTS).wait()

        for j in chunks:
            @pl.when(j >= OUT_SLOTS)
            def _(j=j):
                rows_out(j - OUT_SLOTS, j % OUT_SLOTS, wait)

        for j in chunks:
            xs_slot, w_slot, y_slot = j % ROW_SLOTS, cw_ref[j], j % OUT_SLOTS
            x = jnp.concatenate(
                [xbuf[xs_slot, pl.ds(r, bm, stride=ROW_TILES), :] for r in range(ROW_TILES)],
                axis=1)
            xb = x.astype(BF16)
            g = _dot(xb, wgb[w_slot])
            u = _dot(xb, wub[w_slot])
            h = (g * jax.nn.sigmoid(g) * u).astype(BF16)
            y = _dot(h, wdb[w_slot])
            for r in range(ROW_TILES):
                ybuf[y_slot, pl.ds(r, bm, stride=ROW_TILES), :] = y[:, r * LANES:(r + 1) * LANES]

        for j in chunks:
            rows_out(j, j % OUT_SLOTS, start)

        @pl.when(s == n_steps - 1)
        def _():
            for d in range(OUT_SLOTS):
                last = n_chunks - 1 - d

                @pl.when(last >= 0)
                def _(last=last):
                    rows_out(last, last % OUT_SLOTS, wait)


def _experts(cexp, crow, cn, cnew, cnext, cw, nsteps, xs, wg, wu, wd, n_rows):
    bm = EXPERT_BLOCK
    any_spec = pl.BlockSpec(memory_space=pl.ANY)
    grid_spec = pltpu.PrefetchScalarGridSpec(
        num_scalar_prefetch=7,
        grid=(cexp.shape[0] // CHUNKS_PER_STEP,),
        in_specs=[any_spec] * 4,
        out_specs=any_spec,
        scratch_shapes=[pltpu.VMEM((ROW_SLOTS, bm * ROW_TILES, LANES), F32),
                        pltpu.VMEM((OUT_SLOTS, bm * ROW_TILES, LANES), F32),
                        pltpu.VMEM((2, D_MODEL, D_EXPERT), F32),
                        pltpu.VMEM((2, D_MODEL, D_EXPERT), F32),
                        pltpu.VMEM((2, D_EXPERT, D_MODEL), F32),
                        pltpu.VMEM((2, D_MODEL, D_EXPERT), BF16),
                        pltpu.VMEM((2, D_MODEL, D_EXPERT), BF16),
                        pltpu.VMEM((2, D_EXPERT, D_MODEL), BF16),
                        pltpu.SemaphoreType.DMA((ROW_SLOTS,)),
                        pltpu.SemaphoreType.DMA((OUT_SLOTS,)),
                        pltpu.SemaphoreType.DMA((2,))],
    )
    return pl.pallas_call(
        _expert_kernel,
        grid_spec=grid_spec,
        out_shape=jax.ShapeDtypeStruct((n_rows * ROW_TILES, LANES), F32),
        compiler_params=pltpu.CompilerParams(
            dimension_semantics=("arbitrary",), vmem_limit_bytes=VMEM_LIMIT),
        name="experts",
    )(cexp, crow, cn, cnew, cnext, cw, nsteps, xs, wg, wu, wd)


GATHER_SLOTS = 4


def _combine_kernel(dest_ref, dest_next_ref, ys_ref, gate_ref, mid_ref, wsg_ref, wsu_ref,
                    wsd_ref, g2_ref, b2_ref, out_a_ref, out_b_ref, *scratch, alpha, steps_a):
    tt = MOVE_TILE
    ns = GATHER_SLOTS
    per_tile = tt * TOP_K
    g = pl.program_id(0)
    ng = pl.num_programs(0)
    bufs, (gate_scr, routed_scr, sem) = scratch[:ns], scratch[ns:]

    def row_copy(slots_ref, tile, b, t, k):
        src = _row_slice(ys_ref, slots_ref[tile * per_tile + k * tt + t])
        if isinstance(t, int):
            dst = bufs[b].at[pl.ds((k * tt + t) * ROW_TILES, ROW_TILES), :]
        else:
            dst = _row_slice(bufs[b], k * tt + t)
        return pltpu.make_async_copy(src, dst, sem.at[b])

    def wait_rows(b):
        pltpu.make_async_copy(ys_ref.at[pl.ds(0, per_tile * ROW_TILES), :], bufs[b],
                              sem.at[b]).wait()

    @pl.when(g == 0)
    def _():
        for r in range(ns - 1):
            def issue(t, carry, r=r):
                for k in range(TOP_K):
                    row_copy(dest_ref, r, r, t, k).start(priority=k % 2)
                return carry
            lax.fori_loop(0, tt, issue, 0)

    per_block = tt // (ROW_TILES * TOP_K)
    for r in range(ns):
        rows = slice(r * tt, (r + 1) * tt)
        ahead_ref, ahead_tile = (dest_ref, ns - 1) if r == 0 else (dest_next_ref, r - 1)
        ahead_buf = (r + ns - 1) % ns

        mid = mid_ref[rows, :]
        xb = mid.astype(BF16)
        gs = _dot(xb, wsg_ref[...])
        h = (gs * jax.nn.sigmoid(gs) * _dot(xb, wsu_ref[...])).astype(BF16)
        acc = alpha * mid + _dot(h, wsd_ref[...])

        wait_rows(r)
        gates = gate_ref[rows, :]
        for k in range(TOP_K):
            gate_scr[k] = jnp.broadcast_to(gates[:, k:k + 1], (tt, LANES))
        for j in range(ROW_TILES):
            part = jnp.zeros((tt, LANES), F32)
            for k in range(TOP_K):
                part = part + gate_scr[k] * bufs[r][pl.ds(k * tt * ROW_TILES + j, tt,
                                                           stride=ROW_TILES), :]
                t0 = (j * TOP_K + k) * per_block
                for t in range(t0, t0 + per_block):
                    for kk in range(TOP_K):
                        row_copy(ahead_ref, ahead_tile, ahead_buf, t, kk).start(priority=kk % 2)
            routed_scr[:, j * LANES:(j + 1) * LANES] = part
        out = _layer_norm(acc + routed_scr[...], g2_ref[...], b2_ref[...])

        @pl.when(g < steps_a)
        def _(out=out, rows=rows):
            out_a_ref[rows, :] = out

        @pl.when(g >= steps_a)
        def _(out=out, rows=rows):
            out_b_ref[rows, :] = out

    @pl.when(g == ng - 1)
    def _():
        for b in range(ns - 1):
            wait_rows(b)


def _combine(dest_flat, ys, gate, mid, wsg, wsu, wsd, g2, b2, *, alpha, rows_a):
    tt = MOVE_TILE
    rows = tt * GATHER_SLOTS
    tokens = mid.shape[0]
    n = tokens // rows
    steps_a = rows_a // rows
    consts = [wsg, wsu, wsd, g2, b2]
    slots_spec = lambda f: pl.BlockSpec((rows * TOP_K,), f, memory_space=pltpu.SMEM)
    return pl.pallas_call(
        functools.partial(_combine_kernel, alpha=alpha, steps_a=steps_a),
        grid=(n,),
        in_specs=[slots_spec(lambda i: (i,)),
                  slots_spec(lambda i: (jnp.minimum(i + 1, n - 1),)),
                  pl.BlockSpec(memory_space=pl.ANY),
                  pl.BlockSpec((rows, TOP_K), lambda i: (i, 0)),
                  pl.BlockSpec((rows, D_MODEL), lambda i: (i, 0))]
                 + [_const_spec(c.shape) for c in consts],
        out_specs=(pl.BlockSpec((rows, D_MODEL), lambda i: (jnp.minimum(i, steps_a - 1), 0)),
                   pl.BlockSpec((rows, D_MODEL), lambda i: (jnp.maximum(i - steps_a, 0), 0))),
        out_shape=(jax.ShapeDtypeStruct((rows_a, D_MODEL), F32),
                   jax.ShapeDtypeStruct((tokens - rows_a, D_MODEL), F32)),
        scratch_shapes=[pltpu.VMEM((TOP_K * tt * ROW_TILES, LANES), F32)] * GATHER_SLOTS
                       + [pltpu.VMEM((TOP_K, tt, LANES), F32),
                          pltpu.VMEM((tt, D_MODEL), F32),
                          pltpu.SemaphoreType.DMA((GATHER_SLOTS,))],
        compiler_params=pltpu.CompilerParams(
            dimension_semantics=("arbitrary",), vmem_limit_bytes=VMEM_LIMIT),
        name="combine",
    )(dest_flat, dest_flat, ys, gate, mid, *consts)


def _tri(n, *, lower):
    r = lax.broadcasted_iota(jnp.int32, (n, n), 0)
    c = lax.broadcasted_iota(jnp.int32, (n, n), 1)
    return jnp.where((c <= r) if lower else (r <= c), 1.0, 0.0).astype(BF16)


def _moe(mid, rows3, w_router, b_router, w_e_gate, w_e_up, w_e_down, wsg, wsu, wsd, g2, b2, alpha,
         rows_a):
    tokens = mid.shape[0]
    bm = EXPERT_BLOCK
    wr_t = w_router.T
    wr_hi = wr_t.astype(BF16)
    wr_lo = (wr_t - wr_hi.astype(F32)).astype(BF16)
    r = lax.broadcasted_iota(jnp.int32, (ROUTER_TILE, ROUTER_TILE), 0)
    c = lax.broadcasted_iota(jnp.int32, (ROUTER_TILE, ROUTER_TILE), 1)
    before = jnp.where(r < c, 1.0, 0.0).astype(BF16)
    eidx, gate, rank, cnt = _router(mid, wr_hi, wr_lo, b_router.reshape(N_EXPERTS, 1), before)

    i32 = lambda a: a.astype(jnp.int32)
    experts = jnp.arange(N_EXPERTS, dtype=jnp.int32)
    counts = i32(cnt[:, 0])
    starts = jnp.cumsum(counts) - counts
    n_ch = (counts + bm - 1) // bm
    ch_ends = jnp.cumsum(n_ch)
    ch_starts = ch_ends - n_ch
    n_used = ch_ends[-1]
    max_chunks = tokens * TOP_K // bm + N_EXPERTS
    ci = jnp.arange(max_chunks, dtype=jnp.int32)
    cc = jnp.minimum(ci, n_used - 1)
    cexp = jnp.minimum(jnp.sum(i32(ch_ends[None, :] <= cc[:, None]), axis=1), N_EXPERTS - 1)
    hot = cexp[:, None] == experts[None, :]
    lookup = lambda table: jnp.sum(jnp.where(hot, table[None, :], 0), axis=1)
    k_in_expert = cc - lookup(ch_starts)
    crow = lookup(starts) + k_in_expert * bm
    used = ci < n_used
    cn = jnp.where(used, jnp.clip(lookup(counts) - k_in_expert * bm, 0, bm), 0)
    is_first = used & (k_in_expert == 0)
    cw = (jnp.cumsum(i32(is_first)) - 1) % 2
    cnew = jnp.where(is_first, cw, -1)
    later = (experts[None, :] > experts[:, None]) & (n_ch[None, :] > 0)
    next_expert = jnp.min(jnp.where(later, experts[None, :], N_EXPERTS), axis=1)
    cnext = lookup(jnp.where(next_expert < N_EXPERTS, next_expert, -1))
    cnext = jnp.where(is_first, cnext, -1)
    dest = _dest(eidx, rank, starts.astype(F32).reshape(N_EXPERTS, 1)).reshape(tokens * TOP_K)

    xs = _dispatch(dest, rows3)
    n_steps = (n_used + CHUNKS_PER_STEP - 1) // CHUNKS_PER_STEP
    ys = _experts(cexp, i32(crow), i32(cn), i32(cnew), i32(cnext), i32(cw),
                  i32(n_steps).reshape(1), xs, w_e_gate, w_e_up, w_e_down, tokens * TOP_K)
    return _combine(dest, ys, gate.T, mid, wsg, wsu, wsd, g2, b2, alpha=alpha, rows_a=rows_a)


def kernel(x_prompt, x_sample, cache_k, cache_v, cache_logf, state_conv, w_in, b_in, conv_w,
           conv_b, conv_ln_g, conv_ln_b, w_a, w_b, b_b, w_out, ln1_g, ln1_b, w_router, b_router,
           w_e_gate, w_e_up, w_e_down, w_s_gate, w_s_up, w_s_down, ln2_g, ln2_b):
    depth = w_in.shape[0]
    alpha = float((2 * depth) ** 0.25)
    batch, seq, _ = x_prompt.shape
    dbatch, dseq, _ = x_sample.shape
    past = cache_k.shape[2]
    rows_p, rows_s = batch * seq, dbatch * dseq
    total = rows_p + rows_s
    assert seq % IN_TILE == 0 and rows_s % IN_TILE == 0 and seq % MERGE_TILE == 0
    assert total % ROUTER_TILE == 0 and dseq == HIST and rows_p % dseq == 0
    assert total % DEST_TILE == 0 and (total * TOP_K) % EXPERT_BLOCK == 0
    assert rows_p % (MOVE_TILE * GATHER_SLOTS) == 0 and rows_s % (MOVE_TILE * GATHER_SLOTS) == 0

    hp = x_prompt.reshape(rows_p, D_MODEL)
    hs = x_sample.reshape(rows_s, D_MODEL)
    tri_in = _tri(IN_TILE, lower=True)
    upper_past = _tri(past, lower=False)
    row2 = lambda a: a.reshape(1, -1)
    outs = {n: [] for n in ("kp", "vp", "fp", "cp", "ks", "vs", "fs", "cs")}

    for l in range(depth):
        w = w_in[l]
        b = b_in[l]
        main_cols = lambda a: jnp.concatenate([a[..., :OFF_F], a[..., OFF_GLU:]], axis=-1)
        w_main = main_cols(w).astype(BF16)
        b_main = row2(main_cols(b))
        w_f = jnp.pad(w[:, OFF_F:OFF_GLU], ((0, 0), (0, LANES - FOX_HEADS))).astype(BF16)
        b_f = row2(jnp.pad(b[OFF_F:OFF_GLU], (0, LANES - FOX_HEADS)))
        cw = jnp.pad(conv_w[l], ((0, 1), (0, 0)))
        conv_p = (cw, row2(conv_b[l]), row2(conv_ln_g[l]), row2(conv_ln_b[l]),
                  w_b[l].astype(BF16), row2(b_b[l]))
        wa, wout = w_a[l].astype(BF16), w_out[l].astype(BF16)
        g1, b1 = row2(ln1_g[l]), row2(ln1_b[l])

        q, k, v, kb, vb, logf, c, u, sa, sb = _inproj(
            hp, w_main, b_main, w_f, b_f, tri_in, tiles_per_seq=seq // IN_TILE)
        c_row = c.reshape(batch, seq, FOX_HEADS).transpose(0, 2, 1)
        attn = _attn_prompt(q, kb, vb, c_row, batch=batch, seq=seq)
        mid, rows3 = _merge(u, u, attn, sa, sb, hp, conv_p, wa, wout, g1, b1,
                            n_seq=batch, seq=seq, ts=MERGE_TILE, parts=MERGE_PARTS,
                            hist_from_u=True, alpha=alpha, total_rows=total, row_offset=0)
        outs["kp"].append(k.reshape(batch, seq, FOX_HEADS, HEAD_DIM))
        outs["vp"].append(v.reshape(batch, seq, FOX_HEADS, HEAD_DIM))
        outs["fp"].append(logf.reshape(batch, seq, FOX_HEADS))
        outs["cp"].append(u.reshape(batch, seq, CONV_CH)[:, seq - (CONV_WIDTH - 1):])

        q, k, v, kb, vb, logf, _, u, sa, sb = _inproj(
            hs, w_main, b_main, w_f, b_f, tri_in, tiles_per_seq=1)
        logf_t = logf.reshape(dbatch, dseq, FOX_HEADS).transpose(0, 2, 1)
        attn = _attn_sample(
            q, kb, vb, logf, logf_t, cache_k[l].reshape(dbatch, past, FOX_WIDTH),
            cache_v[l].reshape(dbatch, past, FOX_WIDTH), cache_logf[l].transpose(0, 2, 1),
            upper_past, batch=dbatch, t=dseq, past=past)
        hist = jnp.pad(state_conv[l], ((0, 0), (HIST - (CONV_WIDTH - 1), 0), (0, 0)))
        mid, rows3 = _merge(u, hist.reshape(dbatch * HIST, CONV_CH), attn, sa, sb, hs, conv_p,
                            wa, wout, g1, b1, n_seq=dbatch, seq=dseq, ts=dseq, parts=1,
                            hist_from_u=False,
                            alpha=alpha, total_rows=total, row_offset=rows_p, prev=(mid, rows3))
        outs["ks"].append(k.reshape(dbatch, dseq, FOX_HEADS, HEAD_DIM))
        outs["vs"].append(v.reshape(dbatch, dseq, FOX_HEADS, HEAD_DIM))
        outs["fs"].append(logf.reshape(dbatch, dseq, FOX_HEADS))
        u3 = u.reshape(dbatch, dseq, CONV_CH)
        u_ext = jnp.concatenate([state_conv[l], u3], axis=1)
        outs["cs"].append(u_ext[:, -(CONV_WIDTH - 1):])

        hp, hs = _moe(mid, rows3, w_router[l], b_router[l], w_e_gate[l], w_e_up[l], w_e_down[l],
                      w_s_gate[l].astype(BF16), w_s_up[l].astype(BF16), w_s_down[l].astype(BF16),
                      row2(ln2_g[l]), row2(ln2_b[l]), alpha, rows_p)

    st = lambda n: jnp.stack(outs[n])
    return (hp.reshape(batch, seq, D_MODEL), hs.reshape(dbatch, dseq, D_MODEL),
            st("kp"), st("vp"), st("fp"), st("cp"), st("ks"), st("vs"), st("fs"), st("cs"))
```

```python
import functools

import jax
import jax.numpy as jnp
from jax import lax
from jax.experimental import pallas as pl
from jax.experimental.pallas import tpu as pltpu

D_MODEL = 1024
FOX_HEADS = 8
HEAD_DIM = 64
FOX_WIDTH = FOX_HEADS * HEAD_DIM
ATTN_SCALE = HEAD_DIM ** -0.5
LOG2E = 1.4426950408889634
CONV_CH = D_MODEL // 2
CONV_WIDTH = 31
N_EXPERTS = 256
TOP_K = 8
N_GROUPS = 8
GROUP_SIZE = N_EXPERTS // N_GROUPS
TOPK_GROUPS = 4
D_EXPERT = D_MODEL // 4
ROUTED_SCALE = 2.5
LN_EPS = 1e-5

OFF_K = FOX_WIDTH
OFF_V = 2 * FOX_WIDTH
OFF_F = 3 * FOX_WIDTH
OFF_GLU = OFF_F + FOX_HEADS
OFF_GA = OFF_GLU + 2 * CONV_CH
OFF_GB = OFF_GA + D_MODEL

LANES = 128
SUBLANES = 8
ROW_TILES = D_MODEL // LANES
VMEM_LIMIT = 56 * 1024 * 1024

IN_TILE = 512
ATTN_TILE = 256
MERGE_TILE = 512
MERGE_PARTS = 2
CONV_SUB = 32
HIST = 32
ROUTER_TILE = 512
MOVE_TILE = 128
DEST_TILE = 1024
DISPATCH_SLABS = 4
EXPERT_BLOCK = 256

F32 = jnp.float32
BF16 = jnp.bfloat16
NEG_INF = float("-inf")
NT_DIMS = (((1,), (1,)), ((), ()))


def _const_spec(shape):
    nd = len(shape)
    return pl.BlockSpec(shape, lambda *_: (0,) * nd, pipeline_mode=pl.Buffered(1))


def _split3(x):
    hi = x.astype(BF16)
    r1 = x - hi.astype(F32)
    mid = r1.astype(BF16)
    lo = (r1 - mid.astype(F32)).astype(BF16)
    return hi, mid, lo


def _dot(a, b):
    return jnp.dot(a, b, preferred_element_type=F32)


def _dot_nt(a, b):
    return lax.dot_general(a, b, NT_DIMS, preferred_element_type=F32)


def _exact_dot(ones_mat, x, *, ones_on_left):
    acc = None
    for part in _split3(x):
        term = _dot(ones_mat, part) if ones_on_left else _dot(part, ones_mat)
        acc = term if acc is None else acc + term
    return acc


def _layer_norm(x, g, b):
    mu = jnp.mean(x, axis=-1, keepdims=True)
    xc = x - mu
    var = jnp.mean(xc * xc, axis=-1, keepdims=True)
    return xc * lax.rsqrt(var + LN_EPS) * g + b


def _log_sigmoid(z):
    return jnp.minimum(z, 0.0) - jnp.log1p(jnp.exp(-jnp.abs(z)))


def _inproj_kernel(x_ref, w_ref, b_ref, wf_ref, bf_ref, tri_ref,
                   q_ref, k_ref, v_ref, kb_ref, vb_ref, logf_ref, c_ref, u_ref,
                   sa_ref, sb_ref, carry_ref, *, tiles_per_seq):
    i = pl.program_id(0)
    xb = x_ref[...].astype(BF16)

    def proj(c0, c1):
        return _dot(xb, w_ref[:, c0:c1]) + b_ref[:, c0:c1]

    q = proj(0, FOX_WIDTH)
    q_ref[...] = (q * (ATTN_SCALE * LOG2E)).astype(BF16)
    k = proj(FOX_WIDTH, 2 * FOX_WIDTH)
    kb_ref[...] = k.astype(BF16)
    v = proj(2 * FOX_WIDTH, 3 * FOX_WIDTH)
    vb_ref[...] = v.astype(BF16)
    for h in range(FOX_HEADS):
        head_rows = pl.ds(h, IN_TILE, stride=FOX_HEADS)
        k_ref[head_rows, :] = k[:, h * HEAD_DIM:(h + 1) * HEAD_DIM]
        v_ref[head_rows, :] = v[:, h * HEAD_DIM:(h + 1) * HEAD_DIM]

    logf = _log_sigmoid(_dot(xb, wf_ref[...]) + bf_ref[...])
    logf_ref[...] = logf[:, :FOX_HEADS]

    @pl.when(i % tiles_per_seq == 0)
    def _():
        carry_ref[...] = jnp.zeros_like(carry_ref)

    c = _exact_dot(tri_ref[...], logf, ones_on_left=True) + carry_ref[...]
    c_ref[...] = c[:, :FOX_HEADS]
    carry_ref[...] = c[IN_TILE - 1:IN_TILE, :]

    g0 = 3 * FOX_WIDTH
    glu_a = proj(g0, g0 + CONV_CH)
    glu_b = proj(g0 + CONV_CH, g0 + 2 * CONV_CH)
    u_ref[...] = glu_a * jax.nn.sigmoid(glu_b)
    g1 = g0 + 2 * CONV_CH
    sa_ref[...] = jax.nn.sigmoid(proj(g1, g1 + D_MODEL)).astype(BF16)
    sb_ref[...] = jax.nn.sigmoid(proj(g1 + D_MODEL, g1 + 2 * D_MODEL)).astype(BF16)


def _inproj(x, w_main, b_main, w_f, b_f, tri, *, tiles_per_seq):
    rows = x.shape[0]
    n_main = w_main.shape[1]
    row_spec = lambda w: pl.BlockSpec((IN_TILE, w), lambda i: (i, 0))
    out_shapes = (
        jax.ShapeDtypeStruct((rows, FOX_WIDTH), BF16),
        jax.ShapeDtypeStruct((rows * FOX_HEADS, HEAD_DIM), F32),
        jax.ShapeDtypeStruct((rows * FOX_HEADS, HEAD_DIM), F32),
        jax.ShapeDtypeStruct((rows, FOX_WIDTH), BF16),
        jax.ShapeDtypeStruct((rows, FOX_WIDTH), BF16),
        jax.ShapeDtypeStruct((rows, FOX_HEADS), F32),
        jax.ShapeDtypeStruct((rows, FOX_HEADS), F32),
        jax.ShapeDtypeStruct((rows, CONV_CH), F32),
        jax.ShapeDtypeStruct((rows, D_MODEL), BF16),
        jax.ShapeDtypeStruct((rows, D_MODEL), BF16),
    )
    head_spec = pl.BlockSpec((IN_TILE * FOX_HEADS, HEAD_DIM), lambda i: (i, 0))
    out_specs = (row_spec(FOX_WIDTH), head_spec, head_spec) + (row_spec(FOX_WIDTH),) * 2 + (
        row_spec(FOX_HEADS),) * 2 + (row_spec(CONV_CH), row_spec(D_MODEL), row_spec(D_MODEL))
    return pl.pallas_call(
        functools.partial(_inproj_kernel, tiles_per_seq=tiles_per_seq),
        grid=(rows // IN_TILE,),
        in_specs=[row_spec(D_MODEL), _const_spec((D_MODEL, n_main)), _const_spec((1, n_main)),
                  _const_spec((D_MODEL, LANES)), _const_spec((1, LANES)),
                  _const_spec((IN_TILE, IN_TILE))],
        out_specs=out_specs,
        out_shape=out_shapes,
        scratch_shapes=[pltpu.VMEM((1, LANES), F32)],
        compiler_params=pltpu.CompilerParams(
            dimension_semantics=("arbitrary",), vmem_limit_bytes=VMEM_LIMIT),
        name="inproj",
    )(x, w_main, b_main, w_f, b_f, tri)


def _attn_prompt_kernel(q_ref, k_ref, v_ref, crow_ref, o_ref, qm_scr, m_scr, acc_scr):
    t = ATTN_TILE
    i = pl.program_id(1)
    lane = lax.broadcasted_iota(jnp.int32, (t, LANES), 1)
    row = lax.broadcasted_iota(jnp.int32, (t, t), 0)
    col = lax.broadcasted_iota(jnp.int32, (t, t), 1)
    causal = col <= row
    wide = lambda x: jnp.concatenate([x] * (t // LANES), axis=1)
    pair_lanes = lambda pair: slice(pair * LANES, (pair + 1) * LANES)

    for h in range(FOX_HEADS):
        qp = q_ref[:, pair_lanes(h // 2)]
        in_head = (lane < HEAD_DIM) if h % 2 == 0 else (lane >= HEAD_DIM)
        qm_scr[h] = jnp.where(in_head, qp, jnp.zeros_like(qp))
        m_scr[h] = jnp.full((t, LANES), NEG_INF, F32)
        acc_scr[h] = jnp.zeros((t, 2 * LANES), F32)
    ones = jnp.ones((t, LANES), BF16)

    def step(j, masked):
        r0 = pl.multiple_of(j * t, t)
        ck = crow_ref[0, :, pl.ds(r0, t)] * LOG2E
        for pair in range(FOX_HEADS // 2):
            kj = k_ref[pl.ds(r0, t), pair_lanes(pair)]
            vj = jnp.concatenate([v_ref[pl.ds(r0, t), pair_lanes(pair)], ones], axis=1)
            for h in (2 * pair, 2 * pair + 1):
                s = _dot_nt(qm_scr[h], kj) - ck[h:h + 1, :]
                if masked:
                    s = jnp.where(causal, s, NEG_INF)
                m_prev = m_scr[h]
                m_new = jnp.maximum(m_prev, jnp.max(s, axis=1, keepdims=True))
                alpha = jnp.exp2(m_prev - m_new)
                p = jnp.exp2(s - wide(m_new))
                acc_scr[h] = wide(alpha) * acc_scr[h] + _dot(p.astype(BF16), vj)
                m_scr[h] = m_new

    def body(jj, carry):
        step(2 * jj, False)
        step(2 * jj + 1, False)
        return carry

    lax.fori_loop(0, i // 2, body, 0)

    @pl.when(i % 2 == 1)
    def _():
        step(i - 1, False)

    step(i, True)
    for pair in range(FOX_HEADS // 2):
        o0, o1 = (acc_scr[h, :, :LANES] / acc_scr[h, :, LANES:] for h in (2 * pair, 2 * pair + 1))
        o_ref[:, pair_lanes(pair)] = jnp.where(lane < HEAD_DIM, o0, o1).astype(BF16)


def _attn_prompt(q, kb, vb, c_row, *, batch, seq):
    t = ATTN_TILE
    nq = seq // t
    return pl.pallas_call(
        _attn_prompt_kernel,
        grid=(batch, nq),
        in_specs=[
            pl.BlockSpec((t, FOX_WIDTH), lambda b, i: (b * nq + i, 0)),
            pl.BlockSpec((seq, FOX_WIDTH), lambda b, i: (b, 0)),
            pl.BlockSpec((seq, FOX_WIDTH), lambda b, i: (b, 0)),
            pl.BlockSpec((1, FOX_HEADS, seq), lambda b, i: (b, 0, 0)),
        ],
        out_specs=pl.BlockSpec((t, FOX_WIDTH), lambda b, i: (b * nq + i, 0)),
        out_shape=jax.ShapeDtypeStruct((batch * seq, FOX_WIDTH), BF16),
        scratch_shapes=[pltpu.VMEM((FOX_HEADS, t, LANES), BF16),
                        pltpu.VMEM((FOX_HEADS, t, LANES), F32),
                        pltpu.VMEM((FOX_HEADS, t, 2 * LANES), F32)],
        compiler_params=pltpu.CompilerParams(
            dimension_semantics=("arbitrary", "arbitrary"), vmem_limit_bytes=VMEM_LIMIT),
        name="attn_prompt",
    )(q, kb, vb, c_row)


def _attn_sample_kernel(q_ref, kn_ref, vn_ref, lf_ref, lft_ref, ck_ref, cv_ref, clft_ref,
                        upper_ref, o_ref, *, t, past):
    rows = FOX_HEADS * t
    lane_head = lax.broadcasted_iota(jnp.int32, (t, FOX_WIDTH), 1) // HEAD_DIM
    q = q_ref[...]
    q_stack = jnp.concatenate(
        [jnp.where(lane_head == h, q, jnp.zeros_like(q)) for h in range(FOX_HEADS)], axis=0)

    clf = clft_ref[0]
    prefix = _exact_dot(upper_ref[...], clf, ones_on_left=False)
    to_end = prefix[:, past - 1:past] - prefix
    ri = lax.broadcasted_iota(jnp.int32, (t, t), 0)
    ci = lax.broadcasted_iota(jnp.int32, (t, t), 1)
    lower = jnp.where(ci <= ri, 1.0, 0.0).astype(BF16)
    upper = jnp.where(ri <= ci, 1.0, 0.0).astype(BF16)
    cn_col = _exact_dot(lower, lf_ref[...], ones_on_left=True)
    cn_row = _exact_dot(upper, lft_ref[0], ones_on_left=False)

    stack = lambda f: jnp.concatenate([f(h) for h in range(FOX_HEADS)], axis=0)
    cn_stack = stack(lambda h: cn_col[:, h:h + 1])
    bias_c = stack(lambda h: jnp.broadcast_to(to_end[h:h + 1, :], (t, past)))
    bias_n = stack(lambda h: jnp.broadcast_to(cn_row[h:h + 1, :], (t, t)))

    kc = ck_ref[0].astype(BF16)
    vc = cv_ref[0].astype(BF16)
    s_c = _dot_nt(q_stack, kc) + (bias_c + cn_stack) * LOG2E
    s_n = _dot_nt(q_stack, kn_ref[...]) + (cn_stack - bias_n) * LOG2E
    tq = lax.broadcasted_iota(jnp.int32, (rows, t), 0) % t
    tk = lax.broadcasted_iota(jnp.int32, (rows, t), 1)
    s_n = jnp.where(tk <= tq, s_n, NEG_INF)
    m = jnp.maximum(jnp.max(s_c, axis=1, keepdims=True), jnp.max(s_n, axis=1, keepdims=True))
    p_c = jnp.exp2(s_c - m)
    p_n = jnp.exp2(s_n - m)
    denom = jnp.sum(p_c, axis=1, keepdims=True) + jnp.sum(p_n, axis=1, keepdims=True)
    o_stack = (_dot(p_c.astype(BF16), vc) + _dot(p_n.astype(BF16), vn_ref[...])) / denom
    out = jnp.zeros((t, FOX_WIDTH), F32)
    for h in range(FOX_HEADS):
        out = out + jnp.where(lane_head == h, o_stack[h * t:(h + 1) * t, :], 0.0)
    o_ref[...] = out.astype(BF16)


def _attn_sample(q, kb, vb, logf, logf_t, cache_k, cache_v, cache_logf_t, upper, *, batch, t, past):
    row_spec = lambda w: pl.BlockSpec((t, w), lambda b: (b, 0))
    return pl.pallas_call(
        functools.partial(_attn_sample_kernel, t=t, past=past),
        grid=(batch,),
        in_specs=[
            row_spec(FOX_WIDTH), row_spec(FOX_WIDTH), row_spec(FOX_WIDTH), row_spec(FOX_HEADS),
            pl.BlockSpec((1, FOX_HEADS, t), lambda b: (b, 0, 0)),
            pl.BlockSpec((1, past, FOX_WIDTH), lambda b: (b, 0, 0)),
            pl.BlockSpec((1, past, FOX_WIDTH), lambda b: (b, 0, 0)),
            pl.BlockSpec((1, FOX_HEADS, past), lambda b: (b, 0, 0)),
            _const_spec((past, past)),
        ],
        out_specs=row_spec(FOX_WIDTH),
        out_shape=jax.ShapeDtypeStruct((batch * t, FOX_WIDTH), BF16),
        compiler_params=pltpu.CompilerParams(
            dimension_semantics=("arbitrary",), vmem_limit_bytes=VMEM_LIMIT),
        name="attn_sample",
    )(q, kb, vb, logf, logf_t, cache_k, cache_v, cache_logf_t, upper)


def _merge_kernel(u_ref, hist_ref, attn_ref, sa_ref, sb_ref, x_ref,
                  cw_ref, cb_ref, cg_ref, cbeta_ref, wb_ref, bb_ref, wa_ref, wout_ref,
                  g1_ref, b1_ref, *rest, ts, parts, tiles_per_seq, n_tiles, zero_first_hist, alpha,
                  aliased):
    if aliased:
        rest = rest[2:]
    mid_ref, mid3_ref = rest[:2]
    scratch = rest[2:]
    i = pl.program_id(0)
    th = ts // parts

    @pl.when(i >= n_tiles)
    def _():
        mid_ref[...] = jnp.zeros_like(mid_ref)
        mid3_ref[...] = jnp.zeros_like(mid3_ref)

    @pl.when(i < n_tiles)
    def _():
        for p in range(parts):
            r0 = p * th
            if p == 0:
                hist = hist_ref[...]
                if zero_first_hist:
                    hist = jnp.where(i % tiles_per_seq == 0, 0.0, hist)
            else:
                hist = u_ref[r0 - HIST:r0, :]
            _merge_rows(hist, r0, th, u_ref, attn_ref, sa_ref, sb_ref, x_ref, cw_ref, cb_ref,
                        cg_ref, cbeta_ref, wb_ref, bb_ref, wa_ref, wout_ref, g1_ref, b1_ref,
                        mid_ref, mid3_ref, *scratch[3 * p:3 * p + 3], alpha=alpha)


def _merge_rows(hist, r0, th, u_ref, attn_ref, sa_ref, sb_ref, x_ref, cw_ref, cb_ref, cg_ref,
                cbeta_ref, wb_ref, bb_ref, wa_ref, wout_ref, g1_ref, b1_ref, mid_ref, mid3_ref,
                ue_scr, ph_scr, h_scr, *, alpha):
    rows = slice(r0, r0 + th)
    ue_scr[0:HIST, :] = hist
    ue_scr[HIST:HIST + th, :] = u_ref[rows, :]
    lead = HIST - (CONV_WIDTH - 1)
    for r in range(SUBLANES):
        span = th + (CONV_WIDTH - 1 - r) // SUBLANES * SUBLANES
        ph_scr[r, 0:span, :] = ue_scr[lead + r:lead + r + span, :]

    for rt in range(th // CONV_SUB):
        base = rt * CONV_SUB
        acc = jnp.zeros((CONV_SUB, CONV_CH), F32)
        for j in range(CONV_WIDTH):
            r, a = j % SUBLANES, j // SUBLANES
            start = base + a * SUBLANES
            acc = acc + cw_ref[j:j + 1, :] * ph_scr[r, start:start + CONV_SUB, :]
        h_scr[base:base + CONV_SUB, :] = acc

    h = _layer_norm(h_scr[...] + cb_ref[...], cg_ref[...], cbeta_ref[...])
    h = h * jax.nn.sigmoid(h)
    conv_out = _dot(h.astype(BF16), wb_ref[...]) + bb_ref[...]
    attn_out = _dot(attn_ref[rows, :], wa_ref[...])
    m = sa_ref[rows, :].astype(F32) * attn_out + sb_ref[rows, :].astype(F32) * conv_out
    z = alpha * x_ref[rows, :] + _dot(m.astype(BF16), wout_ref[...])
    mid = _layer_norm(z, g1_ref[...], b1_ref[...])
    mid_ref[rows, :] = mid
    for j in range(ROW_TILES):
        mid3_ref[pl.ds(r0 * ROW_TILES + j, th, stride=ROW_TILES), :] = (
            mid[:, j * LANES:(j + 1) * LANES])


def _merge(u, hist, attn, sa, sb, x, conv_p, wa, wout, g1, b1, *, n_seq, seq, ts, parts,
           hist_from_u, alpha, total_rows, row_offset, prev=None):
    nt = seq // ts
    n_tiles = n_seq * nt
    off = row_offset // ts
    grid_tiles = n_tiles if prev is not None else total_rows // ts
    src = lambda i: jnp.minimum(i, n_tiles - 1)
    row_spec = lambda w: pl.BlockSpec((ts, w), lambda i: (src(i), 0))
    if hist_from_u:
        per = ts // HIST
        hist_spec = pl.BlockSpec((HIST, CONV_CH), lambda i: (jnp.maximum(src(i) * per - 1, 0), 0))
    else:
        hist_spec = pl.BlockSpec((HIST, CONV_CH), lambda i: (src(i) // nt, 0))
    cw, cb, cg, cbeta, wb, bb = conv_p
    consts = [cw, cb, cg, cbeta, wb, bb, wa, wout, g1, b1]
    in_specs = [row_spec(CONV_CH), hist_spec, row_spec(FOX_WIDTH), row_spec(D_MODEL),
                row_spec(D_MODEL), row_spec(D_MODEL)] + [_const_spec(c.shape) for c in consts]
    args = [u, hist, attn, sa, sb, x] + consts
    aliases = {}
    if prev is not None:
        in_specs += [pl.BlockSpec(memory_space=pl.ANY)] * 2
        aliases = {len(args): 0, len(args) + 1: 1}
        args += list(prev)
    th = ts // parts
    span = th + (CONV_WIDTH - 1) // SUBLANES * SUBLANES
    return pl.pallas_call(
        functools.partial(_merge_kernel, ts=ts, parts=parts, tiles_per_seq=nt, n_tiles=n_tiles,
                          zero_first_hist=hist_from_u, alpha=alpha, aliased=prev is not None),
        grid=(grid_tiles,),
        in_specs=in_specs,
        out_specs=(pl.BlockSpec((ts, D_MODEL), lambda i: (off + i, 0)),
                   pl.BlockSpec((ts * ROW_TILES, LANES), lambda i: (off + i, 0))),
        out_shape=(jax.ShapeDtypeStruct((total_rows, D_MODEL), F32),
                   jax.ShapeDtypeStruct((total_rows * ROW_TILES, LANES), F32)),
        scratch_shapes=[pltpu.VMEM((HIST + th, CONV_CH), F32),
                        pltpu.VMEM((SUBLANES, span, CONV_CH), F32),
                        pltpu.VMEM((th, CONV_CH), F32)] * parts,
        input_output_aliases=aliases,
        compiler_params=pltpu.CompilerParams(
            dimension_semantics=("arbitrary",), vmem_limit_bytes=VMEM_LIMIT),
        name="merge",
    )(*args)


def _router_kernel(mid_ref, wr_hi_ref, wr_lo_ref, br_ref, before_ref,
                   eidx_ref, gate_ref, rank_ref, cnt_ref, carry_scr):
    tr = ROUTER_TILE
    i = pl.program_id(0)

    @pl.when(i == 0)
    def _():
        carry_scr[...] = jnp.zeros_like(carry_scr)

    x = mid_ref[...]
    x_hi = x.astype(BF16)
    x_lo = (x - x_hi.astype(F32)).astype(BF16)
    wr_hi = wr_hi_ref[...]
    logits = _dot_nt(wr_hi, x_hi) + _dot_nt(wr_hi, x_lo) + _dot_nt(wr_lo_ref[...], x_hi)
    scores = jax.nn.sigmoid(logits)
    sel = scores + br_ref[...]

    sel3 = sel.reshape(N_GROUPS, GROUP_SIZE, tr)
    in_group = lax.broadcasted_iota(jnp.int32, sel3.shape, 1)
    m1 = jnp.max(sel3, axis=1, keepdims=True)
    first = jnp.min(jnp.where(sel3 == m1, in_group, GROUP_SIZE), axis=1, keepdims=True)
    m2 = jnp.max(jnp.where(in_group == first, NEG_INF, sel3), axis=1, keepdims=True)
    gs = m1 + m2
    gi = lax.broadcasted_iota(jnp.int32, gs.shape, 0)
    beaten = jnp.zeros(gs.shape, F32)
    for g in range(N_GROUPS):
        other = gs[g:g + 1]
        wins = (other > gs) | ((other == gs) & (g < gi))
        beaten = beaten + jnp.where(wins, 1.0, 0.0)
    drop = jnp.where(beaten < TOPK_GROUPS, 0.0, NEG_INF)
    cur = (sel3 + drop).reshape(N_EXPERTS, tr)

    ei = lax.broadcasted_iota(jnp.int32, (N_EXPERTS, tr), 0)
    idxs, vals = [], []
    picked = jnp.zeros((N_EXPERTS, tr), F32)
    for _ in range(TOP_K):
        m = jnp.max(cur, axis=0, keepdims=True)
        idx = jnp.min(jnp.where(cur == m, ei, N_EXPERTS), axis=0, keepdims=True)
        hit = ei == idx
        vals.append(jnp.sum(jnp.where(hit, scores, 0.0), axis=0, keepdims=True))
        idxs.append(idx)
        picked = picked + jnp.where(hit, 1.0, 0.0)
        cur = jnp.where(hit, NEG_INF, cur)

    total = vals[0]
    for v in vals[1:]:
        total = total + v
    for k in range(TOP_K):
        gate_ref[k:k + 1, :] = vals[k] / total * ROUTED_SCALE
        eidx_ref[k:k + 1, :] = idxs[k]

    ahead = _dot(picked.astype(BF16), before_ref[...]) + carry_scr[:, 0:1]
    for k in range(TOP_K):
        rank = jnp.sum(jnp.where(ei == idxs[k], ahead, 0.0), axis=0, keepdims=True)
        rank_ref[k:k + 1, :] = rank.astype(jnp.int32)
    carry_scr[...] = carry_scr[...] + jnp.sum(picked, axis=1, keepdims=True)
    cnt_ref[...] = carry_scr[...]


def _router(mid, wr_hi, wr_lo, br, before):
    tr = ROUTER_TILE
    tokens = mid.shape[0]
    tok_spec = pl.BlockSpec((TOP_K, tr), lambda i: (0, i))
    return pl.pallas_call(
        _router_kernel,
        grid=(tokens // tr,),
        in_specs=[pl.BlockSpec((tr, D_MODEL), lambda i: (i, 0)),
                  _const_spec((N_EXPERTS, D_MODEL)), _const_spec((N_EXPERTS, D_MODEL)),
                  _const_spec((N_EXPERTS, 1)), _const_spec((tr, tr))],
        out_specs=(tok_spec, tok_spec, tok_spec,
                   pl.BlockSpec((N_EXPERTS, LANES), lambda i: (0, 0))),
        out_shape=(jax.ShapeDtypeStruct((TOP_K, tokens), jnp.int32),
                   jax.ShapeDtypeStruct((TOP_K, tokens), F32),
                   jax.ShapeDtypeStruct((TOP_K, tokens), jnp.int32),
                   jax.ShapeDtypeStruct((N_EXPERTS, LANES), F32)),
        scratch_shapes=[pltpu.VMEM((N_EXPERTS, LANES), F32)],
        compiler_params=pltpu.CompilerParams(
            dimension_semantics=("arbitrary",), vmem_limit_bytes=VMEM_LIMIT),
        name="router",
    )(mid, wr_hi, wr_lo, br, before)


def _dest_kernel(eidx_ref, rank_ref, starts_ref, dest_ref):
    tt = MOVE_TILE
    tokens = eidx_ref.shape[1]
    ei = lax.broadcasted_iota(jnp.int32, (N_EXPERTS, tokens), 0)
    starts = starts_ref[...]
    for k in range(TOP_K):
        hit = ei == eidx_ref[k:k + 1, :]
        start = jnp.sum(jnp.where(hit, starts, 0.0), axis=0, keepdims=True)
        dest = start.astype(jnp.int32) + rank_ref[k:k + 1, :]
        for c in range(tokens // tt):
            dest_ref[c, k:k + 1, :] = dest[:, c * tt:(c + 1) * tt]


def _dest(eidx, rank, starts):
    tt = MOVE_TILE
    tokens = eidx.shape[1]
    step = DEST_TILE
    tok_spec = pl.BlockSpec((TOP_K, step), lambda i: (0, i))
    return pl.pallas_call(
        _dest_kernel,
        grid=(tokens // step,),
        in_specs=[tok_spec, tok_spec, _const_spec((N_EXPERTS, 1))],
        out_specs=pl.BlockSpec((step // tt, TOP_K, tt), lambda i: (i, 0, 0)),
        out_shape=jax.ShapeDtypeStruct((tokens // tt, TOP_K, tt), jnp.int32),
        compiler_params=pltpu.CompilerParams(dimension_semantics=("arbitrary",)),
        name="dest",
    )(eidx, rank, starts)


def _row_slice(ref, row):
    return ref.at[pl.ds(pl.multiple_of(row * ROW_TILES, ROW_TILES), ROW_TILES), :]


def _dispatch_kernel(dest_ref, rows_ref, xs_ref, zeros_scr, sem, *, n_rows):
    tt = MOVE_TILE

    @pl.when(pl.program_id(0) == 0)
    def _():
        zeros_scr[...] = jnp.zeros_like(zeros_scr)
        tail = pltpu.make_async_copy(
            zeros_scr, xs_ref.at[pl.ds(n_rows * ROW_TILES, EXPERT_BLOCK * ROW_TILES), :], sem)
        tail.start()
        tail.wait()

    for slab in range(DISPATCH_SLABS):
        def issue(t, carry, slab=slab):
            src = _row_slice(rows_ref, slab * tt + t)
            for k in range(TOP_K):
                slot = dest_ref[slab * tt * TOP_K + k * tt + t]
                pltpu.make_async_copy(src, _row_slice(xs_ref, slot), sem).start(priority=k % 2)
            return carry

        lax.fori_loop(0, tt, issue, 0)
    for _ in range(TOP_K):
        pltpu.make_async_copy(
            rows_ref, xs_ref.at[pl.ds(0, DISPATCH_SLABS * tt * ROW_TILES), :], sem).wait()


def _dispatch(dest_flat, rows3):
    tt = MOVE_TILE * DISPATCH_SLABS
    tokens = rows3.shape[0] // ROW_TILES
    n_rows = tokens * TOP_K
    return pl.pallas_call(
        functools.partial(_dispatch_kernel, n_rows=n_rows),
        grid=(tokens // tt,),
        in_specs=[pl.BlockSpec((tt * TOP_K,), lambda i: (i,), memory_space=pltpu.SMEM),
                  pl.BlockSpec((tt * ROW_TILES, LANES), lambda i: (i, 0))],
        out_specs=pl.BlockSpec(memory_space=pl.ANY),
        out_shape=jax.ShapeDtypeStruct(((n_rows + EXPERT_BLOCK) * ROW_TILES, LANES), F32),
        scratch_shapes=[pltpu.VMEM((EXPERT_BLOCK * ROW_TILES, LANES), F32),
                        pltpu.SemaphoreType.DMA(())],
        compiler_params=pltpu.CompilerParams(dimension_semantics=("arbitrary",)),
        name="dispatch",
    )(dest_flat, rows3)


_PIECES = tuple(EXPERT_BLOCK >> s for s in range(EXPERT_BLOCK.bit_length()))
CHUNKS_PER_STEP = 2
ROWS_AHEAD = 6
ROW_SLOTS = ROWS_AHEAD + CHUNKS_PER_STEP
OUT_SLOTS = 2 * CHUNKS_PER_STEP


def _expert_kernel(cexp_ref, crow_ref, cn_ref, cnew_ref, cnext_ref, cw_ref, nsteps_ref,
                   xs_hbm, wg_hbm, wu_hbm, wd_hbm, ys_hbm,
                   xbuf, ybuf, wg_buf, wu_buf, wd_buf, wgb, wub, wdb, xsem, ysem, wsem):
    bm = EXPERT_BLOCK
    s = pl.program_id(0)
    n_steps = nsteps_ref[0]
    n_chunks = n_steps * CHUNKS_PER_STEP
    chunks = [s * CHUNKS_PER_STEP + c for c in range(CHUNKS_PER_STEP)]

    def weight_copies(e, s):
        return [pltpu.make_async_copy(hbm.at[e], buf.at[s], wsem.at[s])
                for hbm, buf in ((wg_hbm, wg_buf), (wu_hbm, wu_buf), (wd_hbm, wd_buf))]

    def rows_in(j, s):
        first = pl.multiple_of(crow_ref[j] * ROW_TILES, ROW_TILES)
        return pltpu.make_async_copy(xs_hbm.at[pl.ds(first, bm * ROW_TILES), :], xbuf.at[s],
                                     xsem.at[s])

    def rows_out(j, s, act):
        n, row0 = cn_ref[j], crow_ref[j]

        def piece(done, p):
            src = pl.multiple_of(done * ROW_TILES, ROW_TILES)
            dst = pl.multiple_of((row0 + done) * ROW_TILES, ROW_TILES)
            act(pltpu.make_async_copy(ybuf.at[s, pl.ds(src, p * ROW_TILES), :],
                                      ys_hbm.at[pl.ds(dst, p * ROW_TILES), :], ysem.at[s]))

        @pl.when(n == bm)
        def _():
            piece(jnp.int32(0), bm)

        @pl.when(n < bm)
        def _():
            done = jnp.int32(0)
            for p in _PIECES[1:]:
                has = (n & p) != 0

                @pl.when(has)
                def _(done=done, p=p):
                    piece(done, p)

                done = done + jnp.where(has, p, 0)

    start = lambda c: c.start()
    wait = lambda c: c.wait()

    @pl.when(s == 0)
    def _():
        for d in range(ROWS_AHEAD):
            @pl.when(d < n_chunks)
            def _(d=d):
                rows_in(d, d).start()
        for c in weight_copies(cexp_ref[0], 0):
            c.start()

    @pl.when(s < n_steps)
    def _():
        for j in chunks:
            slot = cnew_ref[j]

            @pl.when(slot >= 0)
            def _(j=j, slot=slot):
                for c in weight_copies(cexp_ref[j], slot):
                    c.wait()
                nxt = cnext_ref[j]

                @pl.when(nxt >= 0)
                def _():
                    for c in weight_copies(nxt, 1 - slot):
                        c.start()

                wgb[slot] = wg_buf[slot].astype(BF16)
                wub[slot] = wu_buf[slot].astype(BF16)
                wdb[slot] = wd_buf[slot].astype(BF16)

        for j in chunks:
            @pl.when(j + ROWS_AHEAD < n_chunks)
            def _(j=j):
                rows_in(j + ROWS_AHEAD, (j + ROWS_AHEAD) % ROW_SLOTS).start()

        for j in chunks:
            rows_in(j, j % ROW_SLOTS).wait()

        for j in chunks:
            @pl.when(j >= OUT_SLOTS)
            def _(j=j):
                rows_out(j - OUT_SLOTS, j % OUT_SLOTS, wait)

        for j in chunks:
            xs_slot, w_slot, y_slot = j % ROW_SLOTS, cw_ref[j], j % OUT_SLOTS
            x = jnp.concatenate(
                [xbuf[xs_slot, pl.ds(r, bm, stride=ROW_TILES), :] for r in range(ROW_TILES)],
                axis=1)
            xb = x.astype(BF16)
            g = _dot(xb, wgb[w_slot])
            u = _dot(xb, wub[w_slot])
            h = (g * jax.nn.sigmoid(g) * u).astype(BF16)
            y = _dot(h, wdb[w_slot])
            for r in range(ROW_TILES):
                ybuf[y_slot, pl.ds(r, bm, stride=ROW_TILES), :] = y[:, r * LANES:(r + 1) * LANES]

        for j in chunks:
            rows_out(j, j % OUT_SLOTS, start)

        @pl.when(s == n_steps - 1)
        def _():
            for d in range(OUT_SLOTS):
                last = n_chunks - 1 - d

                @pl.when(last >= 0)
                def _(last=last):
                    rows_out(last, last % OUT_SLOTS, wait)


def _experts(cexp, crow, cn, cnew, cnext, cw, nsteps, xs, wg, wu, wd, n_rows):
    bm = EXPERT_BLOCK
    any_spec = pl.BlockSpec(memory_space=pl.ANY)
    grid_spec = pltpu.PrefetchScalarGridSpec(
        num_scalar_prefetch=7,
        grid=(cexp.shape[0] // CHUNKS_PER_STEP,),
        in_specs=[any_spec] * 4,
        out_specs=any_spec,
        scratch_shapes=[pltpu.VMEM((ROW_SLOTS, bm * ROW_TILES, LANES), F32),
                        pltpu.VMEM((OUT_SLOTS, bm * ROW_TILES, LANES), F32),
                        pltpu.VMEM((2, D_MODEL, D_EXPERT), F32),
                        pltpu.VMEM((2, D_MODEL, D_EXPERT), F32),
                        pltpu.VMEM((2, D_EXPERT, D_MODEL), F32),
                        pltpu.VMEM((2, D_MODEL, D_EXPERT), BF16),
                        pltpu.VMEM((2, D_MODEL, D_EXPERT), BF16),
                        pltpu.VMEM((2, D_EXPERT, D_MODEL), BF16),
                        pltpu.SemaphoreType.DMA((ROW_SLOTS,)),
                        pltpu.SemaphoreType.DMA((OUT_SLOTS,)),
                        pltpu.SemaphoreType.DMA((2,))],
    )
    return pl.pallas_call(
        _expert_kernel,
        grid_spec=grid_spec,
        out_shape=jax.ShapeDtypeStruct((n_rows * ROW_TILES, LANES), F32),
        compiler_params=pltpu.CompilerParams(
            dimension_semantics=("arbitrary",), vmem_limit_bytes=VMEM_LIMIT),
        name="experts",
    )(cexp, crow, cn, cnew, cnext, cw, nsteps, xs, wg, wu, wd)


GATHER_SLOTS = 4


def _combine_kernel(dest_ref, dest_next_ref, ys_ref, gate_ref, mid_ref, wsg_ref, wsu_ref,
                    wsd_ref, g2_ref, b2_ref, out_a_ref, out_b_ref, *scratch, alpha, steps_a):
    tt = MOVE_TILE
    ns = GATHER_SLOTS
    per_tile = tt * TOP_K
    g = pl.program_id(0)
    ng = pl.num_programs(0)
    bufs, (gate_scr, routed_scr, sem) = scratch[:ns], scratch[ns:]

    def row_copy(slots_ref, tile, b, t, k):
        src = _row_slice(ys_ref, slots_ref[tile * per_tile + k * tt + t])
        if isinstance(t, int):
            dst = bufs[b].at[pl.ds((k * tt + t) * ROW_TILES, ROW_TILES), :]
        else:
            dst = _row_slice(bufs[b], k * tt + t)
        return pltpu.make_async_copy(src, dst, sem.at[b])

    def wait_rows(b):
        pltpu.make_async_copy(ys_ref.at[pl.ds(0, per_tile * ROW_TILES), :], bufs[b],
                              sem.at[b]).wait()

    @pl.when(g == 0)
    def _():
        for r in range(ns - 1):
            def issue(t, carry, r=r):
                for k in range(TOP_K):
                    row_copy(dest_ref, r, r, t, k).start(priority=k % 2)
                return carry
            lax.fori_loop(0, tt, issue, 0)

    per_block = tt // (ROW_TILES * TOP_K)
    for r in range(ns):
        rows = slice(r * tt, (r + 1) * tt)
        ahead_ref, ahead_tile = (dest_ref, ns - 1) if r == 0 else (dest_next_ref, r - 1)
        ahead_buf = (r + ns - 1) % ns

        mid = mid_ref[rows, :]
        xb = mid.astype(BF16)
        gs = _dot(xb, wsg_ref[...])
        h = (gs * jax.nn.sigmoid(gs) * _dot(xb, wsu_ref[...])).astype(BF16)
        acc = alpha * mid + _dot(h, wsd_ref[...])

        wait_rows(r)
        gates = gate_ref[rows, :]
        for k in range(TOP_K):
            gate_scr[k] = jnp.broadcast_to(gates[:, k:k + 1], (tt, LANES))
        for j in range(ROW_TILES):
            part = jnp.zeros((tt, LANES), F32)
            for k in range(TOP_K):
                part = part + gate_scr[k] * bufs[r][pl.ds(k * tt * ROW_TILES + j, tt,
                                                           stride=ROW_TILES), :]
                t0 = (j * TOP_K + k) * per_block
                for t in range(t0, t0 + per_block):
                    for kk in range(TOP_K):
                        row_copy(ahead_ref, ahead_tile, ahead_buf, t, kk).start(priority=kk % 2)
            routed_scr[:, j * LANES:(j + 1) * LANES] = part
        out = _layer_norm(acc + routed_scr[...], g2_ref[...], b2_ref[...])

        @pl.when(g < steps_a)
        def _(out=out, rows=rows):
            out_a_ref[rows, :] = out

        @pl.when(g >= steps_a)
        def _(out=out, rows=rows):
            out_b_ref[rows, :] = out

    @pl.when(g == ng - 1)
    def _():
        for b in range(ns - 1):
            wait_rows(b)


def _combine(dest_flat, ys, gate, mid, wsg, wsu, wsd, g2, b2, *, alpha, rows_a):
    tt = MOVE_TILE
    rows = tt * GATHER_SLOTS
    tokens = mid.shape[0]
    n = tokens // rows
    steps_a = rows_a // rows
    consts = [wsg, wsu, wsd, g2, b2]
    slots_spec = lambda f: pl.BlockSpec((rows * TOP_K,), f, memory_space=pltpu.SMEM)
    return pl.pallas_call(
        functools.partial(_combine_kernel, alpha=alpha, steps_a=steps_a),
        grid=(n,),
        in_specs=[slots_spec(lambda i: (i,)),
                  slots_spec(lambda i: (jnp.minimum(i + 1, n - 1),)),
                  pl.BlockSpec(memory_space=pl.ANY),
                  pl.BlockSpec((rows, TOP_K), lambda i: (i, 0)),
                  pl.BlockSpec((rows, D_MODEL), lambda i: (i, 0))]
                 + [_const_spec(c.shape) for c in consts],
        out_specs=(pl.BlockSpec((rows, D_MODEL), lambda i: (jnp.minimum(i, steps_a - 1), 0)),
                   pl.BlockSpec((rows, D_MODEL), lambda i: (jnp.maximum(i - steps_a, 0), 0))),
        out_shape=(jax.ShapeDtypeStruct((rows_a, D_MODEL), F32),
                   jax.ShapeDtypeStruct((tokens - rows_a, D_MODEL), F32)),
        scratch_shapes=[pltpu.VMEM((TOP_K * tt * ROW_TILES, LANES), F32)] * GATHER_SLOTS
                       + [pltpu.VMEM((TOP_K, tt, LANES), F32),
                          pltpu.VMEM((tt, D_MODEL), F32),
                          pltpu.SemaphoreType.DMA((GATHER_SLOTS,))],
        compiler_params=pltpu.CompilerParams(
            dimension_semantics=("arbitrary",), vmem_limit_bytes=VMEM_LIMIT),
        name="combine",
    )(dest_flat, dest_flat, ys, gate, mid, *consts)


def _tri(n, *, lower):
    r = lax.broadcasted_iota(jnp.int32, (n, n), 0)
    c = lax.broadcasted_iota(jnp.int32, (n, n), 1)
    return jnp.where((c <= r) if lower else (r <= c), 1.0, 0.0).astype(BF16)


def _moe(mid, rows3, w_router, b_router, w_e_gate, w_e_up, w_e_down, wsg, wsu, wsd, g2, b2, alpha,
         rows_a):
    tokens = mid.shape[0]
    bm = EXPERT_BLOCK
    wr_t = w_router.T
    wr_hi = wr_t.astype(BF16)
    wr_lo = (wr_t - wr_hi.astype(F32)).astype(BF16)
    r = lax.broadcasted_iota(jnp.int32, (ROUTER_TILE, ROUTER_TILE), 0)
    c = lax.broadcasted_iota(jnp.int32, (ROUTER_TILE, ROUTER_TILE), 1)
    before = jnp.where(r < c, 1.0, 0.0).astype(BF16)
    eidx, gate, rank, cnt = _router(mid, wr_hi, wr_lo, b_router.reshape(N_EXPERTS, 1), before)

    i32 = lambda a: a.astype(jnp.int32)
    experts = jnp.arange(N_EXPERTS, dtype=jnp.int32)
    counts = i32(cnt[:, 0])
    starts = jnp.cumsum(counts) - counts
    n_ch = (counts + bm - 1) // bm
    ch_ends = jnp.cumsum(n_ch)
    ch_starts = ch_ends - n_ch
    n_used = ch_ends[-1]
    max_chunks = tokens * TOP_K // bm + N_EXPERTS
    ci = jnp.arange(max_chunks, dtype=jnp.int32)
    cc = jnp.minimum(ci, n_used - 1)
    cexp = jnp.minimum(jnp.sum(i32(ch_ends[None, :] <= cc[:, None]), axis=1), N_EXPERTS - 1)
    hot = cexp[:, None] == experts[None, :]
    lookup = lambda table: jnp.sum(jnp.where(hot, table[None, :], 0), axis=1)
    k_in_expert = cc - lookup(ch_starts)
    crow = lookup(starts) + k_in_expert * bm
    used = ci < n_used
    cn = jnp.where(used, jnp.clip(lookup(counts) - k_in_expert * bm, 0, bm), 0)
    is_first = used & (k_in_expert == 0)
    cw = (jnp.cumsum(i32(is_first)) - 1) % 2
    cnew = jnp.where(is_first, cw, -1)
    later = (experts[None, :] > experts[:, None]) & (n_ch[None, :] > 0)
    next_expert = jnp.min(jnp.where(later, experts[None, :], N_EXPERTS), axis=1)
    cnext = lookup(jnp.where(next_expert < N_EXPERTS, next_expert, -1))
    cnext = jnp.where(is_first, cnext, -1)
    dest = _dest(eidx, rank, starts.astype(F32).reshape(N_EXPERTS, 1)).reshape(tokens * TOP_K)

    xs = _dispatch(dest, rows3)
    n_steps = (n_used + CHUNKS_PER_STEP - 1) // CHUNKS_PER_STEP
    ys = _experts(cexp, i32(crow), i32(cn), i32(cnew), i32(cnext), i32(cw),
                  i32(n_steps).reshape(1), xs, w_e_gate, w_e_up, w_e_down, tokens * TOP_K)
    return _combine(dest, ys, gate.T, mid, wsg, wsu, wsd, g2, b2, alpha=alpha, rows_a=rows_a)


def kernel(x_prompt, x_sample, cache_k, cache_v, cache_logf, state_conv, w_in, b_in, conv_w,
           conv_b, conv_ln_g, conv_ln_b, w_a, w_b, b_b, w_out, ln1_g, ln1_b, w_router, b_router,
           w_e_gate, w_e_up, w_e_down, w_s_gate, w_s_up, w_s_down, ln2_g, ln2_b):
    depth = w_in.shape[0]
    alpha = float((2 * depth) ** 0.25)
    batch, seq, _ = x_prompt.shape
    dbatch, dseq, _ = x_sample.shape
    past = cache_k.shape[2]
    rows_p, rows_s = batch * seq, dbatch * dseq
    total = rows_p + rows_s
    assert seq % IN_TILE == 0 and rows_s % IN_TILE == 0 and seq % MERGE_TILE == 0
    assert total % ROUTER_TILE == 0 and dseq == HIST and rows_p % dseq == 0
    assert total % DEST_TILE == 0 and (total * TOP_K) % EXPERT_BLOCK == 0
    assert rows_p % (MOVE_TILE * GATHER_SLOTS) == 0 and rows_s % (MOVE_TILE * GATHER_SLOTS) == 0

    hp = x_prompt.reshape(rows_p, D_MODEL)
    hs = x_sample.reshape(rows_s, D_MODEL)
    tri_in = _tri(IN_TILE, lower=True)
    upper_past = _tri(past, lower=False)
    row2 = lambda a: a.reshape(1, -1)
    outs = {n: [] for n in ("kp", "vp", "fp", "cp", "ks", "vs", "fs", "cs")}

    for l in range(depth):
        w = w_in[l]
        b = b_in[l]
        main_cols = lambda a: jnp.concatenate([a[..., :OFF_F], a[..., OFF_GLU:]], axis=-1)
        w_main = main_cols(w).astype(BF16)
        b_main = row2(main_cols(b))
        w_f = jnp.pad(w[:, OFF_F:OFF_GLU], ((0, 0), (0, LANES - FOX_HEADS))).astype(BF16)
        b_f = row2(jnp.pad(b[OFF_F:OFF_GLU], (0, LANES - FOX_HEADS)))
        cw = jnp.pad(conv_w[l], ((0, 1), (0, 0)))
        conv_p = (cw, row2(conv_b[l]), row2(conv_ln_g[l]), row2(conv_ln_b[l]),
                  w_b[l].astype(BF16), row2(b_b[l]))
        wa, wout = w_a[l].astype(BF16), w_out[l].astype(BF16)
        g1, b1 = row2(ln1_g[l]), row2(ln1_b[l])

        q, k, v, kb, vb, logf, c, u, sa, sb = _inproj(
            hp, w_main, b_main, w_f, b_f, tri_in, tiles_per_seq=seq // IN_TILE)
        c_row = c.reshape(batch, seq, FOX_HEADS).transpose(0, 2, 1)
        attn = _attn_prompt(q, kb, vb, c_row, batch=batch, seq=seq)
        mid, rows3 = _merge(u, u, attn, sa, sb, hp, conv_p, wa, wout, g1, b1,
                            n_seq=batch, seq=seq, ts=MERGE_TILE, parts=MERGE_PARTS,
                            hist_from_u=True, alpha=alpha, total_rows=total, row_offset=0)
        outs["kp"].append(k.reshape(batch, seq, FOX_HEADS, HEAD_DIM))
        outs["vp"].append(v.reshape(batch, seq, FOX_HEADS, HEAD_DIM))
        outs["fp"].append(logf.reshape(batch, seq, FOX_HEADS))
        outs["cp"].append(u.reshape(batch, seq, CONV_CH)[:, seq - (CONV_WIDTH - 1):])

        q, k, v, kb, vb, logf, _, u, sa, sb = _inproj(
            hs, w_main, b_main, w_f, b_f, tri_in, tiles_per_seq=1)
        logf_t = logf.reshape(dbatch, dseq, FOX_HEADS).transpose(0, 2, 1)
        attn = _attn_sample(
            q, kb, vb, logf, logf_t, cache_k[l].reshape(dbatch, past, FOX_WIDTH),
            cache_v[l].reshape(dbatch, past, FOX_WIDTH), cache_logf[l].transpose(0, 2, 1),
            upper_past, batch=dbatch, t=dseq, past=past)
        hist = jnp.pad(state_conv[l], ((0, 0), (HIST - (CONV_WIDTH - 1), 0), (0, 0)))
        mid, rows3 = _merge(u, hist.reshape(dbatch * HIST, CONV_CH), attn, sa, sb, hs, conv_p,
                            wa, wout, g1, b1, n_seq=dbatch, seq=dseq, ts=dseq, parts=1,
                            hist_from_u=False,
                            alpha=alpha, total_rows=total, row_offset=rows_p, prev=(mid, rows3))
        outs["ks"].append(k.reshape(dbatch, dseq, FOX_HEADS, HEAD_DIM))
        outs["vs"].append(v.reshape(dbatch, dseq, FOX_HEADS, HEAD_DIM))
        outs["fs"].append(logf.reshape(dbatch, dseq, FOX_HEADS))
        u3 = u.reshape(dbatch, dseq, CONV_CH)
        u_ext = jnp.concatenate([state_conv[l], u3], axis=1)
        outs["cs"].append(u_ext[:, -(CONV_WIDTH - 1):])

        hp, hs = _moe(mid, rows3, w_router[l], b_router[l], w_e_gate[l], w_e_up[l], w_e_down[l],
                      w_s_gate[l].astype(BF16), w_s_up[l].astype(BF16), w_s_down[l].astype(BF16),
                      row2(ln2_g[l]), row2(ln2_b[l]), alpha, rows_p)

    st = lambda n: jnp.stack(outs[n])
    return (hp.reshape(batch, seq, D_MODEL), hs.reshape(dbatch, dseq, D_MODEL),
            st("kp"), st("vp"), st("fp"), st("cp"), st("ks"), st("vs"), st("fs"), st("cs"))
```

```python
import functools

import jax
import jax.numpy as jnp
from jax import lax
from jax.experimental import pallas as pl
from jax.experimental.pallas import tpu as pltpu

D_MODEL = 1024
FOX_HEADS = 8
HEAD_DIM = 64
FOX_WIDTH = FOX_HEADS * HEAD_DIM
ATTN_SCALE = HEAD_DIM ** -0.5
LOG2E = 1.4426950408889634
CONV_CH = D_MODEL // 2
CONV_WIDTH = 31
N_EXPERTS = 256
TOP_K = 8
N_GROUPS = 8
GROUP_SIZE = N_EXPERTS // N_GROUPS
TOPK_GROUPS = 4
D_EXPERT = D_MODEL // 4
ROUTED_SCALE = 2.5
LN_EPS = 1e-5

OFF_K = FOX_WIDTH
OFF_V = 2 * FOX_WIDTH
OFF_F = 3 * FOX_WIDTH
OFF_GLU = OFF_F + FOX_HEADS
OFF_GA = OFF_GLU + 2 * CONV_CH
OFF_GB = OFF_GA + D_MODEL

LANES = 128
SUBLANES = 8
ROW_TILES = D_MODEL // LANES
VMEM_LIMIT = 56 * 1024 * 1024

IN_TILE = 512
ATTN_TILE = 256
MERGE_TILE = 512
MERGE_PARTS = 2
CONV_SUB = 32
HIST = 32
ROUTER_TILE = 512
MOVE_TILE = 128
DEST_TILE = 1024
DISPATCH_SLABS = 4
EXPERT_BLOCK = 256

F32 = jnp.float32
BF16 = jnp.bfloat16
NEG_INF = float("-inf")
NT_DIMS = (((1,), (1,)), ((), ()))


def _const_spec(shape):
    nd = len(shape)
    return pl.BlockSpec(shape, lambda *_: (0,) * nd, pipeline_mode=pl.Buffered(1))


def _split3(x):
    hi = x.astype(BF16)
    r1 = x - hi.astype(F32)
    mid = r1.astype(BF16)
    lo = (r1 - mid.astype(F32)).astype(BF16)
    return hi, mid, lo


def _dot(a, b):
    return jnp.dot(a, b, preferred_element_type=F32)


def _dot_nt(a, b):
    return lax.dot_general(a, b, NT_DIMS, preferred_element_type=F32)


def _exact_dot(ones_mat, x, *, ones_on_left):
    acc = None
    for part in _split3(x):
        term = _dot(ones_mat, part) if ones_on_left else _dot(part, ones_mat)
        acc = term if acc is None else acc + term
    return acc


def _layer_norm(x, g, b):
    mu = jnp.mean(x, axis=-1, keepdims=True)
    xc = x - mu
    var = jnp.mean(xc * xc, axis=-1, keepdims=True)
    return xc * lax.rsqrt(var + LN_EPS) * g + b


def _log_sigmoid(z):
    return jnp.minimum(z, 0.0) - jnp.log1p(jnp.exp(-jnp.abs(z)))


def _inproj_kernel(x_ref, w_ref, b_ref, wf_ref, bf_ref, tri_ref,
                   q_ref, k_ref, v_ref, kb_ref, vb_ref, logf_ref, c_ref, u_ref,
                   sa_ref, sb_ref, carry_ref, *, tiles_per_seq):
    i = pl.program_id(0)
    xb = x_ref[...].astype(BF16)

    def proj(c0, c1):
        return _dot(xb, w_ref[:, c0:c1]) + b_ref[:, c0:c1]

    q = proj(0, FOX_WIDTH)
    q_ref[...] = (q * (ATTN_SCALE * LOG2E)).astype(BF16)
    k = proj(FOX_WIDTH, 2 * FOX_WIDTH)
    kb_ref[...] = k.astype(BF16)
    v = proj(2 * FOX_WIDTH, 3 * FOX_WIDTH)
    vb_ref[...] = v.astype(BF16)
    for h in range(FOX_HEADS):
        head_rows = pl.ds(h, IN_TILE, stride=FOX_HEADS)
        k_ref[head_rows, :] = k[:, h * HEAD_DIM:(h + 1) * HEAD_DIM]
        v_ref[head_rows, :] = v[:, h * HEAD_DIM:(h + 1) * HEAD_DIM]

    logf = _log_sigmoid(_dot(xb, wf_ref[...]) + bf_ref[...])
    logf_ref[...] = logf[:, :FOX_HEADS]

    @pl.when(i % tiles_per_seq == 0)
    def _():
        carry_ref[...] = jnp.zeros_like(carry_ref)

    c = _exact_dot(tri_ref[...], logf, ones_on_left=True) + carry_ref[...]
    c_ref[...] = c[:, :FOX_HEADS]
    carry_ref[...] = c[IN_TILE - 1:IN_TILE, :]

    g0 = 3 * FOX_WIDTH
    glu_a = proj(g0, g0 + CONV_CH)
    glu_b = proj(g0 + CONV_CH, g0 + 2 * CONV_CH)
    u_ref[...] = glu_a * jax.nn.sigmoid(glu_b)
    g1 = g0 + 2 * CONV_CH
    sa_ref[...] = jax.nn.sigmoid(proj(g1, g1 + D_MODEL)).astype(BF16)
    sb_ref[...] = jax.nn.sigmoid(proj(g1 + D_MODEL, g1 + 2 * D_MODEL)).astype(BF16)


def _inproj(x, w_main, b_main, w_f, b_f, tri, *, tiles_per_seq):
    rows = x.shape[0]
    n_main = w_main.shape[1]
    row_spec = lambda w: pl.BlockSpec((IN_TILE, w), lambda i: (i, 0))
    out_shapes = (
        jax.ShapeDtypeStruct((rows, FOX_WIDTH), BF16),
        jax.ShapeDtypeStruct((rows * FOX_HEADS, HEAD_DIM), F32),
        jax.ShapeDtypeStruct((rows * FOX_HEADS, HEAD_DIM), F32),
        jax.ShapeDtypeStruct((rows, FOX_WIDTH), BF16),
        jax.ShapeDtypeStruct((rows, FOX_WIDTH), BF16),
        jax.ShapeDtypeStruct((rows, FOX_HEADS), F32),
        jax.ShapeDtypeStruct((rows, FOX_HEADS), F32),
        jax.ShapeDtypeStruct((rows, CONV_CH), F32),
        jax.ShapeDtypeStruct((rows, D_MODEL), BF16),
        jax.ShapeDtypeStruct((rows, D_MODEL), BF16),
    )
    head_spec = pl.BlockSpec((IN_TILE * FOX_HEADS, HEAD_DIM), lambda i: (i, 0))
    out_specs = (row_spec(FOX_WIDTH), head_spec, head_spec) + (row_spec(FOX_WIDTH),) * 2 + (
        row_spec(FOX_HEADS),) * 2 + (row_spec(CONV_CH), row_spec(D_MODEL), row_spec(D_MODEL))
    return pl.pallas_call(
        functools.partial(_inproj_kernel, tiles_per_seq=tiles_per_seq),
        grid=(rows // IN_TILE,),
        in_specs=[row_spec(D_MODEL), _const_spec((D_MODEL, n_main)), _const_spec((1, n_main)),
                  _const_spec((D_MODEL, LANES)), _const_spec((1, LANES)),
                  _const_spec((IN_TILE, IN_TILE))],
        out_specs=out_specs,
        out_shape=out_shapes,
        scratch_shapes=[pltpu.VMEM((1, LANES), F32)],
        compiler_params=pltpu.CompilerParams(
            dimension_semantics=("arbitrary",), vmem_limit_bytes=VMEM_LIMIT),
        name="inproj",
    )(x, w_main, b_main, w_f, b_f, tri)


def _attn_prompt_kernel(q_ref, k_ref, v_ref, crow_ref, o_ref, qm_scr, m_scr, acc_scr):
    t = ATTN_TILE
    i = pl.program_id(1)
    lane = lax.broadcasted_iota(jnp.int32, (t, LANES), 1)
    row = lax.broadcasted_iota(jnp.int32, (t, t), 0)
    col = lax.broadcasted_iota(jnp.int32, (t, t), 1)
    causal = col <= row
    wide = lambda x: jnp.concatenate([x] * (t // LANES), axis=1)
    pair_lanes = lambda pair: slice(pair * LANES, (pair + 1) * LANES)

    for h in range(FOX_HEADS):
        qp = q_ref[:, pair_lanes(h // 2)]
        in_head = (lane < HEAD_DIM) if h % 2 == 0 else (lane >= HEAD_DIM)
        qm_scr[h] = jnp.where(in_head, qp, jnp.zeros_like(qp))
        m_scr[h] = jnp.full((t, LANES), NEG_INF, F32)
        acc_scr[h] = jnp.zeros((t, 2 * LANES), F32)
    ones = jnp.ones((t, LANES), BF16)

    def step(j, masked):
        r0 = pl.multiple_of(j * t, t)
        ck = crow_ref[0, :, pl.ds(r0, t)] * LOG2E
        for pair in range(FOX_HEADS // 2):
            kj = k_ref[pl.ds(r0, t), pair_lanes(pair)]
            vj = jnp.concatenate([v_ref[pl.ds(r0, t), pair_lanes(pair)], ones], axis=1)
            for h in (2 * pair, 2 * pair + 1):
                s = _dot_nt(qm_scr[h], kj) - ck[h:h + 1, :]
                if masked:
                    s = jnp.where(causal, s, NEG_INF)
                m_prev = m_scr[h]
                m_new = jnp.maximum(m_prev, jnp.max(s, axis=1, keepdims=True))
                alpha = jnp.exp2(m_prev - m_new)
                p = jnp.exp2(s - wide(m_new))
                acc_scr[h] = wide(alpha) * acc_scr[h] + _dot(p.astype(BF16), vj)
                m_scr[h] = m_new

    def body(jj, carry):
        step(2 * jj, False)
        step(2 * jj + 1, False)
        return carry

    lax.fori_loop(0, i // 2, body, 0)

    @pl.when(i % 2 == 1)
    def _():
        step(i - 1, False)

    step(i, True)
    for pair in range(FOX_HEADS // 2):
        o0, o1 = (acc_scr[h, :, :LANES] / acc_scr[h, :, LANES:] for h in (2 * pair, 2 * pair + 1))
        o_ref[:, pair_lanes(pair)] = jnp.where(lane < HEAD_DIM, o0, o1).astype(BF16)


def _attn_prompt(q, kb, vb, c_row, *, batch, seq):
    t = ATTN_TILE
    nq = seq // t
    return pl.pallas_call(
        _attn_prompt_kernel,
        grid=(batch, nq),
        in_specs=[
            pl.BlockSpec((t, FOX_WIDTH), lambda b, i: (b * nq + i, 0)),
            pl.BlockSpec((seq, FOX_WIDTH), lambda b, i: (b, 0)),
            pl.BlockSpec((seq, FOX_WIDTH), lambda b, i: (b, 0)),
            pl.BlockSpec((1, FOX_HEADS, seq), lambda b, i: (b, 0, 0)),
        ],
        out_specs=pl.BlockSpec((t, FOX_WIDTH), lambda b, i: (b * nq + i, 0)),
        out_shape=jax.ShapeDtypeStruct((batch * seq, FOX_WIDTH), BF16),
        scratch_shapes=[pltpu.VMEM((FOX_HEADS, t, LANES), BF16),
                        pltpu.VMEM((FOX_HEADS, t, LANES), F32),
                        pltpu.VMEM((FOX_HEADS, t, 2 * LANES), F32)],
        compiler_params=pltpu.CompilerParams(
            dimension_semantics=("arbitrary", "arbitrary"), vmem_limit_bytes=VMEM_LIMIT),
        name="attn_prompt",
    )(q, kb, vb, c_row)


def _attn_sample_kernel(q_ref, kn_ref, vn_ref, lf_ref, lft_ref, ck_ref, cv_ref, clft_ref,
                        upper_ref, o_ref, *, t, past):
    rows = FOX_HEADS * t
    lane_head = lax.broadcasted_iota(jnp.int32, (t, FOX_WIDTH), 1) // HEAD_DIM
    q = q_ref[...]
    q_stack = jnp.concatenate(
        [jnp.where(lane_head == h, q, jnp.zeros_like(q)) for h in range(FOX_HEADS)], axis=0)

    clf = clft_ref[0]
    prefix = _exact_dot(upper_ref[...], clf, ones_on_left=False)
    to_end = prefix[:, past - 1:past] - prefix
    ri = lax.broadcasted_iota(jnp.int32, (t, t), 0)
    ci = lax.broadcasted_iota(jnp.int32, (t, t), 1)
    lower = jnp.where(ci <= ri, 1.0, 0.0).astype(BF16)
    upper = jnp.where(ri <= ci, 1.0, 0.0).astype(BF16)
    cn_col = _exact_dot(lower, lf_ref[...], ones_on_left=True)
    cn_row = _exact_dot(upper, lft_ref[0], ones_on_left=False)

    stack = lambda f: jnp.concatenate([f(h) for h in range(FOX_HEADS)], axis=0)
    cn_stack = stack(lambda h: cn_col[:, h:h + 1])
    bias_c = stack(lambda h: jnp.broadcast_to(to_end[h:h + 1, :], (t, past)))
    bias_n = stack(lambda h: jnp.broadcast_to(cn_row[h:h + 1, :], (t, t)))

    kc = ck_ref[0].astype(BF16)
    vc = cv_ref[0].astype(BF16)
    s_c = _dot_nt(q_stack, kc) + (bias_c + cn_stack) * LOG2E
    s_n = _dot_nt(q_stack, kn_ref[...]) + (cn_stack - bias_n) * LOG2E
    tq = lax.broadcasted_iota(jnp.int32, (rows, t), 0) % t
    tk = lax.broadcasted_iota(jnp.int32, (rows, t), 1)
    s_n = jnp.where(tk <= tq, s_n, NEG_INF)
    m = jnp.maximum(jnp.max(s_c, axis=1, keepdims=True), jnp.max(s_n, axis=1, keepdims=True))
    p_c = jnp.exp2(s_c - m)
    p_n = jnp.exp2(s_n - m)
    denom = jnp.sum(p_c, axis=1, keepdims=True) + jnp.sum(p_n, axis=1, keepdims=True)
    o_stack = (_dot(p_c.astype(BF16), vc) + _dot(p_n.astype(BF16), vn_ref[...])) / denom
    out = jnp.zeros((t, FOX_WIDTH), F32)
    for h in range(FOX_HEADS):
        out = out + jnp.where(lane_head == h, o_stack[h * t:(h + 1) * t, :], 0.0)
    o_ref[...] = out.astype(BF16)


def _attn_sample(q, kb, vb, logf, logf_t, cache_k, cache_v, cache_logf_t, upper, *, batch, t, past):
    row_spec = lambda w: pl.BlockSpec((t, w), lambda b: (b, 0))
    return pl.pallas_call(
        functools.partial(_attn_sample_kernel, t=t, past=past),
        grid=(batch,),
        in_specs=[
            row_spec(FOX_WIDTH), row_spec(FOX_WIDTH), row_spec(FOX_WIDTH), row_spec(FOX_HEADS),
            pl.BlockSpec((1, FOX_HEADS, t), lambda b: (b, 0, 0)),
            pl.BlockSpec((1, past, FOX_WIDTH), lambda b: (b, 0, 0)),
            pl.BlockSpec((1, past, FOX_WIDTH), lambda b: (b, 0, 0)),
            pl.BlockSpec((1, FOX_HEADS, past), lambda b: (b, 0, 0)),
            _const_spec((past, past)),
        ],
        out_specs=row_spec(FOX_WIDTH),
        out_shape=jax.ShapeDtypeStruct((batch * t, FOX_WIDTH), BF16),
        compiler_params=pltpu.CompilerParams(
            dimension_semantics=("arbitrary",), vmem_limit_bytes=VMEM_LIMIT),
        name="attn_sample",
    )(q, kb, vb, logf, logf_t, cache_k, cache_v, cache_logf_t, upper)


def _merge_kernel(u_ref, hist_ref, attn_ref, sa_ref, sb_ref, x_ref,
                  cw_ref, cb_ref, cg_ref, cbeta_ref, wb_ref, bb_ref, wa_ref, wout_ref,
                  g1_ref, b1_ref, *rest, ts, parts, tiles_per_seq, n_tiles, zero_first_hist, alpha,
                  aliased):
    if aliased:
        rest = rest[2:]
    mid_ref, mid3_ref = rest[:2]
    scratch = rest[2:]
    i = pl.program_id(0)
    th = ts // parts

    @pl.when(i >= n_tiles)
    def _():
        mid_ref[...] = jnp.zeros_like(mid_ref)
        mid3_ref[...] = jnp.zeros_like(mid3_ref)

    @pl.when(i < n_tiles)
    def _():
        for p in range(parts):
            r0 = p * th
            if p == 0:
                hist = hist_ref[...]
                if zero_first_hist:
                    hist = jnp.where(i % tiles_per_seq == 0, 0.0, hist)
            else:
                hist = u_ref[r0 - HIST:r0, :]
            _merge_rows(hist, r0, th, u_ref, attn_ref, sa_ref, sb_ref, x_ref, cw_ref, cb_ref,
                        cg_ref, cbeta_ref, wb_ref, bb_ref, wa_ref, wout_ref, g1_ref, b1_ref,
                        mid_ref, mid3_ref, *scratch[3 * p:3 * p + 3], alpha=alpha)


def _merge_rows(hist, r0, th, u_ref, attn_ref, sa_ref, sb_ref, x_ref, cw_ref, cb_ref, cg_ref,
                cbeta_ref, wb_ref, bb_ref, wa_ref, wout_ref, g1_ref, b1_ref, mid_ref, mid3_ref,
                ue_scr, ph_scr, h_scr, *, alpha):
    rows = slice(r0, r0 + th)
    ue_scr[0:HIST, :] = hist
    ue_scr[HIST:HIST + th, :] = u_ref[rows, :]
    lead = HIST - (CONV_WIDTH - 1)
    for r in range(SUBLANES):
        span = th + (CONV_WIDTH - 1 - r) // SUBLANES * SUBLANES
        ph_scr[r, 0:span, :] = ue_scr[lead + r:lead + r + span, :]

    for rt in range(th // CONV_SUB):
        base = rt * CONV_SUB
        acc = jnp.zeros((CONV_SUB, CONV_CH), F32)
        for j in range(CONV_WIDTH):
            r, a = j % SUBLANES, j // SUBLANES
            start = base + a * SUBLANES
            acc = acc + cw_ref[j:j + 1, :] * ph_scr[r, start:start + CONV_SUB, :]
        h_scr[base:base + CONV_SUB, :] = acc

    h = _layer_norm(h_scr[...] + cb_ref[...], cg_ref[...], cbeta_ref[...])
    h = h * jax.nn.sigmoid(h)
    conv_out = _dot(h.astype(BF16), wb_ref[...]) + bb_ref[...]
    attn_out = _dot(attn_ref[rows, :], wa_ref[...])
    m = sa_ref[rows, :].astype(F32) * attn_out + sb_ref[rows, :].astype(F32) * conv_out
    z = alpha * x_ref[rows, :] + _dot(m.astype(BF16), wout_ref[...])
    mid = _layer_norm(z, g1_ref[...], b1_ref[...])
    mid_ref[rows, :] = mid
    for j in range(ROW_TILES):
        mid3_ref[pl.ds(r0 * ROW_TILES + j, th, stride=ROW_TILES), :] = (
            mid[:, j * LANES:(j + 1) * LANES])


def _merge(u, hist, attn, sa, sb, x, conv_p, wa, wout, g1, b1, *, n_seq, seq, ts, parts,
           hist_from_u, alpha, total_rows, row_offset, prev=None):
    nt = seq // ts
    n_tiles = n_seq * nt
    off = row_offset // ts
    grid_tiles = n_tiles if prev is not None else total_rows // ts
    src = lambda i: jnp.minimum(i, n_tiles - 1)
    row_spec = lambda w: pl.BlockSpec((ts, w), lambda i: (src(i), 0))
    if hist_from_u:
        per = ts // HIST
        hist_spec = pl.BlockSpec((HIST, CONV_CH), lambda i: (jnp.maximum(src(i) * per - 1, 0), 0))
    else:
        hist_spec = pl.BlockSpec((HIST, CONV_CH), lambda i: (src(i) // nt, 0))
    cw, cb, cg, cbeta, wb, bb = conv_p
    consts = [cw, cb, cg, cbeta, wb, bb, wa, wout, g1, b1]
    in_specs = [row_spec(CONV_CH), hist_spec, row_spec(FOX_WIDTH), row_spec(D_MODEL),
                row_spec(D_MODEL), row_spec(D_MODEL)] + [_const_spec(c.shape) for c in consts]
    args = [u, hist, attn, sa, sb, x] + consts
    aliases = {}
    if prev is not None:
        in_specs += [pl.BlockSpec(memory_space=pl.ANY)] * 2
        aliases = {len(args): 0, len(args) + 1: 1}
        args += list(prev)
    th = ts // parts
    span = th + (CONV_WIDTH - 1) // SUBLANES * SUBLANES
    return pl.pallas_call(
        functools.partial(_merge_kernel, ts=ts, parts=parts, tiles_per_seq=nt, n_tiles=n_tiles,
                          zero_first_hist=hist_from_u, alpha=alpha, aliased=prev is not None),
        grid=(grid_tiles,),
        in_specs=in_specs,
        out_specs=(pl.BlockSpec((ts, D_MODEL), lambda i: (off + i, 0)),
                   pl.BlockSpec((ts * ROW_TILES, LANES), lambda i: (off + i, 0))),
        out_shape=(jax.ShapeDtypeStruct((total_rows, D_MODEL), F32),
                   jax.ShapeDtypeStruct((total_rows * ROW_TILES, LANES), F32)),
        scratch_shapes=[pltpu.VMEM((HIST + th, CONV_CH), F32),
                        pltpu.VMEM((SUBLANES, span, CONV_CH), F32),
                        pltpu.VMEM((th, CONV_CH), F32)] * parts,
        input_output_aliases=aliases,
        compiler_params=pltpu.CompilerParams(
            dimension_semantics=("arbitrary",), vmem_limit_bytes=VMEM_LIMIT),
        name="merge",
    )(*args)


def _router_kernel(mid_ref, wr_hi_ref, wr_lo_ref, br_ref, before_ref,
                   eidx_ref, gate_ref, rank_ref, cnt_ref, carry_scr):
    tr = ROUTER_TILE
    i = pl.program_id(0)

    @pl.when(i == 0)
    def _():
        carry_scr[...] = jnp.zeros_like(carry_scr)

    x = mid_ref[...]
    x_hi = x.astype(BF16)
    x_lo = (x - x_hi.astype(F32)).astype(BF16)
    wr_hi = wr_hi_ref[...]
    logits = _dot_nt(wr_hi, x_hi) + _dot_nt(wr_hi, x_lo) + _dot_nt(wr_lo_ref[...], x_hi)
    scores = jax.nn.sigmoid(logits)
    sel = scores + br_ref[...]

    sel3 = sel.reshape(N_GROUPS, GROUP_SIZE, tr)
    in_group = lax.broadcasted_iota(jnp.int32, sel3.shape, 1)
    m1 = jnp.max(sel3, axis=1, keepdims=True)
    first = jnp.min(jnp.where(sel3 == m1, in_group, GROUP_SIZE), axis=1, keepdims=True)
    m2 = jnp.max(jnp.where(in_group == first, NEG_INF, sel3), axis=1, keepdims=True)
    gs = m1 + m2
    gi = lax.broadcasted_iota(jnp.int32, gs.shape, 0)
    beaten = jnp.zeros(gs.shape, F32)
    for g in range(N_GROUPS):
        other = gs[g:g + 1]
        wins = (other > gs) | ((other == gs) & (g < gi))
        beaten = beaten + jnp.where(wins, 1.0, 0.0)
    drop = jnp.where(beaten < TOPK_GROUPS, 0.0, NEG_INF)
    cur = (sel3 + drop).reshape(N_EXPERTS, tr)

    ei = lax.broadcasted_iota(jnp.int32, (N_EXPERTS, tr), 0)
    idxs, vals = [], []
    candidates = cur
    for _ in range(TOP_K):
        m = jnp.max(cur, axis=0, keepdims=True)
        idx = jnp.min(jnp.where(cur == m, ei, N_EXPERTS), axis=0, keepdims=True)
        hit = ei == idx
        vals.append(jnp.sum(jnp.where(hit, scores, 0.0), axis=0, keepdims=True))
        idxs.append(idx)
        cur = jnp.where(hit, NEG_INF, cur)
    picked = jnp.where(cur != candidates, 1.0, 0.0)

    total = vals[0]
    for v in vals[1:]:
        total = total + v
    for k in range(TOP_K):
        gate_ref[k:k + 1, :] = vals[k] / total * ROUTED_SCALE
        eidx_ref[k:k + 1, :] = idxs[k]

    ahead = _dot(picked.astype(BF16), before_ref[...]) + carry_scr[:, 0:1]
    for k in range(TOP_K):
        rank = jnp.sum(jnp.where(ei == idxs[k], ahead, 0.0), axis=0, keepdims=True)
        rank_ref[k:k + 1, :] = rank.astype(jnp.int32)
    carry_scr[...] = carry_scr[...] + jnp.sum(picked, axis=1, keepdims=True)
    cnt_ref[...] = carry_scr[...]


def _router(mid, wr_hi, wr_lo, br, before):
    tr = ROUTER_TILE
    tokens = mid.shape[0]
    tok_spec = pl.BlockSpec((TOP_K, tr), lambda i: (0, i))
    return pl.pallas_call(
        _router_kernel,
        grid=(tokens // tr,),
        in_specs=[pl.BlockSpec((tr, D_MODEL), lambda i: (i, 0)),
                  _const_spec((N_EXPERTS, D_MODEL)), _const_spec((N_EXPERTS, D_MODEL)),
                  _const_spec((N_EXPERTS, 1)), _const_spec((tr, tr))],
        out_specs=(tok_spec, tok_spec, tok_spec,
                   pl.BlockSpec((N_EXPERTS, LANES), lambda i: (0, 0))),
        out_shape=(jax.ShapeDtypeStruct((TOP_K, tokens), jnp.int32),
                   jax.ShapeDtypeStruct((TOP_K, tokens), F32),
                   jax.ShapeDtypeStruct((TOP_K, tokens), jnp.int32),
                   jax.ShapeDtypeStruct((N_EXPERTS, LANES), F32)),
        scratch_shapes=[pltpu.VMEM((N_EXPERTS, LANES), F32)],
        compiler_params=pltpu.CompilerParams(
            dimension_semantics=("arbitrary",), vmem_limit_bytes=VMEM_LIMIT),
        name="router",
    )(mid, wr_hi, wr_lo, br, before)


def _dest_kernel(eidx_ref, rank_ref, starts_ref, dest_ref):
    tt = MOVE_TILE
    tokens = eidx_ref.shape[1]
    ei = lax.broadcasted_iota(jnp.int32, (N_EXPERTS, tokens), 0)
    starts = starts_ref[...]
    for k in range(TOP_K):
        hit = ei == eidx_ref[k:k + 1, :]
        start = jnp.sum(jnp.where(hit, starts, 0.0), axis=0, keepdims=True)
        dest = start.astype(jnp.int32) + rank_ref[k:k + 1, :]
        for c in range(tokens // tt):
            dest_ref[c, k:k + 1, :] = dest[:, c * tt:(c + 1) * tt]


def _dest(eidx, rank, starts):
    tt = MOVE_TILE
    tokens = eidx.shape[1]
    step = DEST_TILE
    tok_spec = pl.BlockSpec((TOP_K, step), lambda i: (0, i))
    return pl.pallas_call(
        _dest_kernel,
        grid=(tokens // step,),
        in_specs=[tok_spec, tok_spec, _const_spec((N_EXPERTS, 1))],
        out_specs=pl.BlockSpec((step // tt, TOP_K, tt), lambda i: (i, 0, 0)),
        out_shape=jax.ShapeDtypeStruct((tokens // tt, TOP_K, tt), jnp.int32),
        compiler_params=pltpu.CompilerParams(dimension_semantics=("arbitrary",)),
        name="dest",
    )(eidx, rank, starts)


def _row_slice(ref, row):
    return ref.at[pl.ds(pl.multiple_of(row * ROW_TILES, ROW_TILES), ROW_TILES), :]


def _dispatch_kernel(dest_ref, rows_ref, xs_ref, zeros_scr, sem, *, n_rows):
    tt = MOVE_TILE

    @pl.when(pl.program_id(0) == 0)
    def _():
        zeros_scr[...] = jnp.zeros_like(zeros_scr)
        tail = pltpu.make_async_copy(
            zeros_scr, xs_ref.at[pl.ds(n_rows * ROW_TILES, EXPERT_BLOCK * ROW_TILES), :], sem)
        tail.start()
        tail.wait()

    for slab in range(DISPATCH_SLABS):
        def issue(t, carry, slab=slab):
            src = _row_slice(rows_ref, slab * tt + t)
            for k in range(TOP_K):
                slot = dest_ref[slab * tt * TOP_K + k * tt + t]
                pltpu.make_async_copy(src, _row_slice(xs_ref, slot), sem).start(priority=k % 2)
            return carry

        lax.fori_loop(0, tt, issue, 0)
    for _ in range(TOP_K):
        pltpu.make_async_copy(
            rows_ref, xs_ref.at[pl.ds(0, DISPATCH_SLABS * tt * ROW_TILES), :], sem).wait()


def _dispatch(dest_flat, rows3):
    tt = MOVE_TILE * DISPATCH_SLABS
    tokens = rows3.shape[0] // ROW_TILES
    n_rows = tokens * TOP_K
    return pl.pallas_call(
        functools.partial(_dispatch_kernel, n_rows=n_rows),
        grid=(tokens // tt,),
        in_specs=[pl.BlockSpec((tt * TOP_K,), lambda i: (i,), memory_space=pltpu.SMEM),
                  pl.BlockSpec((tt * ROW_TILES, LANES), lambda i: (i, 0))],
        out_specs=pl.BlockSpec(memory_space=pl.ANY),
        out_shape=jax.ShapeDtypeStruct(((n_rows + EXPERT_BLOCK) * ROW_TILES, LANES), F32),
        scratch_shapes=[pltpu.VMEM((EXPERT_BLOCK * ROW_TILES, LANES), F32),
                        pltpu.SemaphoreType.DMA(())],
        compiler_params=pltpu.CompilerParams(dimension_semantics=("arbitrary",)),
        name="dispatch",
    )(dest_flat, rows3)


_PIECES = tuple(EXPERT_BLOCK >> s for s in range(EXPERT_BLOCK.bit_length()))
CHUNKS_PER_STEP = 2
ROWS_AHEAD = 6
ROW_SLOTS = ROWS_AHEAD + CHUNKS_PER_STEP
OUT_SLOTS = 2 * CHUNKS_PER_STEP


def _expert_kernel(cexp_ref, crow_ref, cn_ref, cnew_ref, cnext_ref, cw_ref, nsteps_ref,
                   xs_hbm, wg_hbm, wu_hbm, wd_hbm, ys_hbm,
                   xbuf, ybuf, wg_buf, wu_buf, wd_buf, wgb, wub, wdb, xsem, ysem, wsem):
    bm = EXPERT_BLOCK
    s = pl.program_id(0)
    n_steps = nsteps_ref[0]
    n_chunks = n_steps * CHUNKS_PER_STEP
    chunks = [s * CHUNKS_PER_STEP + c for c in range(CHUNKS_PER_STEP)]

    def weight_copies(e, s):
        return [pltpu.make_async_copy(hbm.at[e], buf.at[s], wsem.at[s])
                for hbm, buf in ((wg_hbm, wg_buf), (wu_hbm, wu_buf), (wd_hbm, wd_buf))]

    def rows_in(j, s):
        first = pl.multiple_of(crow_ref[j] * ROW_TILES, ROW_TILES)
        return pltpu.make_async_copy(xs_hbm.at[pl.ds(first, bm * ROW_TILES), :], xbuf.at[s],
                                     xsem.at[s])

    def rows_out(j, s, act):
        n, row0 = cn_ref[j], crow_ref[j]

        def piece(done, p):
            src = pl.multiple_of(done * ROW_TILES, ROW_TILES)
            dst = pl.multiple_of((row0 + done) * ROW_TILES, ROW_TILES)
            act(pltpu.make_async_copy(ybuf.at[s, pl.ds(src, p * ROW_TILES), :],
                                      ys_hbm.at[pl.ds(dst, p * ROW_TILES), :], ysem.at[s]))

        @pl.when(n == bm)
        def _():
            piece(jnp.int32(0), bm)

        @pl.when(n < bm)
        def _():
            done = jnp.int32(0)
            for p in _PIECES[1:]:
                has = (n & p) != 0

                @pl.when(has)
                def _(done=done, p=p):
                    piece(done, p)

                done = done + jnp.where(has, p, 0)

    start = lambda c: c.start()
    wait = lambda c: c.wait()

    @pl.when(s == 0)
    def _():
        for d in range(ROWS_AHEAD):
            @pl.when(d < n_chunks)
            def _(d=d):
                rows_in(d, d).start()
        for c in weight_copies(cexp_ref[0], 0):
            c.start()

    @pl.when(s < n_steps)
    def _():
        for j in chunks:
            slot = cnew_ref[j]

            @pl.when(slot >= 0)
            def _(j=j, slot=slot):
                for c in weight_copies(cexp_ref[j], slot):
                    c.wait()
                nxt = cnext_ref[j]

                @pl.when(nxt >= 0)
                def _():
                    for c in weight_copies(nxt, 1 - slot):
                        c.start()

                wgb[slot] = wg_buf[slot].astype(BF16)
                wub[slot] = wu_buf[slot].astype(BF16)
                wdb[slot] = wd_buf[slot].astype(BF16)

        for j in chunks:
            @pl.when(j + ROWS_AHEAD < n_chunks)
            def _(j=j):
                rows_in(j + ROWS_AHEAD, (j + ROWS_AHEAD) % ROW_SLOTS).start()

        for j in chunks:
            rows_in(j, j % ROW_SLOTS).wait()

        for j in chunks:
            @pl.when(j >= OUT_SLOTS)
            def _(j=j):
                rows_out(j - OUT_SLOTS, j % OUT_SLOTS, wait)

        for j in chunks:
            xs_slot, w_slot, y_slot = j % ROW_SLOTS, cw_ref[j], j % OUT_SLOTS
            x = jnp.concatenate(
                [xbuf[xs_slot, pl.ds(r, bm, stride=ROW_TILES), :] for r in range(ROW_TILES)],
                axis=1)
            xb = x.astype(BF16)
            g = _dot(xb, wgb[w_slot])
            u = _dot(xb, wub[w_slot])
            h = (g * jax.nn.sigmoid(g) * u).astype(BF16)
            y = _dot(h, wdb[w_slot])
            for r in range(ROW_TILES):
                ybuf[y_slot, pl.ds(r, bm, stride=ROW_TILES), :] = y[:, r * LANES:(r + 1) * LANES]

        for j in chunks:
            rows_out(j, j % OUT_SLOTS, start)

        @pl.when(s == n_steps - 1)
        def _():
            for d in range(OUT_SLOTS):
                last = n_chunks - 1 - d

                @pl.when(last >= 0)
                def _(last=last):
                    rows_out(last, last % OUT_SLOTS, wait)


def _experts(cexp, crow, cn, cnew, cnext, cw, nsteps, xs, wg, wu, wd, n_rows):
    bm = EXPERT_BLOCK
    any_spec = pl.BlockSpec(memory_space=pl.ANY)
    grid_spec = pltpu.PrefetchScalarGridSpec(
        num_scalar_prefetch=7,
        grid=(cexp.shape[0] // CHUNKS_PER_STEP,),
        in_specs=[any_spec] * 4,
        out_specs=any_spec,
        scratch_shapes=[pltpu.VMEM((ROW_SLOTS, bm * ROW_TILES, LANES), F32),
                        pltpu.VMEM((OUT_SLOTS, bm * ROW_TILES, LANES), F32),
                        pltpu.VMEM((2, D_MODEL, D_EXPERT), F32),
                        pltpu.VMEM((2, D_MODEL, D_EXPERT), F32),
                        pltpu.VMEM((2, D_EXPERT, D_MODEL), F32),
                        pltpu.VMEM((2, D_MODEL, D_EXPERT), BF16),
                        pltpu.VMEM((2, D_MODEL, D_EXPERT), BF16),
                        pltpu.VMEM((2, D_EXPERT, D_MODEL), BF16),
                        pltpu.SemaphoreType.DMA((ROW_SLOTS,)),
                        pltpu.SemaphoreType.DMA((OUT_SLOTS,)),
                        pltpu.SemaphoreType.DMA((2,))],
    )
    return pl.pallas_call(
        _expert_kernel,
        grid_spec=grid_spec,
        out_shape=jax.ShapeDtypeStruct((n_rows * ROW_TILES, LANES), F32),
        compiler_params=pltpu.CompilerParams(
            dimension_semantics=("arbitrary",), vmem_limit_bytes=VMEM_LIMIT),
        name="experts",
    )(cexp, crow, cn, cnew, cnext, cw, nsteps, xs, wg, wu, wd)


GATHER_SLOTS = 4


def _combine_kernel(dest_ref, dest_next_ref, ys_ref, gate_ref, mid_ref, wsg_ref, wsu_ref,
                    wsd_ref, g2_ref, b2_ref, out_a_ref, out_b_ref, *scratch, alpha, steps_a):
    tt = MOVE_TILE
    ns = GATHER_SLOTS
    per_tile = tt * TOP_K
    g = pl.program_id(0)
    ng = pl.num_programs(0)
    bufs, (gate_scr, routed_scr, sem) = scratch[:ns], scratch[ns:]

    def row_copy(slots_ref, tile, b, t, k):
        src = _row_slice(ys_ref, slots_ref[tile * per_tile + k * tt + t])
        if isinstance(t, int):
            dst = bufs[b].at[pl.ds((k * tt + t) * ROW_TILES, ROW_TILES), :]
        else:
            dst = _row_slice(bufs[b], k * tt + t)
        return pltpu.make_async_copy(src, dst, sem.at[b])

    def wait_rows(b):
        pltpu.make_async_copy(ys_ref.at[pl.ds(0, per_tile * ROW_TILES), :], bufs[b],
                              sem.at[b]).wait()

    @pl.when(g == 0)
    def _():
        for r in range(ns - 1):
            def issue(t, carry, r=r):
                for k in range(TOP_K):
                    row_copy(dest_ref, r, r, t, k).start(priority=k % 2)
                return carry
            lax.fori_loop(0, tt, issue, 0)

    per_block = tt // (ROW_TILES * TOP_K)
    for r in range(ns):
        rows = slice(r * tt, (r + 1) * tt)
        ahead_ref, ahead_tile = (dest_ref, ns - 1) if r == 0 else (dest_next_ref, r - 1)
        ahead_buf = (r + ns - 1) % ns

        mid = mid_ref[rows, :]
        xb = mid.astype(BF16)
        gs = _dot(xb, wsg_ref[...])
        h = (gs * jax.nn.sigmoid(gs) * _dot(xb, wsu_ref[...])).astype(BF16)
        acc = alpha * mid + _dot(h, wsd_ref[...])

        wait_rows(r)
        gates = gate_ref[rows, :]
        for k in range(TOP_K):
            gate_scr[k] = jnp.broadcast_to(gates[:, k:k + 1], (tt, LANES))
        for j in range(ROW_TILES):
            part = jnp.zeros((tt, LANES), F32)
            for k in range(TOP_K):
                part = part + gate_scr[k] * bufs[r][pl.ds(k * tt * ROW_TILES + j, tt,
                                                           stride=ROW_TILES), :]
                t0 = (j * TOP_K + k) * per_block
                for t in range(t0, t0 + per_block):
                    for kk in range(TOP_K):
                        row_copy(ahead_ref, ahead_tile, ahead_buf, t, kk).start(priority=kk % 2)
            routed_scr[:, j * LANES:(j + 1) * LANES] = part
        out = _layer_norm(acc + routed_scr[...], g2_ref[...], b2_ref[...])

        @pl.when(g < steps_a)
        def _(out=out, rows=rows):
            out_a_ref[rows, :] = out

        @pl.when(g >= steps_a)
        def _(out=out, rows=rows):
            out_b_ref[rows, :] = out

    @pl.when(g == ng - 1)
    def _():
        for b in range(ns - 1):
            wait_rows(b)


def _combine(dest_flat, ys, gate, mid, wsg, wsu, wsd, g2, b2, *, alpha, rows_a):
    tt = MOVE_TILE
    rows = tt * GATHER_SLOTS
    tokens = mid.shape[0]
    n = tokens // rows
    steps_a = rows_a // rows
    consts = [wsg, wsu, wsd, g2, b2]
    slots_spec = lambda f: pl.BlockSpec((rows * TOP_K,), f, memory_space=pltpu.SMEM)
    return pl.pallas_call(
        functools.partial(_combine_kernel, alpha=alpha, steps_a=steps_a),
        grid=(n,),
        in_specs=[slots_spec(lambda i: (i,)),
                  slots_spec(lambda i: (jnp.minimum(i + 1, n - 1),)),
                  pl.BlockSpec(memory_space=pl.ANY),
                  pl.BlockSpec((rows, TOP_K), lambda i: (i, 0)),
                  pl.BlockSpec((rows, D_MODEL), lambda i: (i, 0))]
                 + [_const_spec(c.shape) for c in consts],
        out_specs=(pl.BlockSpec((rows, D_MODEL), lambda i: (jnp.minimum(i, steps_a - 1), 0)),
                   pl.BlockSpec((rows, D_MODEL), lambda i: (jnp.maximum(i - steps_a, 0), 0))),
        out_shape=(jax.ShapeDtypeStruct((rows_a, D_MODEL), F32),
                   jax.ShapeDtypeStruct((tokens - rows_a, D_MODEL), F32)),
        scratch_shapes=[pltpu.VMEM((TOP_K * tt * ROW_TILES, LANES), F32)] * GATHER_SLOTS
                       + [pltpu.VMEM((TOP_K, tt, LANES), F32),
                          pltpu.VMEM((tt, D_MODEL), F32),
                          pltpu.SemaphoreType.DMA((GATHER_SLOTS,))],
        compiler_params=pltpu.CompilerParams(
            dimension_semantics=("arbitrary",), vmem_limit_bytes=VMEM_LIMIT),
        name="combine",
    )(dest_flat, dest_flat, ys, gate, mid, *consts)


def _tri(n, *, lower):
    r = lax.broadcasted_iota(jnp.int32, (n, n), 0)
    c = lax.broadcasted_iota(jnp.int32, (n, n), 1)
    return jnp.where((c <= r) if lower else (r <= c), 1.0, 0.0).astype(BF16)


def _moe(mid, rows3, w_router, b_router, w_e_gate, w_e_up, w_e_down, wsg, wsu, wsd, g2, b2, alpha,
         rows_a):
    tokens = mid.shape[0]
    bm = EXPERT_BLOCK
    wr_t = w_router.T
    wr_hi = wr_t.astype(BF16)
    wr_lo = (wr_t - wr_hi.astype(F32)).astype(BF16)
    r = lax.broadcasted_iota(jnp.int32, (ROUTER_TILE, ROUTER_TILE), 0)
    c = lax.broadcasted_iota(jnp.int32, (ROUTER_TILE, ROUTER_TILE), 1)
    before = jnp.where(r < c, 1.0, 0.0).astype(BF16)
    eidx, gate, rank, cnt = _router(mid, wr_hi, wr_lo, b_router.reshape(N_EXPERTS, 1), before)

    i32 = lambda a: a.astype(jnp.int32)
    experts = jnp.arange(N_EXPERTS, dtype=jnp.int32)
    counts = i32(cnt[:, 0])
    starts = jnp.cumsum(counts) - counts
    n_ch = (counts + bm - 1) // bm
    ch_ends = jnp.cumsum(n_ch)
    ch_starts = ch_ends - n_ch
    n_used = ch_ends[-1]
    max_chunks = tokens * TOP_K // bm + N_EXPERTS
    ci = jnp.arange(max_chunks, dtype=jnp.int32)
    cc = jnp.minimum(ci, n_used - 1)
    cexp = jnp.minimum(jnp.sum(i32(ch_ends[None, :] <= cc[:, None]), axis=1), N_EXPERTS - 1)
    hot = cexp[:, None] == experts[None, :]
    lookup = lambda table: jnp.sum(jnp.where(hot, table[None, :], 0), axis=1)
    k_in_expert = cc - lookup(ch_starts)
    crow = lookup(starts) + k_in_expert * bm
    used = ci < n_used
    cn = jnp.where(used, jnp.clip(lookup(counts) - k_in_expert * bm, 0, bm), 0)
    is_first = used & (k_in_expert == 0)
    cw = (jnp.cumsum(i32(is_first)) - 1) % 2
    cnew = jnp.where(is_first, cw, -1)
    later = (experts[None, :] > experts[:, None]) & (n_ch[None, :] > 0)
    next_expert = jnp.min(jnp.where(later, experts[None, :], N_EXPERTS), axis=1)
    cnext = lookup(jnp.where(next_expert < N_EXPERTS, next_expert, -1))
    cnext = jnp.where(is_first, cnext, -1)
    dest = _dest(eidx, rank, starts.astype(F32).reshape(N_EXPERTS, 1)).reshape(tokens * TOP_K)

    xs = _dispatch(dest, rows3)
    n_steps = (n_used + CHUNKS_PER_STEP - 1) // CHUNKS_PER_STEP
    ys = _experts(cexp, i32(crow), i32(cn), i32(cnew), i32(cnext), i32(cw),
                  i32(n_steps).reshape(1), xs, w_e_gate, w_e_up, w_e_down, tokens * TOP_K)
    return _combine(dest, ys, gate.T, mid, wsg, wsu, wsd, g2, b2, alpha=alpha, rows_a=rows_a)


def kernel(x_prompt, x_sample, cache_k, cache_v, cache_logf, state_conv, w_in, b_in, conv_w,
           conv_b, conv_ln_g, conv_ln_b, w_a, w_b, b_b, w_out, ln1_g, ln1_b, w_router, b_router,
           w_e_gate, w_e_up, w_e_down, w_s_gate, w_s_up, w_s_down, ln2_g, ln2_b):
    depth = w_in.shape[0]
    alpha = float((2 * depth) ** 0.25)
    batch, seq, _ = x_prompt.shape
    dbatch, dseq, _ = x_sample.shape
    past = cache_k.shape[2]
    rows_p, rows_s = batch * seq, dbatch * dseq
    total = rows_p + rows_s
    assert seq % IN_TILE == 0 and rows_s % IN_TILE == 0 and seq % MERGE_TILE == 0
    assert total % ROUTER_TILE == 0 and dseq == HIST and rows_p % dseq == 0
    assert total % DEST_TILE == 0 and (total * TOP_K) % EXPERT_BLOCK == 0
    assert rows_p % (MOVE_TILE * GATHER_SLOTS) == 0 and rows_s % (MOVE_TILE * GATHER_SLOTS) == 0

    hp = x_prompt.reshape(rows_p, D_MODEL)
    hs = x_sample.reshape(rows_s, D_MODEL)
    tri_in = _tri(IN_TILE, lower=True)
    upper_past = _tri(past, lower=False)
    row2 = lambda a: a.reshape(1, -1)
    outs = {n: [] for n in ("kp", "vp", "fp", "cp", "ks", "vs", "fs", "cs")}

    for l in range(depth):
        w = w_in[l]
        b = b_in[l]
        main_cols = lambda a: jnp.concatenate([a[..., :OFF_F], a[..., OFF_GLU:]], axis=-1)
        w_main = main_cols(w).astype(BF16)
        b_main = row2(main_cols(b))
        w_f = jnp.pad(w[:, OFF_F:OFF_GLU], ((0, 0), (0, LANES - FOX_HEADS))).astype(BF16)
        b_f = row2(jnp.pad(b[OFF_F:OFF_GLU], (0, LANES - FOX_HEADS)))
        cw = jnp.pad(conv_w[l], ((0, 1), (0, 0)))
        conv_p = (cw, row2(conv_b[l]), row2(conv_ln_g[l]), row2(conv_ln_b[l]),
                  w_b[l].astype(BF16), row2(b_b[l]))
        wa, wout = w_a[l].astype(BF16), w_out[l].astype(BF16)
        g1, b1 = row2(ln1_g[l]), row2(ln1_b[l])

        q, k, v, kb, vb, logf, c, u, sa, sb = _inproj(
            hp, w_main, b_main, w_f, b_f, tri_in, tiles_per_seq=seq // IN_TILE)
        c_row = c.reshape(batch, seq, FOX_HEADS).transpose(0, 2, 1)
        attn = _attn_prompt(q, kb, vb, c_row, batch=batch, seq=seq)
        mid, rows3 = _merge(u, u, attn, sa, sb, hp, conv_p, wa, wout, g1, b1,
                            n_seq=batch, seq=seq, ts=MERGE_TILE, parts=MERGE_PARTS,
                            hist_from_u=True, alpha=alpha, total_rows=total, row_offset=0)
        outs["kp"].append(k.reshape(batch, seq, FOX_HEADS, HEAD_DIM))
        outs["vp"].append(v.reshape(batch, seq, FOX_HEADS, HEAD_DIM))
        outs["fp"].append(logf.reshape(batch, seq, FOX_HEADS))
        outs["cp"].append(u.reshape(batch, seq, CONV_CH)[:, seq - (CONV_WIDTH - 1):])

        q, k, v, kb, vb, logf, _, u, sa, sb = _inproj(
            hs, w_main, b_main, w_f, b_f, tri_in, tiles_per_seq=1)
        logf_t = logf.reshape(dbatch, dseq, FOX_HEADS).transpose(0, 2, 1)
        attn = _attn_sample(
            q, kb, vb, logf, logf_t, cache_k[l].reshape(dbatch, past, FOX_WIDTH),
            cache_v[l].reshape(dbatch, past, FOX_WIDTH), cache_logf[l].transpose(0, 2, 1),
            upper_past, batch=dbatch, t=dseq, past=past)
        hist = jnp.pad(state_conv[l], ((0, 0), (HIST - (CONV_WIDTH - 1), 0), (0, 0)))
        mid, rows3 = _merge(u, hist.reshape(dbatch * HIST, CONV_CH), attn, sa, sb, hs, conv_p,
                            wa, wout, g1, b1, n_seq=dbatch, seq=dseq, ts=dseq, parts=1,
                            hist_from_u=False,
                            alpha=alpha, total_rows=total, row_offset=rows_p, prev=(mid, rows3))
        outs["ks"].append(k.reshape(dbatch, dseq, FOX_HEADS, HEAD_DIM))
        outs["vs"].append(v.reshape(dbatch, dseq, FOX_HEADS, HEAD_DIM))
        outs["fs"].append(logf.reshape(dbatch, dseq, FOX_HEADS))
        u3 = u.reshape(dbatch, dseq, CONV_CH)
        u_ext = jnp.concatenate([state_conv[l], u3], axis=1)
        outs["cs"].append(u_ext[:, -(CONV_WIDTH - 1):])

        hp, hs = _moe(mid, rows3, w_router[l], b_router[l], w_e_gate[l], w_e_up[l], w_e_down[l],
                      w_s_gate[l].astype(BF16), w_s_up[l].astype(BF16), w_s_down[l].astype(BF16),
                      row2(ln2_g[l]), row2(ln2_b[l]), alpha, rows_p)

    st = lambda n: jnp.stack(outs[n])
    return (hp.reshape(batch, seq, D_MODEL), hs.reshape(dbatch, dseq, D_MODEL),
            st("kp"), st("vp"), st("fp"), st("cp"), st("ks"), st("vs"), st("fs"), st("cs"))
```
